```python
import jax, jax.numpy as jnp
from jax import lax
import numpy as np

D_MODEL = 1024
BATCH = 8
SEQ = 4096
DEPTH = 4

GRID_W = 64
CTX_LEN = 256
N_MIXERS = 3
D_FF = 2816
EPS = 1e-6
Q_BLOCK = 128
ROPE_THETA = 10000.0
N_MOD = 9

A_HEADS = 8
A_NOPE = 128
A_ROPE = 64
A_V = 128
A_Q_RANK = 384
A_KV_RANK = 256

B_HEADS = 8
B_KV_HEADS = 2
B_HEAD_DIM = 128
B_GROUP = B_HEADS // B_KV_HEADS

C_HEADS = 16
C_HEAD_DIM = 64
C_WIN_ROWS = 8
C_WIN_COLS = 16

N_A_LAYERS = (DEPTH + 2) // 3
N_B_LAYERS = (DEPTH + 1) // 3
N_C_LAYERS = DEPTH // 3

kernel_name = 'hybrid_mla_gqa_natten_macaron_prefix_dit'


def rmsnorm(x, g):
    xf = x.astype(jnp.float32)
    y = xf * lax.rsqrt(jnp.mean(xf * xf, axis=-1, keepdims=True) + EPS)
    return (y * g.astype(jnp.float32)).astype(x.dtype)


def axial_rope(rows, rot_dim):
    n = rot_dim // 4
    inv_freq = ROPE_THETA ** (-jnp.arange(n, dtype=jnp.float32) / n)
    t = jnp.arange(rows * GRID_W, dtype=jnp.int32)
    r = (t // GRID_W).astype(jnp.float32)
    col = (t % GRID_W).astype(jnp.float32)
    ang = jnp.concatenate([r[:, None] * inv_freq[None, :], col[:, None] * inv_freq[None, :]], axis=-1)
    return jnp.cos(ang), jnp.sin(ang)


def apply_rope(x, cos, sin):
    half = x.shape[-1] // 2
    xf = x.astype(jnp.float32)
    x1, x2 = xf[..., :half], xf[..., half:]
    cs, sn = cos[None, :, None, :], sin[None, :, None, :]
    return jnp.concatenate([x1 * cs - x2 * sn, x2 * cs + x1 * sn], axis=-1).astype(x.dtype)


def blocked_attention(q, k, v, scale):
    b, t, hk, g, dk = q.shape
    nb = t // Q_BLOCK
    qb = jnp.moveaxis(q.reshape(b, nb, Q_BLOCK, hk, g, dk), 1, 0)

    def one_block(qi):
        s = jnp.einsum('bqhgd,bshd->bhgqs', qi, k, preferred_element_type=jnp.float32) * scale
        p = jax.nn.softmax(s, axis=-1).astype(v.dtype)
        return jnp.einsum('bhgqs,bshd->bqhgd', p, v)

    o = lax.map(one_block, qb)
    return jnp.moveaxis(o, 0, 1).reshape(b, t, hk * g * v.shape[-1])


def half_ffn(x, g, shift, scale, gate, w13, w2):
    h = rmsnorm(x, g) * (1 + scale) + shift
    hg, hu = jnp.split(h @ w13, 2, axis=-1)
    return x + 0.5 * gate * ((jax.nn.silu(hg) * hu) @ w2)


def mla_qkv(h, w_in, q_norm_g, kv_norm_g, w_uq, w_ukv, rope, with_q):
    b, t, _ = h.shape
    if with_q:
        proj = h @ w_in
        c_q, rest = proj[..., :A_Q_RANK], proj[..., A_Q_RANK:]
    else:
        rest = h @ w_in[:, A_Q_RANK:]
    c_kv, k_r = rest[..., :A_KV_RANK], rest[..., A_KV_RANK:]
    kv = (rmsnorm(c_kv, kv_norm_g) @ w_ukv).reshape(b, t, A_HEADS, A_NOPE + A_V)
    k_nope, v = kv[..., :A_NOPE], kv[..., A_NOPE:]
    k_r = k_r[:, :, None, :]
    if rope is not None:
        k_r = apply_rope(k_r, *rope)
    k = jnp.concatenate([k_nope, jnp.broadcast_to(k_r, (b, t, A_HEADS, A_ROPE))], axis=-1)
    q = None
    if with_q:
        q = (rmsnorm(c_q, q_norm_g) @ w_uq).reshape(b, t, A_HEADS, A_NOPE + A_ROPE)
        if rope is not None:
            q = jnp.concatenate([q[..., :A_NOPE], apply_rope(q[..., A_NOPE:], *rope)], axis=-1)
    return q, k, v


def mixer_mla(h, hc, w_in, q_norm_g, kv_norm_g, w_uq, w_ukv, w_o, rope, ctx_out):
    scale = (A_NOPE + A_ROPE) ** -0.5
    q, k, v = mla_qkv(h, w_in, q_norm_g, kv_norm_g, w_uq, w_ukv, rope, True)
    qc, kc, vc = mla_qkv(hc, w_in, q_norm_g, kv_norm_g, w_uq, w_ukv, None, ctx_out)
    o_lat = blocked_attention(q[:, :, :, None], jnp.concatenate([kc, k], axis=1),
                              jnp.concatenate([vc, v], axis=1), scale) @ w_o
    o_ctx = None
    if ctx_out:
        o_ctx = blocked_attention(qc[:, :, :, None], kc, vc, scale) @ w_o
    return o_lat, o_ctx


def gqa_qkv(h, w_qkv, q_norm_g, k_norm_g, rope, with_q):
    b, t, _ = h.shape
    qw = B_HEADS * B_HEAD_DIM
    kw = B_KV_HEADS * B_HEAD_DIM
    if with_q:
        proj = h @ w_qkv
        q, rest = proj[..., :qw], proj[..., qw:]
    else:
        rest = h @ w_qkv[:, qw:]
    k = rmsnorm(rest[..., :kw].reshape(b, t, B_KV_HEADS, B_HEAD_DIM), k_norm_g)
    v = rest[..., kw:].reshape(b, t, B_KV_HEADS, B_HEAD_DIM)
    if rope is not None:
        k = apply_rope(k, *rope)
    if with_q:
        q = rmsnorm(q.reshape(b, t, B_HEADS, B_HEAD_DIM), q_norm_g)
        if rope is not None:
            q = apply_rope(q, *rope)
        q = q.reshape(b, t, B_KV_HEADS, B_GROUP, B_HEAD_DIM)
    else:
        q = None
    return q, k, v


def mixer_gqa(h, hc, w_qkv, q_norm_g, k_norm_g, w_o, rope, ctx_out):
    scale = B_HEAD_DIM ** -0.5
    q, k, v = gqa_qkv(h, w_qkv, q_norm_g, k_norm_g, rope, True)
    qc, kc, vc = gqa_qkv(hc, w_qkv, q_norm_g, k_norm_g, None, ctx_out)
    o_lat = blocked_attention(q, jnp.concatenate([kc, k], axis=1),
                              jnp.concatenate([vc, v], axis=1), scale) @ w_o
    o_ctx = None
    if ctx_out:
        o_ctx = blocked_attention(qc, kc, vc, scale) @ w_o
    return o_lat, o_ctx


def mixer_neighbourhood(h, hc, w_qkv, rpb, w_o, ctx_out):
    b, t, _ = h.shape
    rows = t // GRID_W
    kr = min(C_WIN_ROWS, rows)
    hd = C_HEADS * C_HEAD_DIM
    scale = C_HEAD_DIM ** -0.5
    proj = h @ w_qkv
    qg = proj[..., :hd].reshape(b, rows, GRID_W, C_HEADS, C_HEAD_DIM)
    kg = proj[..., hd:2 * hd].reshape(b, rows, GRID_W, C_HEADS, C_HEAD_DIM)
    vg = proj[..., 2 * hd:].reshape(b, rows, GRID_W, C_HEADS, C_HEAD_DIM)
    tc = hc.shape[1]
    if ctx_out:
        pc = hc @ w_qkv
        qc = pc[..., :hd].reshape(b, tc, C_HEADS, C_HEAD_DIM)
        pc = pc[..., hd:]
    else:
        pc = hc @ w_qkv[:, hd:]
    kc = pc[..., :hd].reshape(b, tc, C_HEADS, C_HEAD_DIM)
    vc = pc[..., hd:].reshape(b, tc, C_HEADS, C_HEAD_DIM)

    cols = jnp.arange(GRID_W, dtype=jnp.int32)
    col_start = jnp.clip(cols - C_WIN_COLS // 2, 0, GRID_W - C_WIN_COLS)
    col_idx = col_start[:, None] + jnp.arange(C_WIN_COLS, dtype=jnp.int32)[None, :]
    col_off = col_idx - cols[:, None] + (C_WIN_COLS - 1)

    def one_row(r):
        r0 = jnp.clip(r - kr // 2, 0, rows - kr)
        k_rows = lax.dynamic_slice_in_dim(kg, r0, kr, axis=1)
        v_rows = lax.dynamic_slice_in_dim(vg, r0, kr, axis=1)
        k_win = k_rows[:, :, col_idx]
        v_win = v_rows[:, :, col_idx]
        q_r = lax.dynamic_index_in_dim(qg, r, axis=1, keepdims=False)
        s_loc = jnp.einsum('bqhd,bkqwhd->bhqkw', q_r, k_win, preferred_element_type=jnp.float32) * scale
        row_off = r0 + jnp.arange(kr, dtype=jnp.int32) - r + (C_WIN_ROWS - 1)
        bias = rpb[:, row_off[:, None, None], col_off[None, :, :]]
        s_loc = s_loc + jnp.transpose(bias, (0, 2, 1, 3)).astype(jnp.float32)[None]
        s_ctx = jnp.einsum('bqhd,bchd->bhqc', q_r, kc, preferred_element_type=jnp.float32) * scale
        s = jnp.concatenate([s_ctx, s_loc.reshape(b, C_HEADS, GRID_W, kr * C_WIN_COLS)], axis=-1)
        p = jax.nn.softmax(s, axis=-1).astype(vg.dtype)
        p_ctx = p[..., :tc]
        p_loc = p[..., tc:].reshape(b, C_HEADS, GRID_W, kr, C_WIN_COLS)
        return (jnp.einsum('bhqc,bchd->bqhd', p_ctx, vc)
                + jnp.einsum('bhqkw,bkqwhd->bqhd', p_loc, v_win))

    o = lax.map(one_row, jnp.arange(rows, dtype=jnp.int32))
    o_lat = jnp.moveaxis(o, 0, 1).reshape(b, t, hd) @ w_o
    o_ctx = None
    if ctx_out:
        o_ctx = blocked_attention(qc[:, :, :, None], kc, vc, scale) @ w_o
    return o_lat, o_ctx


def _dense(k, shape, fan_in, gain=1.0):
    return jax.random.normal(k, shape, jnp.float32) * (gain * fan_in ** -0.5)


def _gain(k, shape):
    return 1.0 + 0.02 * jax.random.normal(k, shape, jnp.float32)


def setup_inputs(seed: int = 0) -> dict:
    key = jax.random.key(seed)
    ks = jax.random.split(key, 32)
    D = D_MODEL
    return {
        'x': jax.random.normal(ks[0], (BATCH, SEQ, D), jnp.float32),
        'c': jax.random.normal(ks[1], (BATCH, D), jnp.float32),
        'ctx': jax.random.normal(ks[2], (BATCH, CTX_LEN, D), jnp.float32),
        'c_ctx': jax.random.normal(ks[3], (D,), jnp.float32),
        'norm_g': _gain(ks[4], (DEPTH, 3, D)),
        'w_mod': _dense(ks[5], (DEPTH, D, N_MOD * D), D, 0.5),
        'b_mod': 0.02 * jax.random.normal(ks[6], (DEPTH, N_MOD * D), jnp.float32),
        'ffn1_w13': _dense(ks[7], (DEPTH, D, 2 * D_FF), D),
        'ffn1_w2': _dense(ks[8], (DEPTH, D_FF, D), D_FF),
        'ffn2_w13': _dense(ks[9], (DEPTH, D, 2 * D_FF), D),
        'ffn2_w2': _dense(ks[10], (DEPTH, D_FF, D), D_FF),
        'a_w_in': _dense(ks[11], (N_A_LAYERS, D, A_Q_RANK + A_KV_RANK + A_ROPE), D),
        'a_q_norm': _gain(ks[12], (N_A_LAYERS, A_Q_RANK)),
        'a_kv_norm': _gain(ks[13], (N_A_LAYERS, A_KV_RANK)),
        'a_w_uq': _dense(ks[14], (N_A_LAYERS, A_Q_RANK, A_HEADS * (A_NOPE + A_ROPE)), A_Q_RANK),
        'a_w_ukv': _dense(ks[15], (N_A_LAYERS, A_KV_RANK, A_HEADS * (A_NOPE + A_V)), A_KV_RANK),
        'a_w_o': _dense(ks[16], (N_A_LAYERS, A_HEADS * A_V, D), A_HEADS * A_V),
        'b_w_qkv': _dense(ks[17], (N_B_LAYERS, D, (B_HEADS + 2 * B_KV_HEADS) * B_HEAD_DIM), D),
        'b_q_norm': _gain(ks[18], (N_B_LAYERS, B_HEAD_DIM)),
        'b_k_norm': _gain(ks[19], (N_B_LAYERS, B_HEAD_DIM)),
        'b_w_o': _dense(ks[20], (N_B_LAYERS, B_HEADS * B_HEAD_DIM, D), B_HEADS * B_HEAD_DIM),
        'c_w_qkv': _dense(ks[21], (N_C_LAYERS, D, 3 * C_HEADS * C_HEAD_DIM), D),
        'c_rpb': 0.1 * jax.random.normal(ks[22], (N_C_LAYERS, C_HEADS, 2 * C_WIN_ROWS - 1, 2 * C_WIN_COLS - 1), jnp.float32),
        'c_w_o': _dense(ks[23], (N_C_LAYERS, C_HEADS * C_HEAD_DIM, D), C_HEADS * C_HEAD_DIM),
        'final_norm_g': _gain(ks[24], (D,)),
    }


def reference(x, c, ctx, c_ctx, norm_g, w_mod, b_mod, ffn1_w13, ffn1_w2, ffn2_w13, ffn2_w2,
              a_w_in, a_q_norm, a_kv_norm, a_w_uq, a_w_ukv, a_w_o,
              b_w_qkv, b_q_norm, b_k_norm, b_w_o,
              c_w_qkv, c_rpb, c_w_o, final_norm_g):
    rows = x.shape[1] // GRID_W
    rope_a = axial_rope(rows, A_ROPE)
    rope_b = axial_rope(rows, B_HEAD_DIM)
    xc = ctx
    sc = jax.nn.silu(c)[:, None, :]
    scc = jax.nn.silu(c_ctx)
    for i in range(DEPTH):
        ctx_out = i < DEPTH - 1
        m = jnp.split(sc @ w_mod[i] + b_mod[i], N_MOD, axis=-1)
        mc = jnp.split(scc @ w_mod[i] + b_mod[i], N_MOD, axis=-1)
        x = half_ffn(x, norm_g[i, 0], m[0], m[1], m[2], ffn1_w13[i], ffn1_w2[i])
        xc = half_ffn(xc, norm_g[i, 0], mc[0], mc[1], mc[2], ffn1_w13[i], ffn1_w2[i])
        h = rmsnorm(x, norm_g[i, 1]) * (1 + m[4]) + m[3]
        hc = rmsnorm(xc, norm_g[i, 1]) * (1 + mc[4]) + mc[3]
        kind, j = i % N_MIXERS, i // N_MIXERS
        if kind == 0:
            o, oc = mixer_mla(h, hc, a_w_in[j], a_q_norm[j], a_kv_norm[j], a_w_uq[j], a_w_ukv[j], a_w_o[j], rope_a, ctx_out)
        elif kind == 1:
            o, oc = mixer_gqa(h, hc, b_w_qkv[j], b_q_norm[j], b_k_norm[j], b_w_o[j], rope_b, ctx_out)
        else:
            o, oc = mixer_neighbourhood(h, hc, c_w_qkv[j], c_rpb[j], c_w_o[j], ctx_out)
        x = x + m[5] * o
        x = half_ffn(x, norm_g[i, 2], m[6], m[7], m[8], ffn2_w13[i], ffn2_w2[i])
        if ctx_out:
            xc = xc + mc[5] * oc
            xc = half_ffn(xc, norm_g[i, 2], mc[6], mc[7], mc[8], ffn2_w13[i], ffn2_w2[i])
    return rmsnorm(x, final_norm_g)
```

```python
import functools

import jax
import jax.numpy as jnp
from jax import lax
from jax.experimental import pallas as pl
from jax.experimental.pallas import tpu as pltpu

F32 = jnp.float32
BF16 = jnp.bfloat16

GRID_W = 64
N_MIXERS = 3
N_MOD = 9
EPS = 1e-6
ROPE_THETA = 10000.0

A_HEADS, A_NOPE, A_ROPE, A_V = 8, 128, 64, 128
A_Q_RANK, A_KV_RANK = 384, 256
B_HEADS, B_KV_HEADS, B_HEAD_DIM = 8, 2, 128
C_HEADS, C_HEAD_DIM, C_WIN_ROWS, C_WIN_COLS = 16, 64, 8, 16

LANES = 128
MOD_ROWS = 16
MASK_VALUE = -1e30
VMEM_LIMIT = 56 * 1024 * 1024

ROW_TILE = 256
Q_TILE = 256
NB_ROWS = 4
NB_KEY_ROWS = 12


def _params(*sem):
    return pltpu.CompilerParams(dimension_semantics=sem, vmem_limit_bytes=VMEM_LIMIT)


def _const_spec(shape):
    nd = len(shape)
    return pl.BlockSpec(shape, lambda *_: (0,) * nd, pipeline_mode=pl.Buffered(1))


def _dot(a, b):
    return jnp.dot(a, b, preferred_element_type=F32)


def _dot_nt(a, b):
    return lax.dot_general(a, b, (((1,), (1,)), ((), ())), preferred_element_type=F32)


def _rms(x):
    return x * lax.rsqrt(jnp.mean(x * x, axis=-1, keepdims=True) + EPS)


def _modulated_norm(x, g, scale, shift):
    return (_rms(x) * g) * (1.0 + scale) + shift


def _silu(x):
    return x / (1.0 + jnp.exp(-x))


def _mod_kernel(c_ref, w_ref, b_ref, o_ref):
    sc = _silu(c_ref[...]).astype(BF16)
    o_ref[0] = _dot(sc, w_ref[0].astype(BF16)) + b_ref[0]


def _modulation(c_rows, w_mod, b_mod):
    depth, d, n = w_mod.shape
    tn = d
    return pl.pallas_call(
        _mod_kernel,
        grid=(depth, n // tn),
        in_specs=[
            pl.BlockSpec((MOD_ROWS, d), lambda i, j: (0, 0)),
            pl.BlockSpec((1, d, tn), lambda i, j: (i, 0, j)),
            pl.BlockSpec((1, 1, tn), lambda i, j: (i, 0, j)),
        ],
        out_specs=pl.BlockSpec((1, MOD_ROWS, tn), lambda i, j: (i, 0, j)),
        out_shape=jax.ShapeDtypeStruct((depth, MOD_ROWS, n), F32),
        compiler_params=_params("parallel", "parallel"),
        name="modulation",
    )(c_rows, w_mod, b_mod.reshape(depth, 1, n))


def _ffn_kernel(*refs, k0, d_ff, has_oproj, final_norm):
    it = iter(refs)
    x_ref, mod_ref, g_ref, w13_ref, w2_ref = (next(it) for _ in range(5))
    if has_oproj:
        a_ref, wo_ref = next(it), next(it)
    if final_norm:
        fg_ref = next(it)
    out_ref = next(it)

    x = x_ref[0]
    mod = mod_ref[0]
    if has_oproj:
        x = x + mod[5:6] * _dot(a_ref[0], wo_ref[...])
    h = _modulated_norm(x, g_ref[...], mod[k0 + 1:k0 + 2], mod[k0:k0 + 1]).astype(BF16)
    hgu = _dot(h, w13_ref[...])
    act = (_silu(hgu[:, :d_ff]) * hgu[:, d_ff:]).astype(BF16)
    y = x + (0.5 * mod[k0 + 2:k0 + 3]) * _dot(act, w2_ref[...])
    if final_norm:
        y = _rms(y) * fg_ref[...]
    out_ref[0] = y


def _ffn(x, mod, mod_row, g, w13, w2, *, k0, attn=None, w_o=None, final_g=None):
    b, t, d = x.shape
    d_ff = w2.shape[0]
    tm = min(ROW_TILE, t)
    row = lambda bi, ti: (bi, ti, 0)
    in_specs = [
        pl.BlockSpec((1, tm, d), row),
        pl.BlockSpec((1, N_MOD, d), lambda bi, ti: (mod_row(bi), 0, 0)),
        _const_spec((1, d)),
        _const_spec(w13.shape),
        _const_spec(w2.shape),
    ]
    args = [x, mod, g.reshape(1, d), w13, w2]
    if attn is not None:
        in_specs += [pl.BlockSpec((1, tm, attn.shape[2]), row), _const_spec(w_o.shape)]
        args += [attn, w_o]
    if final_g is not None:
        in_specs.append(_const_spec((1, d)))
        args.append(final_g.reshape(1, d))
    return pl.pallas_call(
        functools.partial(_ffn_kernel, k0=k0, d_ff=d_ff, has_oproj=attn is not None,
                          final_norm=final_g is not None),
        grid=(b, t // tm),
        in_specs=in_specs,
        out_specs=pl.BlockSpec((1, tm, d), row),
        out_shape=jax.ShapeDtypeStruct((b, t, d), F32),
        compiler_params=_params("parallel", "parallel"),
        name="half_ffn",
    )(*args)


def _rope_pairs(x, cos, sin_signed, half):
    if 2 * half == LANES:
        rot = pltpu.roll(x, half, 1)
    else:
        lane = lax.broadcasted_iota(jnp.int32, x.shape, 1)
        first = (lane % (2 * half)) < half
        rot = jnp.where(first, pltpu.roll(x, LANES - half, 1), pltpu.roll(x, half, 1))
    return x * cos + rot * sin_signed


def _mla_proj_kernel(*refs, use_rope, sm_scale):
    it = iter(refs)
    x_ref, mod_ref, g_ref, w_in_ref, qg_ref, kvg_ref, w_uq_ref, w_ukv_ref = (next(it) for _ in range(8))
    if use_rope:
        cos_ref, sin_ref = next(it), next(it)
    q_ref, kn_ref, kr_ref, vt_ref = (next(it) for _ in range(4))

    mod = mod_ref[0]
    h = _modulated_norm(x_ref[0], g_ref[...], mod[4:5], mod[3:4]).astype(BF16)
    proj = _dot(h, w_in_ref[...])
    c_q = (_rms(proj[:, :A_Q_RANK]) * qg_ref[...]).astype(BF16)
    c_kv = (_rms(proj[:, A_Q_RANK:A_Q_RANK + A_KV_RANK]) * kvg_ref[...]).astype(BF16)
    k_r = proj[:, A_Q_RANK + A_KV_RANK:]
    q_all = _dot(c_q, w_uq_ref[...])
    kv = _dot(c_kv, w_ukv_ref[...])
    n_nope = A_HEADS * A_NOPE
    if use_rope:
        cos, sin = cos_ref[...], sin_ref[...]
        k_r = _rope_pairs(k_r, cos, sin, A_ROPE // 2)
    lane = lax.broadcasted_iota(jnp.int32, (1, LANES), 1)
    for j in range(A_HEADS // 2):
        qr = q_all[:, n_nope + j * LANES:n_nope + (j + 1) * LANES]
        if use_rope:
            qr = _rope_pairs(qr, cos, sin, A_ROPE // 2)
        qr = qr * sm_scale
        for e in range(2):
            hd = 2 * j + e
            keep = (lane < A_ROPE) if e == 0 else (lane >= A_ROPE)
            q_ref[0, :, 2 * hd * LANES:(2 * hd + 1) * LANES] = (
                q_all[:, hd * A_NOPE:(hd + 1) * A_NOPE] * sm_scale).astype(BF16)
            q_ref[0, :, (2 * hd + 1) * LANES:(2 * hd + 2) * LANES] = jnp.where(keep, qr, 0.0).astype(BF16)
    kn_ref[0] = kv[:, :n_nope].astype(BF16)
    kr_ref[0] = k_r.astype(BF16)
    vt_ref[0] = kv[:, n_nope:].T.astype(BF16)


def _mla_proj(x, mod, mod_row, g, w, rope):
    b, t, d = x.shape
    tm = min(ROW_TILE, t)
    row = lambda bi, ti: (bi, ti, 0)
    in_specs = [
        pl.BlockSpec((1, tm, d), row),
        pl.BlockSpec((1, N_MOD, d), lambda bi, ti: (mod_row(bi), 0, 0)),
        _const_spec((1, d)),
        _const_spec(w["w_in"].shape), _const_spec((1, A_Q_RANK)), _const_spec((1, A_KV_RANK)),
        _const_spec(w["w_uq"].shape), _const_spec(w["w_ukv"].shape),
    ]
    args = [x, mod, g.reshape(1, d), w["w_in"], w["q_norm"].reshape(1, -1), w["kv_norm"].reshape(1, -1),
            w["w_uq"], w["w_ukv"]]
    if rope is not None:
        in_specs += [pl.BlockSpec((tm, LANES), lambda bi, ti: (ti, 0))] * 2
        args += list(rope)
    hv = A_HEADS * A_V
    return pl.pallas_call(
        functools.partial(_mla_proj_kernel, use_rope=rope is not None,
                          sm_scale=float((A_NOPE + A_ROPE) ** -0.5)),
        grid=(b, t // tm),
        in_specs=in_specs,
        out_specs=[
            pl.BlockSpec((1, tm, 2 * LANES * A_HEADS), row),
            pl.BlockSpec((1, tm, A_HEADS * A_NOPE), row),
            pl.BlockSpec((1, tm, LANES), row),
            pl.BlockSpec((1, hv, tm), lambda bi, ti: (bi, 0, ti)),
        ],
        out_shape=[
            jax.ShapeDtypeStruct((b, t, 2 * LANES * A_HEADS), BF16),
            jax.ShapeDtypeStruct((b, t, A_HEADS * A_NOPE), BF16),
            jax.ShapeDtypeStruct((b, t, LANES), BF16),
            jax.ShapeDtypeStruct((b, hv, t), BF16),
        ],
        compiler_params=_params("parallel", "parallel"),
        name="mla_proj",
    )(*args)


def _gqa_proj_kernel(*refs, use_rope, sm_scale):
    it = iter(refs)
    x_ref, mod_ref, g_ref, w_ref, qg_ref, kg_ref = (next(it) for _ in range(6))
    if use_rope:
        cos_ref, sin_ref = next(it), next(it)
    q_ref, k_ref, vt_ref = (next(it) for _ in range(3))

    mod = mod_ref[0]
    h = _modulated_norm(x_ref[0], g_ref[...], mod[4:5], mod[3:4]).astype(BF16)
    proj = _dot(h, w_ref[...])
    qw = B_HEADS * B_HEAD_DIM
    kw = B_KV_HEADS * B_HEAD_DIM
    if use_rope:
        cos, sin = cos_ref[...], sin_ref[...]

    def head(col, gain, scale):
        y = _rms(proj[:, col:col + B_HEAD_DIM]) * gain
        if use_rope:
            y = _rope_pairs(y, cos, sin, B_HEAD_DIM // 2)
        return (y * scale).astype(BF16) if scale != 1.0 else y.astype(BF16)

    for hd in range(B_HEADS):
        q_ref[0, :, hd * B_HEAD_DIM:(hd + 1) * B_HEAD_DIM] = head(hd * B_HEAD_DIM, qg_ref[...], sm_scale)
    for hd in range(B_KV_HEADS):
        k_ref[0, :, hd * B_HEAD_DIM:(hd + 1) * B_HEAD_DIM] = head(qw + hd * B_HEAD_DIM, kg_ref[...], 1.0)
    vt_ref[0] = proj[:, qw + kw:].T.astype(BF16)


def _gqa_proj(x, mod, mod_row, g, w, rope):
    b, t, d = x.shape
    tm = min(ROW_TILE, t)
    row = lambda bi, ti: (bi, ti, 0)
    qw, kw = B_HEADS * B_HEAD_DIM, B_KV_HEADS * B_HEAD_DIM
    in_specs = [
        pl.BlockSpec((1, tm, d), row),
        pl.BlockSpec((1, N_MOD, d), lambda bi, ti: (mod_row(bi), 0, 0)),
        _const_spec((1, d)),
        _const_spec(w["w_qkv"].shape), _const_spec((1, B_HEAD_DIM)), _const_spec((1, B_HEAD_DIM)),
    ]
    args = [x, mod, g.reshape(1, d), w["w_qkv"], w["q_norm"].reshape(1, -1), w["k_norm"].reshape(1, -1)]
    if rope is not None:
        in_specs += [pl.BlockSpec((tm, LANES), lambda bi, ti: (ti, 0))] * 2
        args += list(rope)
    return pl.pallas_call(
        functools.partial(_gqa_proj_kernel, use_rope=rope is not None, sm_scale=float(B_HEAD_DIM ** -0.5)),
        grid=(b, t // tm),
        in_specs=in_specs,
        out_specs=[
            pl.BlockSpec((1, tm, qw), row),
            pl.BlockSpec((1, tm, kw), row),
            pl.BlockSpec((1, kw, tm), lambda bi, ti: (bi, 0, ti)),
        ],
        out_shape=[
            jax.ShapeDtypeStruct((b, t, qw), BF16),
            jax.ShapeDtypeStruct((b, t, kw), BF16),
            jax.ShapeDtypeStruct((b, kw, t), BF16),
        ],
        compiler_params=_params("parallel", "parallel"),
        name="gqa_proj",
    )(*args)


def _nb_proj_kernel(x_ref, mod_ref, g_ref, w_ref, q_ref, k_ref, vt_ref, *, sm_scale):
    mod = mod_ref[0]
    h = _modulated_norm(x_ref[0], g_ref[...], mod[4:5], mod[3:4]).astype(BF16)
    proj = _dot(h, w_ref[...])
    hd = C_HEADS * C_HEAD_DIM
    q_ref[0] = (proj[:, :hd] * sm_scale).astype(BF16)
    k_ref[0] = proj[:, hd:2 * hd].astype(BF16)
    vt_ref[0] = proj[:, 2 * hd:].T.astype(BF16)


def _nb_proj(x, mod, mod_row, g, w):
    b, t, d = x.shape
    tm = min(ROW_TILE, t)
    row = lambda bi, ti: (bi, ti, 0)
    hd = C_HEADS * C_HEAD_DIM
    return pl.pallas_call(
        functools.partial(_nb_proj_kernel, sm_scale=float(C_HEAD_DIM ** -0.5)),
        grid=(b, t // tm),
        in_specs=[
            pl.BlockSpec((1, tm, d), row),
            pl.BlockSpec((1, N_MOD, d), lambda bi, ti: (mod_row(bi), 0, 0)),
            _const_spec((1, d)),
            _const_spec(w["w_qkv"].shape),
        ],
        out_specs=[
            pl.BlockSpec((1, tm, hd), row),
            pl.BlockSpec((1, tm, hd), row),
            pl.BlockSpec((1, hd, tm), lambda bi, ti: (bi, 0, ti)),
        ],
        out_shape=[
            jax.ShapeDtypeStruct((b, t, hd), BF16),
            jax.ShapeDtypeStruct((b, t, hd), BF16),
            jax.ShapeDtypeStruct((b, hd, t), BF16),
        ],
        compiler_params=_params("parallel", "parallel"),
        name="nb_proj",
    )(x, mod, g.reshape(1, d), w["w_qkv"])


def _softmax_pv(s_t, v_t):
    m = jnp.max(s_t, axis=0, keepdims=True)
    p = jnp.exp(s_t - m)
    l = jnp.sum(p, axis=0, keepdims=True)
    return _dot(v_t, p.astype(BF16)) / l


def _attn_kernel(*refs, key_rows, k_pieces):
    n_groups = len(key_rows)
    it = iter(refs)
    q_ref = next(it)
    k_refs = [[next(it) for _ in range(k_pieces)] for _ in range(n_groups)]
    v_refs = [next(it) for _ in range(n_groups)]
    o_ref = next(it)
    k_scr, v_scr = next(it), next(it)

    @pl.when(pl.program_id(2) == 0)
    def _():
        r0 = 0
        for gi, rows in enumerate(key_rows):
            for pi in range(k_pieces):
                k_scr[r0:r0 + rows, pi * LANES:(pi + 1) * LANES] = k_refs[gi][pi][0]
            v_scr[:, r0:r0 + rows] = v_refs[gi][0]
            r0 += rows

    s_t = _dot_nt(k_scr[...], q_ref[0])
    o_ref[0] = _softmax_pv(s_t, v_scr[...]).T.astype(BF16)


def _attention(q, k_groups, v_groups, *, heads, kv_of, k_lane_blocks):
    b, tq, qw = q.shape
    dq = qw // heads
    tile = min(Q_TILE, tq)
    key_rows = tuple(g[0].shape[1] for g in k_groups)
    k_pieces = len(k_groups[0])
    s_total = sum(key_rows)
    in_specs = [pl.BlockSpec((1, tile, dq), lambda bi, h, qi: (bi, qi, h))]
    args = [q]
    for grp in k_groups:
        for pi, arr in enumerate(grp):
            in_specs.append(pl.BlockSpec((1, arr.shape[1], LANES),
                                         lambda bi, h, qi, f=k_lane_blocks[pi]: (bi, 0, f(h))))
            args.append(arr)
    for arr in v_groups:
        in_specs.append(pl.BlockSpec((1, LANES, arr.shape[2]), lambda bi, h, qi: (bi, kv_of(h), 0)))
        args.append(arr)
    return pl.pallas_call(
        functools.partial(_attn_kernel, key_rows=key_rows, k_pieces=k_pieces),
        grid=(b, heads, tq // tile),
        in_specs=in_specs,
        out_specs=pl.BlockSpec((1, tile, LANES), lambda bi, h, qi: (bi, qi, h)),
        out_shape=jax.ShapeDtypeStruct((b, tq, heads * LANES), BF16),
        scratch_shapes=[pltpu.VMEM((s_total, k_pieces * LANES), BF16), pltpu.VMEM((LANES, s_total), BF16)],
        compiler_params=_params("parallel", "parallel", "arbitrary"),
        name="attention",
    )(*args)


def _nb_attn_kernel(*refs, has_window, rows):
    it = iter(refs)
    q_ref, kc_ref, vc_ref = next(it), next(it), next(it)
    if has_window:
        kl_ref, vl_ref, bias_ref = next(it), next(it), next(it)
    o_ref = next(it)

    q = q_ref[0]
    kc = kc_ref[0]
    vc = vc_ref[0]
    if has_window:
        g = pl.program_id(2)
        first_row = jnp.clip(NB_ROWS * g - C_WIN_ROWS // 2, 0, rows - NB_KEY_ROWS)
        start = pl.multiple_of(first_row * GRID_W, 2 * LANES)
        kw = kl_ref[0, pl.ds(start, NB_KEY_ROWS * GRID_W), :]
        vw = vl_ref[0, :, pl.ds(start, NB_KEY_ROWS * GRID_W)]
    lane = lax.broadcasted_iota(jnp.int32, (1, LANES), 1)
    outs = []
    for e in range(2):
        keep = (lane < C_HEAD_DIM) if e == 0 else (lane >= C_HEAD_DIM)
        qe = jnp.where(keep, q, jnp.zeros_like(q))
        sl = slice(e * C_HEAD_DIM, (e + 1) * C_HEAD_DIM)
        s_c = _dot_nt(kc, qe)
        if has_window:
            s_w = _dot_nt(kw, qe) + bias_ref[0, e]
            m = jnp.maximum(jnp.max(s_c, axis=0, keepdims=True), jnp.max(s_w, axis=0, keepdims=True))
            p_c = jnp.exp(s_c - m)
            p_w = jnp.exp(s_w - m)
            l = jnp.sum(p_c, axis=0, keepdims=True) + jnp.sum(p_w, axis=0, keepdims=True)
            o_t = (_dot(vc[sl], p_c.astype(BF16)) + _dot(vw[sl], p_w.astype(BF16))) / l
        else:
            o_t = _softmax_pv(s_c, vc[sl])
        outs.append(o_t)
    o_ref[0] = jnp.concatenate(outs, axis=0).T.astype(BF16)


def _nb_attention(q, k_ctx, vt_ctx, k_lat=None, vt_lat=None, bias=None):
    b, tq, hw = q.shape
    pairs = hw // LANES
    has_window = k_lat is not None
    c = k_ctx.shape[1]
    if has_window:
        t = k_lat.shape[1]
        rows = t // GRID_W
        tile = NB_ROWS * GRID_W
        groups = tq // tile
        variant = lambda g: jnp.where(g == 0, 0, jnp.where(g == groups - 1, 2, 1))
    else:
        rows, tile, groups = 0, tq, 1
    in_specs = [
        pl.BlockSpec((1, tile, LANES), lambda bi, p, g: (bi, g, p)),
        pl.BlockSpec((1, c, LANES), lambda bi, p, g: (bi, 0, p)),
        pl.BlockSpec((1, LANES, c), lambda bi, p, g: (bi, p, 0)),
    ]
    args = [q, k_ctx, vt_ctx]
    if has_window:
        in_specs += [
            pl.BlockSpec((1, t, LANES), lambda bi, p, g: (bi, 0, p)),
            pl.BlockSpec((1, LANES, t), lambda bi, p, g: (bi, p, 0)),
            pl.BlockSpec((1, 2, NB_KEY_ROWS * GRID_W, tile), lambda bi, p, g: (variant(g), p, 0, 0)),
        ]
        args += [k_lat, vt_lat, bias]
    return pl.pallas_call(
        functools.partial(_nb_attn_kernel, has_window=has_window, rows=rows),
        grid=(b, pairs, groups),
        in_specs=in_specs,
        out_specs=pl.BlockSpec((1, tile, LANES), lambda bi, p, g: (bi, g, p)),
        out_shape=jax.ShapeDtypeStruct((b, tq, hw), BF16),
        compiler_params=_params("parallel", "parallel", "arbitrary"),
        name="nb_attention",
    )(*args)


def _nb_bias_table(rpb, rows):
    groups = rows // NB_ROWS
    i = jnp.arange(NB_ROWS, dtype=jnp.int32)
    j = jnp.arange(NB_KEY_ROWS, dtype=jnp.int32)
    cols = jnp.arange(GRID_W, dtype=jnp.int32)
    c0 = jnp.clip(cols - C_WIN_COLS // 2, 0, GRID_W - C_WIN_COLS)
    col_off = cols[:, None] - cols[None, :] + (C_WIN_COLS - 1)
    col_ok = (cols[:, None] >= c0[None, :]) & (cols[:, None] < c0[None, :] + C_WIN_COLS)
    tables = []
    for g in (0, 1, groups - 1):
        qr = NB_ROWS * g + i
        kr = jnp.clip(NB_ROWS * g - C_WIN_ROWS // 2, 0, rows - NB_KEY_ROWS) + j
        r0 = jnp.clip(qr - C_WIN_ROWS // 2, 0, rows - C_WIN_ROWS)
        row_off = kr[:, None] - qr[None, :] + (C_WIN_ROWS - 1)
        row_ok = (kr[:, None] >= r0[None, :]) & (kr[:, None] < r0[None, :] + C_WIN_ROWS)
        ro = jnp.clip(row_off, 0, 2 * C_WIN_ROWS - 2)
        co = jnp.clip(col_off, 0, 2 * C_WIN_COLS - 2)
        vals = rpb[:, ro[:, None, :, None], co[None, :, None, :]]
        ok = row_ok[:, None, :, None] & col_ok[None, :, None, :]
        vals = jnp.where(ok[None], vals, MASK_VALUE)
        tables.append(vals.reshape(rpb.shape[0], NB_KEY_ROWS * GRID_W, NB_ROWS * GRID_W))
    return jnp.stack(tables).astype(F32)


def _rope_tables(rows, rot_dim):
    n = rot_dim // 4
    inv_freq = ROPE_THETA ** (-jnp.arange(n, dtype=F32) / n)
    t = jnp.arange(rows * GRID_W, dtype=jnp.int32)
    r = (t // GRID_W).astype(F32)
    col = (t % GRID_W).astype(F32)
    ang = jnp.concatenate([r[:, None] * inv_freq[None, :], col[:, None] * inv_freq[None, :]], axis=-1)
    cos, sin = jnp.cos(ang), jnp.sin(ang)
    reps = LANES // rot_dim
    return (jnp.tile(jnp.concatenate([cos, cos], axis=-1), (1, reps)),
            jnp.tile(jnp.concatenate([-sin, sin], axis=-1), (1, reps)))


def _mla_weights(w_in, q_norm, kv_norm, w_uq, w_ukv, w_o):
    rank = A_Q_RANK + A_KV_RANK
    k_r = w_in[:, rank:]
    uq = w_uq.reshape(A_Q_RANK, A_HEADS, A_NOPE + A_ROPE)
    ukv = w_ukv.reshape(A_KV_RANK, A_HEADS, A_NOPE + A_V)
    return {
        "w_in": jnp.concatenate([w_in[:, :rank], k_r, k_r], axis=1).astype(BF16),
        "q_norm": q_norm, "kv_norm": kv_norm,
        "w_uq": jnp.concatenate([uq[:, :, :A_NOPE].reshape(A_Q_RANK, -1),
                                 uq[:, :, A_NOPE:].reshape(A_Q_RANK, -1)], axis=1).astype(BF16),
        "w_ukv": jnp.concatenate([ukv[:, :, :A_NOPE].reshape(A_KV_RANK, -1),
                                  ukv[:, :, A_NOPE:].reshape(A_KV_RANK, -1)], axis=1).astype(BF16),
        "w_o": w_o.astype(BF16),
    }


def kernel(x, c, ctx, c_ctx, norm_g, w_mod, b_mod, ffn1_w13, ffn1_w2, ffn2_w13, ffn2_w2, a_w_in, a_q_norm, a_kv_norm, a_w_uq, a_w_ukv, a_w_o, b_w_qkv, b_q_norm, b_k_norm, b_w_o, c_w_qkv, c_rpb, c_w_o, final_norm_g):
    b, t, d = x.shape
    depth = w_mod.shape[0]
    rows = t // GRID_W
    assert b < MOD_ROWS and t % (NB_ROWS * GRID_W) == 0 and rows >= NB_KEY_ROWS + NB_ROWS

    c_rows = jnp.zeros((MOD_ROWS, d), F32).at[:b].set(c).at[b].set(c_ctx)
    mod_all = _modulation(c_rows, w_mod, b_mod).reshape(depth, MOD_ROWS, N_MOD, d)
    lat_row = lambda bi: bi
    ctx_row = lambda bi: b

    rope_a = _rope_tables(rows, A_ROPE)
    rope_b = _rope_tables(rows, B_HEAD_DIM)

    xc = ctx
    for i in range(depth):
        ctx_out = i < depth - 1
        last = i == depth - 1
        mod = mod_all[i]
        kind, j = i % N_MIXERS, i // N_MIXERS
        w13_1, w2_1 = ffn1_w13[i].astype(BF16), ffn1_w2[i].astype(BF16)
        w13_2, w2_2 = ffn2_w13[i].astype(BF16), ffn2_w2[i].astype(BF16)

        x = _ffn(x, mod, lat_row, norm_g[i, 0], w13_1, w2_1, k0=0)
        xc = _ffn(xc, mod, ctx_row, norm_g[i, 0], w13_1, w2_1, k0=0)

        if kind == 0:
            w = _mla_weights(a_w_in[j], a_q_norm[j], a_kv_norm[j], a_w_uq[j], a_w_ukv[j], a_w_o[j])
            q, kn, kr, vt = _mla_proj(x, mod, lat_row, norm_g[i, 1], w, rope_a)
            qc, knc, krc, vtc = _mla_proj(xc, mod, ctx_row, norm_g[i, 1], w, None)
            blocks = (lambda h: h, lambda h: 0)
            o = _attention(q, [[knc, krc], [kn, kr]], [vtc, vt], heads=A_HEADS, kv_of=lambda h: h,
                           k_lane_blocks=blocks)
            if ctx_out:
                oc = _attention(qc, [[knc, krc]], [vtc], heads=A_HEADS, kv_of=lambda h: h,
                                k_lane_blocks=blocks)
            w_o = w["w_o"]
        elif kind == 1:
            w = {"w_qkv": b_w_qkv[j].astype(BF16), "q_norm": b_q_norm[j], "k_norm": b_k_norm[j]}
            q, k, vt = _gqa_proj(x, mod, lat_row, norm_g[i, 1], w, rope_b)
            qc, kc, vtc = _gqa_proj(xc, mod, ctx_row, norm_g[i, 1], w, None)
            group = B_HEADS // B_KV_HEADS
            blocks = (lambda h: h // group,)
            o = _attention(q, [[kc], [k]], [vtc, vt], heads=B_HEADS, kv_of=lambda h: h // group,
                           k_lane_blocks=blocks)
            if ctx_out:
                oc = _attention(qc, [[kc]], [vtc], heads=B_HEADS, kv_of=lambda h: h // group,
                                k_lane_blocks=blocks)
            w_o = b_w_o[j].astype(BF16)
        else:
            w = {"w_qkv": c_w_qkv[j].astype(BF16)}
            q, k, vt = _nb_proj(x, mod, lat_row, norm_g[i, 1], w)
            qc, kc, vtc = _nb_proj(xc, mod, ctx_row, norm_g[i, 1], w)
            o = _nb_attention(q, kc, vtc, k, vt, _nb_bias_table(c_rpb[j], rows))
            if ctx_out:
                oc = _nb_attention(qc, kc, vtc)
            w_o = c_w_o[j].astype(BF16)

        x = _ffn(x, mod, lat_row, norm_g[i, 2], w13_2, w2_2, k0=6, attn=o, w_o=w_o,
                 final_g=final_norm_g if last else None)
        if ctx_out:
            xc = _ffn(xc, mod, ctx_row, norm_g[i, 2], w13_2, w2_2, k0=6, attn=oc, w_o=w_o)
    return x
```

```python
import functools

import jax
import jax.numpy as jnp
from jax import lax
from jax.experimental import pallas as pl
from jax.experimental.pallas import tpu as pltpu

F32 = jnp.float32
BF16 = jnp.bfloat16

GRID_W = 64
N_MIXERS = 3
N_MOD = 9
EPS = 1e-6
ROPE_THETA = 10000.0

A_HEADS, A_NOPE, A_ROPE, A_V = 8, 128, 64, 128
A_Q_RANK, A_KV_RANK = 384, 256
B_HEADS, B_KV_HEADS, B_HEAD_DIM = 8, 2, 128
C_HEADS, C_HEAD_DIM, C_WIN_ROWS, C_WIN_COLS = 16, 64, 8, 16

LANES = 128
MOD_ROWS = 16
MASK_VALUE = -1e30
LOG2E = 1.4426950408889634
VMEM_LIMIT = 56 * 1024 * 1024

ROW_TILE = 256
Q_TILE = 512
NB_ROWS = 4
NB_KEY_ROWS = 12


def _params(*sem):
    return pltpu.CompilerParams(dimension_semantics=sem, vmem_limit_bytes=VMEM_LIMIT)


def _const_spec(shape):
    nd = len(shape)
    return pl.BlockSpec(shape, lambda *_: (0,) * nd, pipeline_mode=pl.Buffered(1))


def _dot(a, b):
    return jnp.dot(a, b, preferred_element_type=F32)


def _dot_nt(a, b):
    return lax.dot_general(a, b, (((1,), (1,)), ((), ())), preferred_element_type=F32)


def _rms(x):
    return x * lax.rsqrt(jnp.mean(x * x, axis=-1, keepdims=True) + EPS)


def _modulated_norm(x, g, scale, shift):
    return (_rms(x) * g) * (1.0 + scale) + shift


def _silu(x):
    return x / (1.0 + jnp.exp(-x))


def _mod_kernel(c_ref, w_ref, b_ref, o_ref):
    sc = _silu(c_ref[...]).astype(BF16)
    o_ref[0] = _dot(sc, w_ref[0].astype(BF16)) + b_ref[0]


def _modulation(c_rows, w_mod, b_mod):
    depth, d, n = w_mod.shape
    tn = d
    return pl.pallas_call(
        _mod_kernel,
        grid=(depth, n // tn),
        in_specs=[
            pl.BlockSpec((MOD_ROWS, d), lambda i, j: (0, 0)),
            pl.BlockSpec((1, d, tn), lambda i, j: (i, 0, j)),
            pl.BlockSpec((1, 1, tn), lambda i, j: (i, 0, j)),
        ],
        out_specs=pl.BlockSpec((1, MOD_ROWS, tn), lambda i, j: (i, 0, j)),
        out_shape=jax.ShapeDtypeStruct((depth, MOD_ROWS, n), F32),
        compiler_params=_params("parallel", "parallel"),
        name="modulation",
    )(c_rows, w_mod, b_mod.reshape(depth, 1, n))


def _ffn_kernel(*refs, k0, d_ff, has_oproj, final_norm):
    it = iter(refs)
    x_ref, mod_ref, g_ref, w13_ref, w2_ref = (next(it) for _ in range(5))
    if has_oproj:
        a_ref, wo_ref = next(it), next(it)
    if final_norm:
        fg_ref = next(it)
    out_ref = next(it)

    x = x_ref[0]
    mod = mod_ref[0]
    if has_oproj:
        x = x + mod[5:6] * _dot(a_ref[0], wo_ref[...])
    h = _modulated_norm(x, g_ref[...], mod[k0 + 1:k0 + 2], mod[k0:k0 + 1]).astype(BF16)
    hgu = _dot(h, w13_ref[...])
    act = (_silu(hgu[:, :d_ff]) * hgu[:, d_ff:]).astype(BF16)
    y = x + (0.5 * mod[k0 + 2:k0 + 3]) * _dot(act, w2_ref[...])
    if final_norm:
        y = _rms(y) * fg_ref[...]
    out_ref[0] = y


def _ffn(x, mod, mod_row, g, w13, w2, *, k0, attn=None, w_o=None, final_g=None):
    b, t, d = x.shape
    d_ff = w2.shape[0]
    tm = min(ROW_TILE, t)
    row = lambda bi, ti: (bi, ti, 0)
    in_specs = [
        pl.BlockSpec((1, tm, d), row),
        pl.BlockSpec((1, N_MOD, d), lambda bi, ti: (mod_row(bi), 0, 0)),
        _const_spec((1, d)),
        _const_spec(w13.shape),
        _const_spec(w2.shape),
    ]
    args = [x, mod, g.reshape(1, d), w13, w2]
    if attn is not None:
        in_specs += [pl.BlockSpec((1, tm, attn.shape[2]), row), _const_spec(w_o.shape)]
        args += [attn, w_o]
    if final_g is not None:
        in_specs.append(_const_spec((1, d)))
        args.append(final_g.reshape(1, d))
    return pl.pallas_call(
        functools.partial(_ffn_kernel, k0=k0, d_ff=d_ff, has_oproj=attn is not None,
                          final_norm=final_g is not None),
        grid=(b, t // tm),
        in_specs=in_specs,
        out_specs=pl.BlockSpec((1, tm, d), row),
        out_shape=jax.ShapeDtypeStruct((b, t, d), F32),
        compiler_params=_params("parallel", "parallel"),
        name="half_ffn",
    )(*args)


def _rope_pairs(x, cos, sin_signed, half):
    if 2 * half == LANES:
        rot = pltpu.roll(x, half, 1)
    else:
        lane = lax.broadcasted_iota(jnp.int32, x.shape, 1)
        first = (lane % (2 * half)) < half
        rot = jnp.where(first, pltpu.roll(x, LANES - half, 1), pltpu.roll(x, half, 1))
    return x * cos + rot * sin_signed


def _mla_proj_kernel(*refs, use_rope, sm_scale):
    it = iter(refs)
    x_ref, mod_ref, g_ref, w_in_ref, qg_ref, kvg_ref, w_uq_ref, w_ukv_ref = (next(it) for _ in range(8))
    if use_rope:
        cos_ref, sin_ref = next(it), next(it)
    q_ref, kn_ref, kr_ref, vt_ref = (next(it) for _ in range(4))

    mod = mod_ref[0]
    h = _modulated_norm(x_ref[0], g_ref[...], mod[4:5], mod[3:4]).astype(BF16)
    proj = _dot(h, w_in_ref[...])
    c_q = (_rms(proj[:, :A_Q_RANK]) * qg_ref[...]).astype(BF16)
    c_kv = (_rms(proj[:, A_Q_RANK:A_Q_RANK + A_KV_RANK]) * kvg_ref[...]).astype(BF16)
    k_r = proj[:, A_Q_RANK + A_KV_RANK:]
    q_all = _dot(c_q, w_uq_ref[...])
    kv = _dot(c_kv, w_ukv_ref[...])
    n_nope = A_HEADS * A_NOPE
    if use_rope:
        cos, sin = cos_ref[...], sin_ref[...]
        k_r = _rope_pairs(k_r, cos, sin, A_ROPE // 2)
    lane = lax.broadcasted_iota(jnp.int32, (1, LANES), 1)
    for j in range(A_HEADS // 2):
        qr = q_all[:, n_nope + j * LANES:n_nope + (j + 1) * LANES]
        if use_rope:
            qr = _rope_pairs(qr, cos, sin, A_ROPE // 2)
        qr = qr * sm_scale
        for e in range(2):
            hd = 2 * j + e
            keep = (lane < A_ROPE) if e == 0 else (lane >= A_ROPE)
            q_ref[0, :, 2 * hd * LANES:(2 * hd + 1) * LANES] = (
                q_all[:, hd * A_NOPE:(hd + 1) * A_NOPE] * sm_scale).astype(BF16)
            q_ref[0, :, (2 * hd + 1) * LANES:(2 * hd + 2) * LANES] = jnp.where(keep, qr, 0.0).astype(BF16)
    kn_ref[0] = kv[:, :n_nope].astype(BF16)
    kr_ref[0] = k_r.astype(BF16)
    vt_ref[0] = kv[:, n_nope:].T.astype(BF16)


def _mla_proj(x, mod, mod_row, g, w, rope):
    b, t, d = x.shape
    tm = min(ROW_TILE, t)
    row = lambda bi, ti: (bi, ti, 0)
    in_specs = [
        pl.BlockSpec((1, tm, d), row),
        pl.BlockSpec((1, N_MOD, d), lambda bi, ti: (mod_row(bi), 0, 0)),
        _const_spec((1, d)),
        _const_spec(w["w_in"].shape), _const_spec((1, A_Q_RANK)), _const_spec((1, A_KV_RANK)),
        _const_spec(w["w_uq"].shape), _const_spec(w["w_ukv"].shape),
    ]
    args = [x, mod, g.reshape(1, d), w["w_in"], w["q_norm"].reshape(1, -1), w["kv_norm"].reshape(1, -1),
            w["w_uq"], w["w_ukv"]]
    if rope is not None:
        in_specs += [pl.BlockSpec((tm, LANES), lambda bi, ti: (ti, 0))] * 2
        args += list(rope)
    hv = A_HEADS * A_V
    return pl.pallas_call(
        functools.partial(_mla_proj_kernel, use_rope=rope is not None,
                          sm_scale=float((A_NOPE + A_ROPE) ** -0.5 * LOG2E)),
        grid=(b, t // tm),
        in_specs=in_specs,
        out_specs=[
            pl.BlockSpec((1, tm, 2 * LANES * A_HEADS), row),
            pl.BlockSpec((1, tm, A_HEADS * A_NOPE), row),
            pl.BlockSpec((1, tm, LANES), row),
            pl.BlockSpec((1, hv, tm), lambda bi, ti: (bi, 0, ti)),
        ],
        out_shape=[
            jax.ShapeDtypeStruct((b, t, 2 * LANES * A_HEADS), BF16),
            jax.ShapeDtypeStruct((b, t, A_HEADS * A_NOPE), BF16),
            jax.ShapeDtypeStruct((b, t, LANES), BF16),
            jax.ShapeDtypeStruct((b, hv, t), BF16),
        ],
        compiler_params=_params("parallel", "parallel"),
        name="mla_proj",
    )(*args)


def _gqa_proj_kernel(*refs, use_rope, sm_scale):
    it = iter(refs)
    x_ref, mod_ref, g_ref, w_ref, qg_ref, kg_ref = (next(it) for _ in range(6))
    if use_rope:
        cos_ref, sin_ref = next(it), next(it)
    q_ref, k_ref, vt_ref = (next(it) for _ in range(3))

    mod = mod_ref[0]
    h = _modulated_norm(x_ref[0], g_ref[...], mod[4:5], mod[3:4]).astype(BF16)
    proj = _dot(h, w_ref[...])
    qw = B_HEADS * B_HEAD_DIM
    kw = B_KV_HEADS * B_HEAD_DIM
    if use_rope:
        cos, sin = cos_ref[...], sin_ref[...]

    def head(col, gain, scale):
        y = _rms(proj[:, col:col + B_HEAD_DIM]) * gain
        if use_rope:
            y = _rope_pairs(y, cos, sin, B_HEAD_DIM // 2)
        return (y * scale).astype(BF16) if scale != 1.0 else y.astype(BF16)

    for hd in range(B_HEADS):
        q_ref[0, :, hd * B_HEAD_DIM:(hd + 1) * B_HEAD_DIM] = head(hd * B_HEAD_DIM, qg_ref[...], sm_scale)
    for hd in range(B_KV_HEADS):
        k_ref[0, :, hd * B_HEAD_DIM:(hd + 1) * B_HEAD_DIM] = head(qw + hd * B_HEAD_DIM, kg_ref[...], 1.0)
    vt_ref[0] = proj[:, qw + kw:].T.astype(BF16)


def _gqa_proj(x, mod, mod_row, g, w, rope):
    b, t, d = x.shape
    tm = min(ROW_TILE, t)
    row = lambda bi, ti: (bi, ti, 0)
    qw, kw = B_HEADS * B_HEAD_DIM, B_KV_HEADS * B_HEAD_DIM
    in_specs = [
        pl.BlockSpec((1, tm, d), row),
        pl.BlockSpec((1, N_MOD, d), lambda bi, ti: (mod_row(bi), 0, 0)),
        _const_spec((1, d)),
        _const_spec(w["w_qkv"].shape), _const_spec((1, B_HEAD_DIM)), _const_spec((1, B_HEAD_DIM)),
    ]
    args = [x, mod, g.reshape(1, d), w["w_qkv"], w["q_norm"].reshape(1, -1), w["k_norm"].reshape(1, -1)]
    if rope is not None:
        in_specs += [pl.BlockSpec((tm, LANES), lambda bi, ti: (ti, 0))] * 2
        args += list(rope)
    return pl.pallas_call(
        functools.partial(_gqa_proj_kernel, use_rope=rope is not None, sm_scale=float(B_HEAD_DIM ** -0.5 * LOG2E)),
        grid=(b, t // tm),
        in_specs=in_specs,
        out_specs=[
            pl.BlockSpec((1, tm, qw), row),
            pl.BlockSpec((1, tm, kw), row),
            pl.BlockSpec((1, kw, tm), lambda bi, ti: (bi, 0, ti)),
        ],
        out_shape=[
            jax.ShapeDtypeStruct((b, t, qw), BF16),
            jax.ShapeDtypeStruct((b, t, kw), BF16),
            jax.ShapeDtypeStruct((b, kw, t), BF16),
        ],
        compiler_params=_params("parallel", "parallel"),
        name="gqa_proj",
    )(*args)


def _nb_proj_kernel(x_ref, mod_ref, g_ref, w_ref, q_ref, k_ref, vt_ref, *, sm_scale):
    mod = mod_ref[0]
    h = _modulated_norm(x_ref[0], g_ref[...], mod[4:5], mod[3:4]).astype(BF16)
    proj = _dot(h, w_ref[...])
    hd = C_HEADS * C_HEAD_DIM
    q_ref[0] = (proj[:, :hd] * sm_scale).astype(BF16)
    k_ref[0] = proj[:, hd:2 * hd].astype(BF16)
    vt_ref[0] = proj[:, 2 * hd:].T.astype(BF16)


def _nb_proj(x, mod, mod_row, g, w):
    b, t, d = x.shape
    tm = min(ROW_TILE, t)
    row = lambda bi, ti: (bi, ti, 0)
    hd = C_HEADS * C_HEAD_DIM
    return pl.pallas_call(
        functools.partial(_nb_proj_kernel, sm_scale=float(C_HEAD_DIM ** -0.5 * LOG2E)),
        grid=(b, t // tm),
        in_specs=[
            pl.BlockSpec((1, tm, d), row),
            pl.BlockSpec((1, N_MOD, d), lambda bi, ti: (mod_row(bi), 0, 0)),
            _const_spec((1, d)),
            _const_spec(w["w_qkv"].shape),
        ],
        out_specs=[
            pl.BlockSpec((1, tm, hd), row),
            pl.BlockSpec((1, tm, hd), row),
            pl.BlockSpec((1, hd, tm), lambda bi, ti: (bi, 0, ti)),
        ],
        out_shape=[
            jax.ShapeDtypeStruct((b, t, hd), BF16),
            jax.ShapeDtypeStruct((b, t, hd), BF16),
            jax.ShapeDtypeStruct((b, hd, t), BF16),
        ],
        compiler_params=_params("parallel", "parallel"),
        name="nb_proj",
    )(x, mod, g.reshape(1, d), w["w_qkv"])


def _attn_kernel(*refs, key_rows, k_pieces, q_tile, n_tiles):
    n_groups = len(key_rows)
    it = iter(refs)
    q_ref = next(it)
    k_refs = [[next(it) for _ in range(k_pieces)] for _ in range(n_groups)]
    v_refs = [next(it) for _ in range(n_groups)]
    o_ref = next(it)
    k_scr, v_scr, s_a, s_b, m_a, m_b = (next(it) for _ in range(6))

    r0 = 0
    for gi, rows in enumerate(key_rows):
        for pi in range(k_pieces):
            k_scr[r0:r0 + rows, pi * LANES:(pi + 1) * LANES] = k_refs[gi][pi][0]
        v_scr[:, r0:r0 + rows] = v_refs[gi][0]
        r0 += rows

    def tile_rows(t):
        return pl.ds(pl.multiple_of(t * q_tile, q_tile), q_tile)

    def scores(t, s_buf, m_buf):
        s_t = _dot_nt(k_scr[...], q_ref[0, tile_rows(t), :])
        s_buf[...] = s_t
        m_buf[...] = jnp.max(s_t, axis=0, keepdims=True)

    def softmax_pv(t, s_buf, m_buf):
        p = jnp.exp2(s_buf[...] - m_buf[...])
        l = jnp.sum(p, axis=0, keepdims=True)
        o_t = _dot(v_scr[...], p.astype(BF16)) / l
        o_ref[0, tile_rows(t), :] = o_t.T.astype(BF16)

    scores(0, s_a, m_a)
    if n_tiles == 1:
        softmax_pv(0, s_a, m_a)
        return

    def pair(i, carry):
        t = 2 * i
        scores(t + 1, s_b, m_b)
        softmax_pv(t, s_a, m_a)
        scores(jnp.minimum(t + 2, n_tiles - 1), s_a, m_a)
        softmax_pv(t + 1, s_b, m_b)
        return carry

    lax.fori_loop(0, n_tiles // 2, pair, 0)


def _attention(q, k_groups, v_groups, *, heads, kv_of, k_lane_blocks):
    b, tq, qw = q.shape
    dq = qw // heads
    q_tile = min(Q_TILE, tq)
    n_tiles = tq // q_tile
    assert n_tiles == 1 or n_tiles % 2 == 0
    key_rows = tuple(g[0].shape[1] for g in k_groups)
    k_pieces = len(k_groups[0])
    s_total = sum(key_rows)
    in_specs = [pl.BlockSpec((1, tq, dq), lambda bi, h: (bi, 0, h))]
    args = [q]
    for grp in k_groups:
        for pi, arr in enumerate(grp):
            in_specs.append(pl.BlockSpec((1, arr.shape[1], LANES),
                                         lambda bi, h, f=k_lane_blocks[pi]: (bi, 0, f(h))))
            args.append(arr)
    for arr in v_groups:
        in_specs.append(pl.BlockSpec((1, LANES, arr.shape[2]), lambda bi, h: (bi, kv_of(h), 0)))
        args.append(arr)
    return pl.pallas_call(
        functools.partial(_attn_kernel, key_rows=key_rows, k_pieces=k_pieces, q_tile=q_tile, n_tiles=n_tiles),
        grid=(b, heads),
        in_specs=in_specs,
        out_specs=pl.BlockSpec((1, tq, LANES), lambda bi, h: (bi, 0, h)),
        out_shape=jax.ShapeDtypeStruct((b, tq, heads * LANES), BF16),
        scratch_shapes=[
            pltpu.VMEM((s_total, k_pieces * LANES), BF16), pltpu.VMEM((LANES, s_total), BF16),
            pltpu.VMEM((s_total, q_tile), F32), pltpu.VMEM((s_total, q_tile), F32),
            pltpu.VMEM((1, q_tile), F32), pltpu.VMEM((1, q_tile), F32),
        ],
        compiler_params=_params("parallel", "parallel"),
        name="attention",
    )(*args)


def _nb_attn_kernel(*refs, has_window, rows, tile, n_groups):
    it = iter(refs)
    q_ref, kc_ref, vc_ref = next(it), next(it), next(it)
    if has_window:
        kl_ref, vl_ref, bias_ref = next(it), next(it), next(it)
    o_ref = next(it)
    s_a, s_b, m_a, m_b = (next(it) for _ in range(4))
    c = kc_ref.shape[1]
    win = NB_KEY_ROWS * GRID_W
    lane = lax.broadcasted_iota(jnp.int32, (1, LANES), 1)

    def tile_rows(g):
        return pl.ds(pl.multiple_of(g * tile, tile), tile)

    def window(g):
        first_row = jnp.clip(NB_ROWS * g - C_WIN_ROWS // 2, 0, rows - NB_KEY_ROWS)
        return pl.ds(pl.multiple_of(first_row * GRID_W, 2 * LANES), win)

    def scores(g, s_buf, m_buf):
        q = q_ref[0, tile_rows(g), :]
        zero = jnp.zeros_like(q)
        q2 = jnp.concatenate([jnp.where(lane < C_HEAD_DIM, q, zero), jnp.where(lane >= C_HEAD_DIM, q, zero)],
                             axis=0)
        s_c = _dot_nt(kc_ref[0], q2)
        s_buf[0:c, :] = s_c
        m = jnp.max(s_c, axis=0, keepdims=True)
        if has_window:
            variant = jnp.where(g == 0, 0, jnp.where(g == n_groups - 1, 2, 1))
            s_w = _dot_nt(kl_ref[0, window(g), :], q2)
            for e in range(2):
                s_e = s_w[:, e * tile:(e + 1) * tile] + bias_ref[variant, e]
                s_buf[c:, e * tile:(e + 1) * tile] = s_e
                m_e = jnp.maximum(m[:, e * tile:(e + 1) * tile], jnp.max(s_e, axis=0, keepdims=True))
                m_buf[:, e * tile:(e + 1) * tile] = m_e
        else:
            m_buf[...] = m

    def softmax_pv(g, s_buf, m_buf):
        p = jnp.exp2(s_buf[...] - m_buf[...])
        l = jnp.sum(p, axis=0, keepdims=True)
        pb = p.astype(BF16)
        o2 = _dot(vc_ref[0], pb[0:c])
        if has_window:
            o2 = o2 + _dot(vl_ref[0, :, window(g)], pb[c:])
        o_t = jnp.concatenate([o2[e * C_HEAD_DIM:(e + 1) * C_HEAD_DIM, e * tile:(e + 1) * tile]
                               / l[:, e * tile:(e + 1) * tile] for e in range(2)], axis=0)
        o_ref[0, tile_rows(g), :] = o_t.T.astype(BF16)

    scores(0, s_a, m_a)
    if n_groups == 1:
        softmax_pv(0, s_a, m_a)
        return

    def pair(i, carry):
        g = 2 * i
        scores(g + 1, s_b, m_b)
        softmax_pv(g, s_a, m_a)
        scores(jnp.minimum(g + 2, n_groups - 1), s_a, m_a)
        softmax_pv(g + 1, s_b, m_b)
        return carry

    lax.fori_loop(0, n_groups // 2, pair, 0)


def _nb_attention(q, k_ctx, vt_ctx, k_lat=None, vt_lat=None, bias=None):
    b, tq, hw = q.shape
    pairs = hw // LANES
    has_window = k_lat is not None
    c = k_ctx.shape[1]
    if has_window:
        t = k_lat.shape[1]
        rows = t // GRID_W
        tile = NB_ROWS * GRID_W
        keys = c + NB_KEY_ROWS * GRID_W
    else:
        rows, tile, keys = 0, tq, c
    n_groups = tq // tile
    assert n_groups == 1 or n_groups % 2 == 0
    in_specs = [
        pl.BlockSpec((1, tq, LANES), lambda bi, p: (bi, 0, p)),
        pl.BlockSpec((1, c, LANES), lambda bi, p: (bi, 0, p)),
        pl.BlockSpec((1, LANES, c), lambda bi, p: (bi, p, 0)),
    ]
    args = [q, k_ctx, vt_ctx]
    if has_window:
        in_specs += [
            pl.BlockSpec((1, t, LANES), lambda bi, p: (bi, 0, p)),
            pl.BlockSpec((1, LANES, t), lambda bi, p: (bi, p, 0)),
            pl.BlockSpec((3, 2, NB_KEY_ROWS * GRID_W, tile), lambda bi, p: (0, p, 0, 0)),
        ]
        args += [k_lat, vt_lat, bias]
    return pl.pallas_call(
        functools.partial(_nb_attn_kernel, has_window=has_window, rows=rows, tile=tile, n_groups=n_groups),
        grid=(b, pairs),
        in_specs=in_specs,
        out_specs=pl.BlockSpec((1, tq, LANES), lambda bi, p: (bi, 0, p)),
        out_shape=jax.ShapeDtypeStruct((b, tq, hw), BF16),
        scratch_shapes=[pltpu.VMEM((keys, 2 * tile), F32), pltpu.VMEM((keys, 2 * tile), F32),
                        pltpu.VMEM((1, 2 * tile), F32), pltpu.VMEM((1, 2 * tile), F32)],
        compiler_params=_params("parallel", "parallel"),
        name="nb_attention",
    )(*args)


def _nb_bias_table(rpb, rows):
    heads = rpb.shape[0]
    tile_q = NB_ROWS * GRID_W
    return pl.pallas_call(
        functools.partial(_nb_bias_kernel, rows=rows),
        grid=(heads,),
        in_specs=[pl.BlockSpec(memory_space=pltpu.SMEM)],
        out_specs=pl.BlockSpec((3, 1, NB_KEY_ROWS * GRID_W, tile_q), lambda h: (0, h, 0, 0)),
        out_shape=jax.ShapeDtypeStruct((3, heads, NB_KEY_ROWS * GRID_W, tile_q), F32),
        compiler_params=_params("parallel"),
        name="nb_bias",
    )(rpb.reshape(-1))


def _nb_bias_kernel(rpb_ref, o_ref, *, rows):
    n_a, n_b = 2 * C_WIN_ROWS - 1, 2 * C_WIN_COLS - 1
    tile_q = NB_ROWS * GRID_W
    shape = (GRID_W, tile_q)
    kc = lax.broadcasted_iota(jnp.int32, shape, 0)
    lane = lax.broadcasted_iota(jnp.int32, shape, 1)
    qc = lane % GRID_W
    qi = lane // GRID_W
    c0 = jnp.clip(qc - C_WIN_COLS // 2, 0, GRID_W - C_WIN_COLS)
    col_ok = (kc >= c0) & (kc < c0 + C_WIN_COLS)
    dcol = kc - qc + (C_WIN_COLS - 1)
    base = pl.program_id(0) * (n_a * n_b)
    masked = jnp.full(shape, MASK_VALUE, F32)
    planes = []
    for a in range(n_a):
        acc = masked
        for bb in range(n_b):
            acc = jnp.where(dcol == bb, rpb_ref[base + a * n_b + bb], acc)
        planes.append(jnp.where(col_ok, acc * LOG2E, MASK_VALUE))
    groups = rows // NB_ROWS
    for v, g in enumerate((0, 1, groups - 1)):
        first_key_row = min(max(NB_ROWS * g - C_WIN_ROWS // 2, 0), rows - NB_KEY_ROWS)
        for j in range(NB_KEY_ROWS):
            kr = first_key_row + j
            blk = masked
            for i in range(NB_ROWS):
                qr = NB_ROWS * g + i
                r0 = min(max(qr - C_WIN_ROWS // 2, 0), rows - C_WIN_ROWS)
                if r0 <= kr < r0 + C_WIN_ROWS:
                    blk = jnp.where(qi == i, planes[kr - qr + C_WIN_ROWS - 1], blk)
            o_ref[v, 0, j * GRID_W:(j + 1) * GRID_W, :] = blk


def _rope_tables(rows, rot_dim):
    n = rot_dim // 4
    inv_freq = ROPE_THETA ** (-jnp.arange(n, dtype=F32) / n)
    t = jnp.arange(rows * GRID_W, dtype=jnp.int32)
    r = (t // GRID_W).astype(F32)
    col = (t % GRID_W).astype(F32)
    ang = jnp.concatenate([r[:, None] * inv_freq[None, :], col[:, None] * inv_freq[None, :]], axis=-1)
    cos, sin = jnp.cos(ang), jnp.sin(ang)
    reps = LANES // rot_dim
    return (jnp.tile(jnp.concatenate([cos, cos], axis=-1), (1, reps)),
            jnp.tile(jnp.concatenate([-sin, sin], axis=-1), (1, reps)))


def _mla_weights(w_in, q_norm, kv_norm, w_uq, w_ukv, w_o):
    rank = A_Q_RANK + A_KV_RANK
    k_r = w_in[:, rank:]
    uq = w_uq.reshape(A_Q_RANK, A_HEADS, A_NOPE + A_ROPE)
    ukv = w_ukv.reshape(A_KV_RANK, A_HEADS, A_NOPE + A_V)
    return {
        "w_in": jnp.concatenate([w_in[:, :rank], k_r, k_r], axis=1).astype(BF16),
        "q_norm": q_norm, "kv_norm": kv_norm,
        "w_uq": jnp.concatenate([uq[:, :, :A_NOPE].reshape(A_Q_RANK, -1),
                                 uq[:, :, A_NOPE:].reshape(A_Q_RANK, -1)], axis=1).astype(BF16),
        "w_ukv": jnp.concatenate([ukv[:, :, :A_NOPE].reshape(A_KV_RANK, -1),
                                  ukv[:, :, A_NOPE:].reshape(A_KV_RANK, -1)], axis=1).astype(BF16),
        "w_o": w_o.astype(BF16),
    }


def kernel(x, c, ctx, c_ctx, norm_g, w_mod, b_mod, ffn1_w13, ffn1_w2, ffn2_w13, ffn2_w2, a_w_in, a_q_norm, a_kv_norm, a_w_uq, a_w_ukv, a_w_o, b_w_qkv, b_q_norm, b_k_norm, b_w_o, c_w_qkv, c_rpb, c_w_o, final_norm_g):
    b, t, d = x.shape
    depth = w_mod.shape[0]
    rows = t // GRID_W
    assert b < MOD_ROWS and t % (NB_ROWS * GRID_W) == 0 and rows >= NB_KEY_ROWS + NB_ROWS

    c_rows = jnp.zeros((MOD_ROWS, d), F32).at[:b].set(c).at[b].set(c_ctx)
    mod_all = _modulation(c_rows, w_mod, b_mod).reshape(depth, MOD_ROWS, N_MOD, d)
    lat_row = lambda bi: bi
    ctx_row = lambda bi: b

    rope_a = _rope_tables(rows, A_ROPE)
    rope_b = _rope_tables(rows, B_HEAD_DIM)

    xc = ctx
    for i in range(depth):
        ctx_out = i < depth - 1
        last = i == depth - 1
        mod = mod_all[i]
        kind, j = i % N_MIXERS, i // N_MIXERS
        w13_1, w2_1 = ffn1_w13[i].astype(BF16), ffn1_w2[i].astype(BF16)
        w13_2, w2_2 = ffn2_w13[i].astype(BF16), ffn2_w2[i].astype(BF16)

        x = _ffn(x, mod, lat_row, norm_g[i, 0], w13_1, w2_1, k0=0)
        xc = _ffn(xc, mod, ctx_row, norm_g[i, 0], w13_1, w2_1, k0=0)

        if kind == 0:
            w = _mla_weights(a_w_in[j], a_q_norm[j], a_kv_norm[j], a_w_uq[j], a_w_ukv[j], a_w_o[j])
            q, kn, kr, vt = _mla_proj(x, mod, lat_row, norm_g[i, 1], w, rope_a)
            qc, knc, krc, vtc = _mla_proj(xc, mod, ctx_row, norm_g[i, 1], w, None)
            blocks = (lambda h: h, lambda h: 0)
            o = _attention(q, [[knc, krc], [kn, kr]], [vtc, vt], heads=A_HEADS, kv_of=lambda h: h,
                           k_lane_blocks=blocks)
            if ctx_out:
                oc = _attention(qc, [[knc, krc]], [vtc], heads=A_HEADS, kv_of=lambda h: h,
                                k_lane_blocks=blocks)
            w_o = w["w_o"]
        elif kind == 1:
            w = {"w_qkv": b_w_qkv[j].astype(BF16), "q_norm": b_q_norm[j], "k_norm": b_k_norm[j]}
            q, k, vt = _gqa_proj(x, mod, lat_row, norm_g[i, 1], w, rope_b)
            qc, kc, vtc = _gqa_proj(xc, mod, ctx_row, norm_g[i, 1], w, None)
            group = B_HEADS // B_KV_HEADS
            blocks = (lambda h: h // group,)
            o = _attention(q, [[kc], [k]], [vtc, vt], heads=B_HEADS, kv_of=lambda h: h // group,
                           k_lane_blocks=blocks)
            if ctx_out:
                oc = _attention(qc, [[kc]], [vtc], heads=B_HEADS, kv_of=lambda h: h // group,
                                k_lane_blocks=blocks)
            w_o = b_w_o[j].astype(BF16)
        else:
            w = {"w_qkv": c_w_qkv[j].astype(BF16)}
            q, k, vt = _nb_proj(x, mod, lat_row, norm_g[i, 1], w)
            qc, kc, vtc = _nb_proj(xc, mod, ctx_row, norm_g[i, 1], w)
            o = _nb_attention(q, kc, vtc, k, vt, _nb_bias_table(c_rpb[j], rows))
            if ctx_out:
                oc = _nb_attention(qc, kc, vtc)
            w_o = c_w_o[j].astype(BF16)

        x = _ffn(x, mod, lat_row, norm_g[i, 2], w13_2, w2_2, k0=6, attn=o, w_o=w_o,
                 final_g=final_norm_g if last else None)
        if ctx_out:
            xc = _ffn(xc, mod, ctx_row, norm_g[i, 2], w13_2, w2_2, k0=6, attn=oc, w_o=w_o)
    return x
```

```python
import functools

import jax
import jax.numpy as jnp
from jax import lax
from jax.experimental import pallas as pl
from jax.experimental.pallas import tpu as pltpu

F32 = jnp.float32
BF16 = jnp.bfloat16

GRID_W = 64
N_MIXERS = 3
N_MOD = 9
EPS = 1e-6
ROPE_THETA = 10000.0

A_HEADS, A_NOPE, A_ROPE, A_V = 8, 128, 64, 128
A_Q_RANK, A_KV_RANK = 384, 256
B_HEADS, B_KV_HEADS, B_HEAD_DIM = 8, 2, 128
C_HEADS, C_HEAD_DIM, C_WIN_ROWS, C_WIN_COLS = 16, 64, 8, 16

LANES = 128
MOD_ROWS = 16
MASK_VALUE = -1e30
LOG2E = 1.4426950408889634
VMEM_LIMIT = 56 * 1024 * 1024

ROW_TILE = 512
GQA_ROW_TILE = 256
A_HEADS_PER_STEP = 2
Q_TILE = 512
NB_ROWS = 4
NB_KEY_ROWS = 12


def _params(*sem):
    return pltpu.CompilerParams(dimension_semantics=sem, vmem_limit_bytes=VMEM_LIMIT)


def _const_spec(shape):
    nd = len(shape)
    return pl.BlockSpec(shape, lambda *_: (0,) * nd, pipeline_mode=pl.Buffered(1))


def _dot(a, b):
    return jnp.dot(a, b, preferred_element_type=F32)


def _dot_nt(a, b):
    return lax.dot_general(a, b, (((1,), (1,)), ((), ())), preferred_element_type=F32)


def _rms(x):
    return x * lax.rsqrt(jnp.mean(x * x, axis=-1, keepdims=True) + EPS)


def _modulated_norm(x, g, scale, shift):
    return (_rms(x) * g) * (1.0 + scale) + shift


def _silu(x):
    return x / (1.0 + jnp.exp(-x))


def _mod_kernel(c_ref, w_ref, b_ref, o_ref):
    sc = _silu(c_ref[...]).astype(BF16)
    o_ref[0] = _dot(sc, w_ref[0].astype(BF16)) + b_ref[0]


def _modulation(c_rows, w_mod, b_mod):
    depth, d, n = w_mod.shape
    tn = d
    return pl.pallas_call(
        _mod_kernel,
        grid=(depth, n // tn),
        in_specs=[
            pl.BlockSpec((MOD_ROWS, d), lambda i, j: (0, 0)),
            pl.BlockSpec((1, d, tn), lambda i, j: (i, 0, j)),
            pl.BlockSpec((1, 1, tn), lambda i, j: (i, 0, j)),
        ],
        out_specs=pl.BlockSpec((1, MOD_ROWS, tn), lambda i, j: (i, 0, j)),
        out_shape=jax.ShapeDtypeStruct((depth, MOD_ROWS, n), F32),
        compiler_params=_params("parallel", "parallel"),
        name="modulation",
    )(c_rows, w_mod, b_mod.reshape(depth, 1, n))


def _ffn_kernel(*refs, k0, d_ff, has_oproj, final_norm):
    it = iter(refs)
    x_ref, mod_ref, g_ref, w13_ref, w2_ref = (next(it) for _ in range(5))
    if has_oproj:
        a_ref, wo_ref = next(it), next(it)
    if final_norm:
        fg_ref = next(it)
    out_ref = next(it)

    x = x_ref[0]
    mod = mod_ref[0]
    if has_oproj:
        x = x + mod[5:6] * _dot(a_ref[0], wo_ref[...])
    h = _modulated_norm(x, g_ref[...], mod[k0 + 1:k0 + 2], mod[k0:k0 + 1]).astype(BF16)
    hgu = _dot(h, w13_ref[...])
    act = (_silu(hgu[:, :d_ff]) * hgu[:, d_ff:]).astype(BF16)
    y = x + (0.5 * mod[k0 + 2:k0 + 3]) * _dot(act, w2_ref[...])
    if final_norm:
        y = _rms(y) * fg_ref[...]
    out_ref[0] = y


def _ffn(x, mod, mod_row, g, w13, w2, *, k0, attn=None, w_o=None, final_g=None):
    b, t, d = x.shape
    d_ff = w2.shape[0]
    tm = min(ROW_TILE, t)
    row = lambda bi, ti: (bi, ti, 0)
    in_specs = [
        pl.BlockSpec((1, tm, d), row),
        pl.BlockSpec((1, N_MOD, d), lambda bi, ti: (mod_row(bi), 0, 0)),
        _const_spec((1, d)),
        _const_spec(w13.shape),
        _const_spec(w2.shape),
    ]
    args = [x, mod, g.reshape(1, d), w13, w2]
    if attn is not None:
        in_specs += [pl.BlockSpec((1, tm, attn.shape[2]), row), _const_spec(w_o.shape)]
        args += [attn, w_o]
    if final_g is not None:
        in_specs.append(_const_spec((1, d)))
        args.append(final_g.reshape(1, d))
    return pl.pallas_call(
        functools.partial(_ffn_kernel, k0=k0, d_ff=d_ff, has_oproj=attn is not None,
                          final_norm=final_g is not None),
        grid=(b, t // tm),
        in_specs=in_specs,
        out_specs=pl.BlockSpec((1, tm, d), row),
        out_shape=jax.ShapeDtypeStruct((b, t, d), F32),
        compiler_params=_params("parallel", "parallel"),
        name="half_ffn",
    )(*args)


def _rope_pairs(x, cos, sin_signed, half):
    if 2 * half == LANES:
        rot = pltpu.roll(x, half, 1)
    else:
        lane = lax.broadcasted_iota(jnp.int32, x.shape, 1)
        first = (lane % (2 * half)) < half
        rot = jnp.where(first, pltpu.roll(x, LANES - half, 1), pltpu.roll(x, half, 1))
    return x * cos + rot * sin_signed


def _mla_proj_kernel(*refs, use_rope, sm_scale):
    it = iter(refs)
    x_ref, mod_ref, g_ref, w_in_ref, qg_ref, kvg_ref, w_uq_ref, w_ukv_ref = (next(it) for _ in range(8))
    if use_rope:
        cos_ref, sin_ref = next(it), next(it)
    q_ref, kn_ref, kr_ref, vt_ref = (next(it) for _ in range(4))

    mod = mod_ref[0]
    h = _modulated_norm(x_ref[0], g_ref[...], mod[4:5], mod[3:4]).astype(BF16)
    proj = _dot(h, w_in_ref[...])
    c_q = (_rms(proj[:, :A_Q_RANK]) * qg_ref[...]).astype(BF16)
    c_kv = (_rms(proj[:, A_Q_RANK:A_Q_RANK + A_KV_RANK]) * kvg_ref[...]).astype(BF16)
    k_r = proj[:, A_Q_RANK + A_KV_RANK:]
    q_all = _dot(c_q, w_uq_ref[...])
    kv = _dot(c_kv, w_ukv_ref[...])
    n_nope = A_HEADS * A_NOPE
    if use_rope:
        cos, sin = cos_ref[...], sin_ref[...]
        k_r = _rope_pairs(k_r, cos, sin, A_ROPE // 2)
    lane = lax.broadcasted_iota(jnp.int32, (1, LANES), 1)
    for j in range(A_HEADS // 2):
        qr = q_all[:, n_nope + j * LANES:n_nope + (j + 1) * LANES]
        if use_rope:
            qr = _rope_pairs(qr, cos, sin, A_ROPE // 2)
        qr = qr * sm_scale
        for e in range(2):
            hd = 2 * j + e
            keep = (lane < A_ROPE) if e == 0 else (lane >= A_ROPE)
            q_ref[0, :, 2 * hd * LANES:(2 * hd + 1) * LANES] = (
                q_all[:, hd * A_NOPE:(hd + 1) * A_NOPE] * sm_scale).astype(BF16)
            q_ref[0, :, (2 * hd + 1) * LANES:(2 * hd + 2) * LANES] = jnp.where(keep, qr, 0.0).astype(BF16)
    kn_ref[0] = kv[:, :n_nope].astype(BF16)
    kr_ref[0] = k_r.astype(BF16)
    vt_ref[0] = kv[:, n_nope:].T.astype(BF16)


def _mla_proj(x, mod, mod_row, g, w, rope):
    b, t, d = x.shape
    tm = min(ROW_TILE, t)
    row = lambda bi, ti: (bi, ti, 0)
    in_specs = [
        pl.BlockSpec((1, tm, d), row),
        pl.BlockSpec((1, N_MOD, d), lambda bi, ti: (mod_row(bi), 0, 0)),
        _const_spec((1, d)),
        _const_spec(w["w_in"].shape), _const_spec((1, A_Q_RANK)), _const_spec((1, A_KV_RANK)),
        _const_spec(w["w_uq"].shape), _const_spec(w["w_ukv"].shape),
    ]
    args = [x, mod, g.reshape(1, d), w["w_in"], w["q_norm"].reshape(1, -1), w["kv_norm"].reshape(1, -1),
            w["w_uq"], w["w_ukv"]]
    if rope is not None:
        in_specs += [pl.BlockSpec((tm, LANES), lambda bi, ti: (ti, 0))] * 2
        args += list(rope)
    hv = A_HEADS * A_V
    return pl.pallas_call(
        functools.partial(_mla_proj_kernel, use_rope=rope is not None,
                          sm_scale=float((A_NOPE + A_ROPE) ** -0.5 * LOG2E)),
        grid=(b, t // tm),
        in_specs=in_specs,
        out_specs=[
            pl.BlockSpec((1, tm, 2 * LANES * A_HEADS), row),
            pl.BlockSpec((1, tm, A_HEADS * A_NOPE), row),
            pl.BlockSpec((1, tm, LANES), row),
            pl.BlockSpec((1, hv, tm), lambda bi, ti: (bi, 0, ti)),
        ],
        out_shape=[
            jax.ShapeDtypeStruct((b, t, 2 * LANES * A_HEADS), BF16),
            jax.ShapeDtypeStruct((b, t, A_HEADS * A_NOPE), BF16),
            jax.ShapeDtypeStruct((b, t, LANES), BF16),
            jax.ShapeDtypeStruct((b, hv, t), BF16),
        ],
        compiler_params=_params("parallel", "parallel"),
        name="mla_proj",
    )(*args)


def _gqa_proj_kernel(*refs, use_rope, sm_scale):
    it = iter(refs)
    x_ref, mod_ref, g_ref, w_ref, qg_ref, kg_ref = (next(it) for _ in range(6))
    if use_rope:
        cos_ref, sin_ref = next(it), next(it)
    q_ref, k_ref, vt_ref = (next(it) for _ in range(3))

    mod = mod_ref[0]
    h = _modulated_norm(x_ref[0], g_ref[...], mod[4:5], mod[3:4]).astype(BF16)
    proj = _dot(h, w_ref[...])
    qw = B_HEADS * B_HEAD_DIM
    kw = B_KV_HEADS * B_HEAD_DIM
    if use_rope:
        cos, sin = cos_ref[...], sin_ref[...]

    def head(col, gain, scale):
        y = _rms(proj[:, col:col + B_HEAD_DIM]) * gain
        if use_rope:
            y = _rope_pairs(y, cos, sin, B_HEAD_DIM // 2)
        return (y * scale).astype(BF16) if scale != 1.0 else y.astype(BF16)

    for hd in range(B_HEADS):
        q_ref[0, :, hd * B_HEAD_DIM:(hd + 1) * B_HEAD_DIM] = head(hd * B_HEAD_DIM, qg_ref[...], sm_scale)
    for hd in range(B_KV_HEADS):
        k_ref[0, :, hd * B_HEAD_DIM:(hd + 1) * B_HEAD_DIM] = head(qw + hd * B_HEAD_DIM, kg_ref[...], 1.0)
    vt_ref[0] = proj[:, qw + kw:].T.astype(BF16)


def _gqa_proj(x, mod, mod_row, g, w, rope):
    b, t, d = x.shape
    tm = min(GQA_ROW_TILE, t)
    row = lambda bi, ti: (bi, ti, 0)
    qw, kw = B_HEADS * B_HEAD_DIM, B_KV_HEADS * B_HEAD_DIM
    in_specs = [
        pl.BlockSpec((1, tm, d), row),
        pl.BlockSpec((1, N_MOD, d), lambda bi, ti: (mod_row(bi), 0, 0)),
        _const_spec((1, d)),
        _const_spec(w["w_qkv"].shape), _const_spec((1, B_HEAD_DIM)), _const_spec((1, B_HEAD_DIM)),
    ]
    args = [x, mod, g.reshape(1, d), w["w_qkv"], w["q_norm"].reshape(1, -1), w["k_norm"].reshape(1, -1)]
    if rope is not None:
        in_specs += [pl.BlockSpec((tm, LANES), lambda bi, ti: (ti, 0))] * 2
        args += list(rope)
    return pl.pallas_call(
        functools.partial(_gqa_proj_kernel, use_rope=rope is not None, sm_scale=float(B_HEAD_DIM ** -0.5 * LOG2E)),
        grid=(b, t // tm),
        in_specs=in_specs,
        out_specs=[
            pl.BlockSpec((1, tm, qw), row),
            pl.BlockSpec((1, tm, kw), row),
            pl.BlockSpec((1, kw, tm), lambda bi, ti: (bi, 0, ti)),
        ],
        out_shape=[
            jax.ShapeDtypeStruct((b, t, qw), BF16),
            jax.ShapeDtypeStruct((b, t, kw), BF16),
            jax.ShapeDtypeStruct((b, kw, t), BF16),
        ],
        compiler_params=_params("parallel", "parallel"),
        name="gqa_proj",
    )(*args)


def _nb_proj_kernel(x_ref, mod_ref, g_ref, w_ref, q_ref, k_ref, vt_ref, *, sm_scale):
    mod = mod_ref[0]
    h = _modulated_norm(x_ref[0], g_ref[...], mod[4:5], mod[3:4]).astype(BF16)
    proj = _dot(h, w_ref[...])
    hd = C_HEADS * C_HEAD_DIM
    q_ref[0] = (proj[:, :hd] * sm_scale).astype(BF16)
    k_ref[0] = proj[:, hd:2 * hd].astype(BF16)
    vt_ref[0] = proj[:, 2 * hd:].T.astype(BF16)


def _nb_proj(x, mod, mod_row, g, w):
    b, t, d = x.shape
    tm = min(ROW_TILE, t)
    row = lambda bi, ti: (bi, ti, 0)
    hd = C_HEADS * C_HEAD_DIM
    return pl.pallas_call(
        functools.partial(_nb_proj_kernel, sm_scale=float(C_HEAD_DIM ** -0.5 * LOG2E)),
        grid=(b, t // tm),
        in_specs=[
            pl.BlockSpec((1, tm, d), row),
            pl.BlockSpec((1, N_MOD, d), lambda bi, ti: (mod_row(bi), 0, 0)),
            _const_spec((1, d)),
            _const_spec(w["w_qkv"].shape),
        ],
        out_specs=[
            pl.BlockSpec((1, tm, hd), row),
            pl.BlockSpec((1, tm, hd), row),
            pl.BlockSpec((1, hd, tm), lambda bi, ti: (bi, 0, ti)),
        ],
        out_shape=[
            jax.ShapeDtypeStruct((b, t, hd), BF16),
            jax.ShapeDtypeStruct((b, t, hd), BF16),
            jax.ShapeDtypeStruct((b, hd, t), BF16),
        ],
        compiler_params=_params("parallel", "parallel"),
        name="nb_proj",
    )(x, mod, g.reshape(1, d), w["w_qkv"])


def _attn_kernel(*refs, key_rows, per_head, n_kv, hps, dq, q_tile, n_tiles):
    n_groups, k_pieces = len(key_rows), len(per_head)
    it = iter(refs)
    q_ref = next(it)
    k_refs = [[next(it) for _ in range(k_pieces)] for _ in range(n_groups)]
    v_refs = [next(it) for _ in range(n_groups)]
    o_ref = next(it)
    k_scr, v_scr, s_a, s_b, m_a, m_b = (next(it) for _ in range(6))

    r0 = 0
    for gi, rows in enumerate(key_rows):
        for j in range(n_kv):
            for pi in range(k_pieces):
                lanes = slice(j * LANES, (j + 1) * LANES) if per_head[pi] else slice(0, LANES)
                k_scr[j, r0:r0 + rows, pi * LANES:(pi + 1) * LANES] = k_refs[gi][pi][0, :, lanes]
            v_scr[j, :, r0:r0 + rows] = v_refs[gi][0, j * LANES:(j + 1) * LANES, :]
        r0 += rows

    n_units = hps * n_tiles

    def unit(u):
        if hps == 1:
            head, tile = 0, u
        else:
            head, tile = u // n_tiles, u % n_tiles
        rows = pl.ds(pl.multiple_of(tile * q_tile, q_tile), q_tile)
        return head, rows, (head if n_kv > 1 else 0)

    def scores(u, s_buf, m_buf):
        head, rows, kv = unit(u)
        q = q_ref[0, rows, pl.ds(pl.multiple_of(head * dq, dq), dq)]
        s_t = _dot_nt(k_scr[kv], q)
        s_buf[...] = s_t
        m_buf[...] = jnp.max(s_t, axis=0, keepdims=True)

    def softmax_pv(u, s_buf, m_buf):
        head, rows, kv = unit(u)
        p = jnp.exp2(s_buf[...] - m_buf[...])
        l = jnp.sum(p, axis=0, keepdims=True)
        o_t = _dot(v_scr[kv], p.astype(BF16)) / l
        o_ref[0, rows, pl.ds(pl.multiple_of(head * LANES, LANES), LANES)] = o_t.T.astype(BF16)

    scores(0, s_a, m_a)
    if n_units == 1:
        softmax_pv(0, s_a, m_a)
        return

    def pair(i, carry):
        u = 2 * i
        scores(u + 1, s_b, m_b)
        softmax_pv(u, s_a, m_a)
        scores(jnp.minimum(u + 2, n_units - 1), s_a, m_a)
        softmax_pv(u + 1, s_b, m_b)
        return carry

    lax.fori_loop(0, n_units // 2, pair, 0)


def _attention(q, k_groups, v_groups, *, heads, hps, per_head, share_kv):
    b, tq, qw = q.shape
    dq = qw // heads
    q_tile = min(Q_TILE, tq)
    n_tiles = tq // q_tile
    n_kv = 1 if share_kv else hps
    assert heads % hps == 0 and (hps * n_tiles == 1 or (hps * n_tiles) % 2 == 0)
    key_rows = tuple(g[0].shape[1] for g in k_groups)
    k_pieces = len(per_head)
    s_total = sum(key_rows)
    in_specs = [pl.BlockSpec((1, tq, hps * dq), lambda bi, hg: (bi, 0, hg))]
    args = [q]
    for grp in k_groups:
        for pi, arr in enumerate(grp):
            width = n_kv * LANES if per_head[pi] else LANES
            moves = per_head[pi] or share_kv
            in_specs.append(pl.BlockSpec((1, arr.shape[1], width),
                                         lambda bi, hg, moves=moves: (bi, 0, hg if moves else 0)))
            args.append(arr)
    for arr in v_groups:
        in_specs.append(pl.BlockSpec((1, n_kv * LANES, arr.shape[2]), lambda bi, hg: (bi, hg, 0)))
        args.append(arr)
    return pl.pallas_call(
        functools.partial(_attn_kernel, key_rows=key_rows, per_head=tuple(per_head), n_kv=n_kv, hps=hps,
                          dq=dq, q_tile=q_tile, n_tiles=n_tiles),
        grid=(b, heads // hps),
        in_specs=in_specs,
        out_specs=pl.BlockSpec((1, tq, hps * LANES), lambda bi, hg: (bi, 0, hg)),
        out_shape=jax.ShapeDtypeStruct((b, tq, heads * LANES), BF16),
        scratch_shapes=[
            pltpu.VMEM((n_kv, s_total, k_pieces * LANES), BF16), pltpu.VMEM((n_kv, LANES, s_total), BF16),
            pltpu.VMEM((s_total, q_tile), F32), pltpu.VMEM((s_total, q_tile), F32),
            pltpu.VMEM((1, q_tile), F32), pltpu.VMEM((1, q_tile), F32),
        ],
        compiler_params=_params("parallel", "parallel"),
        name="attention",
    )(*args)


def _nb_attn_kernel(*refs, has_window, rows, tile, n_groups):
    it = iter(refs)
    q_ref, kc_ref, vc_ref = next(it), next(it), next(it)
    if has_window:
        kl_ref, vl_ref, bias_ref = next(it), next(it), next(it)
    o_ref = next(it)
    s_a, s_b, m_a, m_b = (next(it) for _ in range(4))
    c = kc_ref.shape[1]
    win = NB_KEY_ROWS * GRID_W
    lane = lax.broadcasted_iota(jnp.int32, (1, LANES), 1)

    def tile_rows(g):
        return pl.ds(pl.multiple_of(g * tile, tile), tile)

    def window(g):
        first_row = jnp.clip(NB_ROWS * g - C_WIN_ROWS // 2, 0, rows - NB_KEY_ROWS)
        return pl.ds(pl.multiple_of(first_row * GRID_W, 2 * LANES), win)

    def scores(g, s_buf, m_buf):
        q = q_ref[0, tile_rows(g), :]
        zero = jnp.zeros_like(q)
        q2 = jnp.concatenate([jnp.where(lane < C_HEAD_DIM, q, zero), jnp.where(lane >= C_HEAD_DIM, q, zero)],
                             axis=0)
        s_c = _dot_nt(kc_ref[0], q2)
        s_buf[0:c, :] = s_c
        m = jnp.max(s_c, axis=0, keepdims=True)
        if has_window:
            variant = jnp.where(g == 0, 0, jnp.where(g == n_groups - 1, 2, 1))
            s_w = _dot_nt(kl_ref[0, window(g), :], q2)
            for e in range(2):
                s_e = s_w[:, e * tile:(e + 1) * tile] + bias_ref[variant, e]
                s_buf[c:, e * tile:(e + 1) * tile] = s_e
                m_e = jnp.maximum(m[:, e * tile:(e + 1) * tile], jnp.max(s_e, axis=0, keepdims=True))
                m_buf[:, e * tile:(e + 1) * tile] = m_e
        else:
            m_buf[...] = m

    def softmax_pv(g, s_buf, m_buf):
        p = jnp.exp2(s_buf[...] - m_buf[...])
        l = jnp.sum(p, axis=0, keepdims=True)
        pb = p.astype(BF16)
        o2 = _dot(vc_ref[0], pb[0:c])
        if has_window:
            o2 = o2 + _dot(vl_ref[0, :, window(g)], pb[c:])
        o_t = jnp.concatenate([o2[e * C_HEAD_DIM:(e + 1) * C_HEAD_DIM, e * tile:(e + 1) * tile]
                               / l[:, e * tile:(e + 1) * tile] for e in range(2)], axis=0)
        o_ref[0, tile_rows(g), :] = o_t.T.astype(BF16)

    scores(0, s_a, m_a)
    if n_groups == 1:
        softmax_pv(0, s_a, m_a)
        return

    def pair(i, carry):
        g = 2 * i
        scores(g + 1, s_b, m_b)
        softmax_pv(g, s_a, m_a)
        scores(jnp.minimum(g + 2, n_groups - 1), s_a, m_a)
        softmax_pv(g + 1, s_b, m_b)
        return carry

    lax.fori_loop(0, n_groups // 2, pair, 0)


def _nb_attention(q, k_ctx, vt_ctx, k_lat=None, vt_lat=None, bias=None):
    b, tq, hw = q.shape
    pairs = hw // LANES
    has_window = k_lat is not None
    c = k_ctx.shape[1]
    if has_window:
        t = k_lat.shape[1]
        rows = t // GRID_W
        tile = NB_ROWS * GRID_W
        keys = c + NB_KEY_ROWS * GRID_W
    else:
        rows, tile, keys = 0, tq, c
    n_groups = tq // tile
    assert n_groups == 1 or n_groups % 2 == 0
    in_specs = [
        pl.BlockSpec((1, tq, LANES), lambda bi, p: (bi, 0, p)),
        pl.BlockSpec((1, c, LANES), lambda bi, p: (bi, 0, p)),
        pl.BlockSpec((1, LANES, c), lambda bi, p: (bi, p, 0)),
    ]
    args = [q, k_ctx, vt_ctx]
    if has_window:
        in_specs += [
            pl.BlockSpec((1, t, LANES), lambda bi, p: (bi, 0, p)),
            pl.BlockSpec((1, LANES, t), lambda bi, p: (bi, p, 0)),
            pl.BlockSpec((3, 2, NB_KEY_ROWS * GRID_W, tile), lambda bi, p: (0, p, 0, 0)),
        ]
        args += [k_lat, vt_lat, bias]
    return pl.pallas_call(
        functools.partial(_nb_attn_kernel, has_window=has_window, rows=rows, tile=tile, n_groups=n_groups),
        grid=(b, pairs),
        in_specs=in_specs,
        out_specs=pl.BlockSpec((1, tq, LANES), lambda bi, p: (bi, 0, p)),
        out_shape=jax.ShapeDtypeStruct((b, tq, hw), BF16),
        scratch_shapes=[pltpu.VMEM((keys, 2 * tile), F32), pltpu.VMEM((keys, 2 * tile), F32),
                        pltpu.VMEM((1, 2 * tile), F32), pltpu.VMEM((1, 2 * tile), F32)],
        compiler_params=_params("parallel", "parallel"),
        name="nb_attention",
    )(*args)


def _nb_bias_table(rpb, rows):
    heads = rpb.shape[0]
    tile_q = NB_ROWS * GRID_W
    return pl.pallas_call(
        functools.partial(_nb_bias_kernel, rows=rows),
        grid=(heads,),
        in_specs=[pl.BlockSpec(memory_space=pltpu.SMEM)],
        out_specs=pl.BlockSpec((3, 1, NB_KEY_ROWS * GRID_W, tile_q), lambda h: (0, h, 0, 0)),
        out_shape=jax.ShapeDtypeStruct((3, heads, NB_KEY_ROWS * GRID_W, tile_q), F32),
        compiler_params=_params("parallel"),
        name="nb_bias",
    )(rpb.reshape(-1))


def _nb_bias_kernel(rpb_ref, o_ref, *, rows):
    n_a, n_b = 2 * C_WIN_ROWS - 1, 2 * C_WIN_COLS - 1
    tile_q = NB_ROWS * GRID_W
    shape = (GRID_W, tile_q)
    kc = lax.broadcasted_iota(jnp.int32, shape, 0)
    lane = lax.broadcasted_iota(jnp.int32, shape, 1)
    qc = lane % GRID_W
    qi = lane // GRID_W
    c0 = jnp.clip(qc - C_WIN_COLS // 2, 0, GRID_W - C_WIN_COLS)
    col_ok = (kc >= c0) & (kc < c0 + C_WIN_COLS)
    dcol = kc - qc + (C_WIN_COLS - 1)
    base = pl.program_id(0) * (n_a * n_b)
    masked = jnp.full(shape, MASK_VALUE, F32)
    planes = []
    for a in range(n_a):
        acc = masked
        for bb in range(n_b):
            acc = jnp.where(dcol == bb, rpb_ref[base + a * n_b + bb], acc)
        planes.append(jnp.where(col_ok, acc * LOG2E, MASK_VALUE))
    groups = rows // NB_ROWS
    for v, g in enumerate((0, 1, groups - 1)):
        first_key_row = min(max(NB_ROWS * g - C_WIN_ROWS // 2, 0), rows - NB_KEY_ROWS)
        for j in range(NB_KEY_ROWS):
            kr = first_key_row + j
            blk = masked
            for i in range(NB_ROWS):
                qr = NB_ROWS * g + i
                r0 = min(max(qr - C_WIN_ROWS // 2, 0), rows - C_WIN_ROWS)
                if r0 <= kr < r0 + C_WIN_ROWS:
                    blk = jnp.where(qi == i, planes[kr - qr + C_WIN_ROWS - 1], blk)
            o_ref[v, 0, j * GRID_W:(j + 1) * GRID_W, :] = blk


def _rope_tables(rows, rot_dim):
    n = rot_dim // 4
    inv_freq = ROPE_THETA ** (-jnp.arange(n, dtype=F32) / n)
    t = jnp.arange(rows * GRID_W, dtype=jnp.int32)
    r = (t // GRID_W).astype(F32)
    col = (t % GRID_W).astype(F32)
    ang = jnp.concatenate([r[:, None] * inv_freq[None, :], col[:, None] * inv_freq[None, :]], axis=-1)
    cos, sin = jnp.cos(ang), jnp.sin(ang)
    reps = LANES // rot_dim
    return (jnp.tile(jnp.concatenate([cos, cos], axis=-1), (1, reps)),
            jnp.tile(jnp.concatenate([-sin, sin], axis=-1), (1, reps)))


def _mla_weights(w_in, q_norm, kv_norm, w_uq, w_ukv, w_o):
    rank = A_Q_RANK + A_KV_RANK
    k_r = w_in[:, rank:]
    uq = w_uq.reshape(A_Q_RANK, A_HEADS, A_NOPE + A_ROPE)
    ukv = w_ukv.reshape(A_KV_RANK, A_HEADS, A_NOPE + A_V)
    return {
        "w_in": jnp.concatenate([w_in[:, :rank], k_r, k_r], axis=1).astype(BF16),
        "q_norm": q_norm, "kv_norm": kv_norm,
        "w_uq": jnp.concatenate([uq[:, :, :A_NOPE].reshape(A_Q_RANK, -1),
                                 uq[:, :, A_NOPE:].reshape(A_Q_RANK, -1)], axis=1).astype(BF16),
        "w_ukv": jnp.concatenate([ukv[:, :, :A_NOPE].reshape(A_KV_RANK, -1),
                                  ukv[:, :, A_NOPE:].reshape(A_KV_RANK, -1)], axis=1).astype(BF16),
        "w_o": w_o.astype(BF16),
    }


def kernel(x, c, ctx, c_ctx, norm_g, w_mod, b_mod, ffn1_w13, ffn1_w2, ffn2_w13, ffn2_w2, a_w_in, a_q_norm, a_kv_norm, a_w_uq, a_w_ukv, a_w_o, b_w_qkv, b_q_norm, b_k_norm, b_w_o, c_w_qkv, c_rpb, c_w_o, final_norm_g):
    b, t, d = x.shape
    depth = w_mod.shape[0]
    rows = t // GRID_W
    assert b < MOD_ROWS and t % (NB_ROWS * GRID_W) == 0 and rows >= NB_KEY_ROWS + NB_ROWS

    c_rows = jnp.zeros((MOD_ROWS, d), F32).at[:b].set(c).at[b].set(c_ctx)
    mod_all = _modulation(c_rows, w_mod, b_mod).reshape(depth, MOD_ROWS, N_MOD, d)
    lat_row = lambda bi: bi
    ctx_row = lambda bi: b

    rope_a = _rope_tables(rows, A_ROPE)
    rope_b = _rope_tables(rows, B_HEAD_DIM)

    xc = ctx
    for i in range(depth):
        ctx_out = i < depth - 1
        last = i == depth - 1
        mod = mod_all[i]
        kind, j = i % N_MIXERS, i // N_MIXERS
        w13_1, w2_1 = ffn1_w13[i].astype(BF16), ffn1_w2[i].astype(BF16)
        w13_2, w2_2 = ffn2_w13[i].astype(BF16), ffn2_w2[i].astype(BF16)

        x = _ffn(x, mod, lat_row, norm_g[i, 0], w13_1, w2_1, k0=0)
        xc = _ffn(xc, mod, ctx_row, norm_g[i, 0], w13_1, w2_1, k0=0)

        if kind == 0:
            w = _mla_weights(a_w_in[j], a_q_norm[j], a_kv_norm[j], a_w_uq[j], a_w_ukv[j], a_w_o[j])
            q, kn, kr, vt = _mla_proj(x, mod, lat_row, norm_g[i, 1], w, rope_a)
            qc, knc, krc, vtc = _mla_proj(xc, mod, ctx_row, norm_g[i, 1], w, None)
            cfg = dict(heads=A_HEADS, hps=A_HEADS_PER_STEP, per_head=(True, False), share_kv=False)
            o = _attention(q, [[knc, krc], [kn, kr]], [vtc, vt], **cfg)
            if ctx_out:
                oc = _attention(qc, [[knc, krc]], [vtc], **cfg)
            w_o = w["w_o"]
        elif kind == 1:
            w = {"w_qkv": b_w_qkv[j].astype(BF16), "q_norm": b_q_norm[j], "k_norm": b_k_norm[j]}
            q, k, vt = _gqa_proj(x, mod, lat_row, norm_g[i, 1], w, rope_b)
            qc, kc, vtc = _gqa_proj(xc, mod, ctx_row, norm_g[i, 1], w, None)
            cfg = dict(heads=B_HEADS, hps=B_HEADS // B_KV_HEADS, per_head=(False,), share_kv=True)
            o = _attention(q, [[kc], [k]], [vtc, vt], **cfg)
            if ctx_out:
                oc = _attention(qc, [[kc]], [vtc], **cfg)
            w_o = b_w_o[j].astype(BF16)
        else:
            w = {"w_qkv": c_w_qkv[j].astype(BF16)}
            q, k, vt = _nb_proj(x, mod, lat_row, norm_g[i, 1], w)
            qc, kc, vtc = _nb_proj(xc, mod, ctx_row, norm_g[i, 1], w)
            o = _nb_attention(q, kc, vtc, k, vt, _nb_bias_table(c_rpb[j], rows))
            if ctx_out:
                oc = _nb_attention(qc, kc, vtc)
            w_o = c_w_o[j].astype(BF16)

        x = _ffn(x, mod, lat_row, norm_g[i, 2], w13_2, w2_2, k0=6, attn=o, w_o=w_o,
                 final_g=final_norm_g if last else None)
        if ctx_out:
            xc = _ffn(xc, mod, ctx_row, norm_g[i, 2], w13_2, w2_2, k0=6, attn=oc, w_o=w_o)
    return x
```

```python
import functools

import jax
import jax.numpy as jnp
from jax import lax
from jax.experimental import pallas as pl
from jax.experimental.pallas import tpu as pltpu

F32 = jnp.float32
BF16 = jnp.bfloat16

GRID_W = 64
N_MIXERS = 3
N_MOD = 9
EPS = 1e-6
ROPE_THETA = 10000.0

A_HEADS, A_NOPE, A_ROPE, A_V = 8, 128, 64, 128
A_Q_RANK, A_KV_RANK = 384, 256
B_HEADS, B_KV_HEADS, B_HEAD_DIM = 8, 2, 128
C_HEADS, C_HEAD_DIM, C_WIN_ROWS, C_WIN_COLS = 16, 64, 8, 16

LANES = 128
MOD_ROWS = 16
MASK_VALUE = -1e30
LOG2E = 1.4426950408889634
VMEM_LIMIT = 56 * 1024 * 1024

ROW_TILE = 512
GQA_ROW_TILE = 256
A_HEADS_PER_STEP = 2
Q_TILE = 512
KEY_CHUNK = 512
NB_CHUNK = 256
NB_ROWS = 4
NB_KEY_ROWS = 12


def _params(*sem):
    return pltpu.CompilerParams(dimension_semantics=sem, vmem_limit_bytes=VMEM_LIMIT)


def _const_spec(shape):
    nd = len(shape)
    return pl.BlockSpec(shape, lambda *_: (0,) * nd, pipeline_mode=pl.Buffered(1))


def _dot(a, b):
    return jnp.dot(a, b, preferred_element_type=F32)


def _dot_nt(a, b):
    return lax.dot_general(a, b, (((1,), (1,)), ((), ())), preferred_element_type=F32)


def _rms(x):
    return x * lax.rsqrt(jnp.mean(x * x, axis=-1, keepdims=True) + EPS)


def _modulated_norm(x, g, scale, shift):
    return (_rms(x) * g) * (1.0 + scale) + shift


def _silu(x):
    return x / (1.0 + jnp.exp(-x))


def _mod_kernel(c_ref, w_ref, b_ref, o_ref):
    sc = _silu(c_ref[...]).astype(BF16)
    o_ref[0] = _dot(sc, w_ref[0].astype(BF16)) + b_ref[0]


def _modulation(c_rows, w_mod, b_mod):
    depth, d, n = w_mod.shape
    tn = d
    return pl.pallas_call(
        _mod_kernel,
        grid=(depth, n // tn),
        in_specs=[
            pl.BlockSpec((MOD_ROWS, d), lambda i, j: (0, 0)),
            pl.BlockSpec((1, d, tn), lambda i, j: (i, 0, j)),
            pl.BlockSpec((1, 1, tn), lambda i, j: (i, 0, j)),
        ],
        out_specs=pl.BlockSpec((1, MOD_ROWS, tn), lambda i, j: (i, 0, j)),
        out_shape=jax.ShapeDtypeStruct((depth, MOD_ROWS, n), F32),
        compiler_params=_params("parallel", "parallel"),
        name="modulation",
    )(c_rows, w_mod, b_mod.reshape(depth, 1, n))


def _ffn_kernel(*refs, k0, d_ff, has_oproj, final_norm):
    it = iter(refs)
    x_ref, mod_ref, g_ref, w13_ref, w2_ref = (next(it) for _ in range(5))
    if has_oproj:
        a_ref, wo_ref = next(it), next(it)
    if final_norm:
        fg_ref = next(it)
    out_ref = next(it)

    x = x_ref[0]
    mod = mod_ref[0]
    if has_oproj:
        x = x + mod[5:6] * _dot(a_ref[0], wo_ref[...])
    h = _modulated_norm(x, g_ref[...], mod[k0 + 1:k0 + 2], mod[k0:k0 + 1]).astype(BF16)
    hgu = _dot(h, w13_ref[...])
    act = (_silu(hgu[:, :d_ff]) * hgu[:, d_ff:]).astype(BF16)
    y = x + (0.5 * mod[k0 + 2:k0 + 3]) * _dot(act, w2_ref[...])
    if final_norm:
        y = _rms(y) * fg_ref[...]
    out_ref[0] = y


def _ffn(x, mod, mod_row, g, w13, w2, *, k0, attn=None, w_o=None, final_g=None):
    b, t, d = x.shape
    d_ff = w2.shape[0]
    tm = min(ROW_TILE, t)
    row = lambda bi, ti: (bi, ti, 0)
    in_specs = [
        pl.BlockSpec((1, tm, d), row),
        pl.BlockSpec((1, N_MOD, d), lambda bi, ti: (mod_row(bi), 0, 0)),
        _const_spec((1, d)),
        _const_spec(w13.shape),
        _const_spec(w2.shape),
    ]
    args = [x, mod, g.reshape(1, d), w13, w2]
    if attn is not None:
        in_specs += [pl.BlockSpec((1, tm, attn.shape[2]), row), _const_spec(w_o.shape)]
        args += [attn, w_o]
    if final_g is not None:
        in_specs.append(_const_spec((1, d)))
        args.append(final_g.reshape(1, d))
    return pl.pallas_call(
        functools.partial(_ffn_kernel, k0=k0, d_ff=d_ff, has_oproj=attn is not None,
                          final_norm=final_g is not None),
        grid=(b, t // tm),
        in_specs=in_specs,
        out_specs=pl.BlockSpec((1, tm, d), row),
        out_shape=jax.ShapeDtypeStruct((b, t, d), F32),
        compiler_params=_params("parallel", "parallel"),
        name="half_ffn",
    )(*args)


def _rope_pairs(x, cos, sin_signed, half):
    if 2 * half == LANES:
        rot = pltpu.roll(x, half, 1)
    else:
        lane = lax.broadcasted_iota(jnp.int32, x.shape, 1)
        first = (lane % (2 * half)) < half
        rot = jnp.where(first, pltpu.roll(x, LANES - half, 1), pltpu.roll(x, half, 1))
    return x * cos + rot * sin_signed


def _mla_proj_kernel(*refs, use_rope, sm_scale):
    it = iter(refs)
    x_ref, mod_ref, g_ref, w_in_ref, qg_ref, kvg_ref, w_uq_ref, w_ukv_ref = (next(it) for _ in range(8))
    if use_rope:
        cos_ref, sin_ref = next(it), next(it)
    q_ref, kn_ref, kr_ref, vt_ref = (next(it) for _ in range(4))

    mod = mod_ref[0]
    h = _modulated_norm(x_ref[0], g_ref[...], mod[4:5], mod[3:4]).astype(BF16)
    proj = _dot(h, w_in_ref[...])
    c_q = (_rms(proj[:, :A_Q_RANK]) * qg_ref[...]).astype(BF16)
    c_kv = (_rms(proj[:, A_Q_RANK:A_Q_RANK + A_KV_RANK]) * kvg_ref[...]).astype(BF16)
    k_r = proj[:, A_Q_RANK + A_KV_RANK:]
    q_all = _dot(c_q, w_uq_ref[...])
    kv = _dot(c_kv, w_ukv_ref[...])
    n_nope = A_HEADS * A_NOPE
    if use_rope:
        cos, sin = cos_ref[...], sin_ref[...]
        k_r = _rope_pairs(k_r, cos, sin, A_ROPE // 2)
    lane = lax.broadcasted_iota(jnp.int32, (1, LANES), 1)
    for j in range(A_HEADS // 2):
        qr = q_all[:, n_nope + j * LANES:n_nope + (j + 1) * LANES]
        if use_rope:
            qr = _rope_pairs(qr, cos, sin, A_ROPE // 2)
        qr = qr * sm_scale
        for e in range(2):
            hd = 2 * j + e
            keep = (lane < A_ROPE) if e == 0 else (lane >= A_ROPE)
            q_ref[0, :, 2 * hd * LANES:(2 * hd + 1) * LANES] = (
                q_all[:, hd * A_NOPE:(hd + 1) * A_NOPE] * sm_scale).astype(BF16)
            q_ref[0, :, (2 * hd + 1) * LANES:(2 * hd + 2) * LANES] = jnp.where(keep, qr, 0.0).astype(BF16)
    kn_ref[0] = kv[:, :n_nope].astype(BF16)
    kr_ref[0] = k_r.astype(BF16)
    vt_ref[0] = kv[:, n_nope:].T.astype(BF16)


def _mla_proj(x, mod, mod_row, g, w, rope):
    b, t, d = x.shape
    tm = min(ROW_TILE, t)
    row = lambda bi, ti: (bi, ti, 0)
    in_specs = [
        pl.BlockSpec((1, tm, d), row),
        pl.BlockSpec((1, N_MOD, d), lambda bi, ti: (mod_row(bi), 0, 0)),
        _const_spec((1, d)),
        _const_spec(w["w_in"].shape), _const_spec((1, A_Q_RANK)), _const_spec((1, A_KV_RANK)),
        _const_spec(w["w_uq"].shape), _const_spec(w["w_ukv"].shape),
    ]
    args = [x, mod, g.reshape(1, d), w["w_in"], w["q_norm"].reshape(1, -1), w["kv_norm"].reshape(1, -1),
            w["w_uq"], w["w_ukv"]]
    if rope is not None:
        in_specs += [pl.BlockSpec((tm, LANES), lambda bi, ti: (ti, 0))] * 2
        args += list(rope)
    hv = A_HEADS * A_V
    return pl.pallas_call(
        functools.partial(_mla_proj_kernel, use_rope=rope is not None,
                          sm_scale=float((A_NOPE + A_ROPE) ** -0.5 * LOG2E)),
        grid=(b, t // tm),
        in_specs=in_specs,
        out_specs=[
            pl.BlockSpec((1, tm, 2 * LANES * A_HEADS), row),
            pl.BlockSpec((1, tm, A_HEADS * A_NOPE), row),
            pl.BlockSpec((1, tm, LANES), row),
            pl.BlockSpec((1, hv, tm), lambda bi, ti: (bi, 0, ti)),
        ],
        out_shape=[
            jax.ShapeDtypeStruct((b, t, 2 * LANES * A_HEADS), BF16),
            jax.ShapeDtypeStruct((b, t, A_HEADS * A_NOPE), BF16),
            jax.ShapeDtypeStruct((b, t, LANES), BF16),
            jax.ShapeDtypeStruct((b, hv, t), BF16),
        ],
        compiler_params=_params("parallel", "parallel"),
        name="mla_proj",
    )(*args)


def _gqa_proj_kernel(*refs, use_rope, sm_scale):
    it = iter(refs)
    x_ref, mod_ref, g_ref, w_ref, qg_ref, kg_ref = (next(it) for _ in range(6))
    if use_rope:
        cos_ref, sin_ref = next(it), next(it)
    q_ref, k_ref, vt_ref = (next(it) for _ in range(3))

    mod = mod_ref[0]
    h = _modulated_norm(x_ref[0], g_ref[...], mod[4:5], mod[3:4]).astype(BF16)
    proj = _dot(h, w_ref[...])
    qw = B_HEADS * B_HEAD_DIM
    kw = B_KV_HEADS * B_HEAD_DIM
    if use_rope:
        cos, sin = cos_ref[...], sin_ref[...]

    def head(col, gain, scale):
        y = _rms(proj[:, col:col + B_HEAD_DIM]) * gain
        if use_rope:
            y = _rope_pairs(y, cos, sin, B_HEAD_DIM // 2)
        return (y * scale).astype(BF16) if scale != 1.0 else y.astype(BF16)

    for hd in range(B_HEADS):
        q_ref[0, :, hd * B_HEAD_DIM:(hd + 1) * B_HEAD_DIM] = head(hd * B_HEAD_DIM, qg_ref[...], sm_scale)
    for hd in range(B_KV_HEADS):
        k_ref[0, :, hd * B_HEAD_DIM:(hd + 1) * B_HEAD_DIM] = head(qw + hd * B_HEAD_DIM, kg_ref[...], 1.0)
    vt_ref[0] = proj[:, qw + kw:].T.astype(BF16)


def _gqa_proj(x, mod, mod_row, g, w, rope):
    b, t, d = x.shape
    tm = min(GQA_ROW_TILE, t)
    row = lambda bi, ti: (bi, ti, 0)
    qw, kw = B_HEADS * B_HEAD_DIM, B_KV_HEADS * B_HEAD_DIM
    in_specs = [
        pl.BlockSpec((1, tm, d), row),
        pl.BlockSpec((1, N_MOD, d), lambda bi, ti: (mod_row(bi), 0, 0)),
        _const_spec((1, d)),
        _const_spec(w["w_qkv"].shape), _const_spec((1, B_HEAD_DIM)), _const_spec((1, B_HEAD_DIM)),
    ]
    args = [x, mod, g.reshape(1, d), w["w_qkv"], w["q_norm"].reshape(1, -1), w["k_norm"].reshape(1, -1)]
    if rope is not None:
        in_specs += [pl.BlockSpec((tm, LANES), lambda bi, ti: (ti, 0))] * 2
        args += list(rope)
    return pl.pallas_call(
        functools.partial(_gqa_proj_kernel, use_rope=rope is not None, sm_scale=float(B_HEAD_DIM ** -0.5 * LOG2E)),
        grid=(b, t // tm),
        in_specs=in_specs,
        out_specs=[
            pl.BlockSpec((1, tm, qw), row),
            pl.BlockSpec((1, tm, kw), row),
            pl.BlockSpec((1, kw, tm), lambda bi, ti: (bi, 0, ti)),
        ],
        out_shape=[
            jax.ShapeDtypeStruct((b, t, qw), BF16),
            jax.ShapeDtypeStruct((b, t, kw), BF16),
            jax.ShapeDtypeStruct((b, kw, t), BF16),
        ],
        compiler_params=_params("parallel", "parallel"),
        name="gqa_proj",
    )(*args)


def _nb_proj_kernel(x_ref, mod_ref, g_ref, w_ref, q_ref, k_ref, vt_ref, *, sm_scale):
    mod = mod_ref[0]
    h = _modulated_norm(x_ref[0], g_ref[...], mod[4:5], mod[3:4]).astype(BF16)
    proj = _dot(h, w_ref[...])
    hd = C_HEADS * C_HEAD_DIM
    q_ref[0] = (proj[:, :hd] * sm_scale).astype(BF16)
    k_ref[0] = proj[:, hd:2 * hd].astype(BF16)
    vt_ref[0] = proj[:, 2 * hd:].T.astype(BF16)


def _nb_proj(x, mod, mod_row, g, w):
    b, t, d = x.shape
    tm = min(ROW_TILE, t)
    row = lambda bi, ti: (bi, ti, 0)
    hd = C_HEADS * C_HEAD_DIM
    return pl.pallas_call(
        functools.partial(_nb_proj_kernel, sm_scale=float(C_HEAD_DIM ** -0.5 * LOG2E)),
        grid=(b, t // tm),
        in_specs=[
            pl.BlockSpec((1, tm, d), row),
            pl.BlockSpec((1, N_MOD, d), lambda bi, ti: (mod_row(bi), 0, 0)),
            _const_spec((1, d)),
            _const_spec(w["w_qkv"].shape),
        ],
        out_specs=[
            pl.BlockSpec((1, tm, hd), row),
            pl.BlockSpec((1, tm, hd), row),
            pl.BlockSpec((1, hd, tm), lambda bi, ti: (bi, 0, ti)),
        ],
        out_shape=[
            jax.ShapeDtypeStruct((b, t, hd), BF16),
            jax.ShapeDtypeStruct((b, t, hd), BF16),
            jax.ShapeDtypeStruct((b, hd, t), BF16),
        ],
        compiler_params=_params("parallel", "parallel"),
        name="nb_proj",
    )(x, mod, g.reshape(1, d), w["w_qkv"])


def _attn_kernel(*refs, key_rows, per_head, n_kv, hps, dq, q_tile, n_tiles):
    n_groups, k_pieces = len(key_rows), len(per_head)
    it = iter(refs)
    q_ref = next(it)
    k_refs = [[next(it) for _ in range(k_pieces)] for _ in range(n_groups)]
    v_refs = [next(it) for _ in range(n_groups)]
    o_ref = next(it)
    k_scr, v_scr, s_a, s_b, m_a, m_b = (next(it) for _ in range(6))

    r0 = 0
    for gi, rows in enumerate(key_rows):
        for j in range(n_kv):
            for pi in range(k_pieces):
                lanes = slice(j * LANES, (j + 1) * LANES) if per_head[pi] else slice(0, LANES)
                k_scr[j, r0:r0 + rows, pi * LANES:(pi + 1) * LANES] = k_refs[gi][pi][0, :, lanes]
            v_scr[j, :, r0:r0 + rows] = v_refs[gi][0, j * LANES:(j + 1) * LANES, :]
        r0 += rows

    n_units = hps * n_tiles

    def unit(u):
        if hps == 1:
            head, tile = 0, u
        else:
            head, tile = u // n_tiles, u % n_tiles
        rows = pl.ds(pl.multiple_of(tile * q_tile, q_tile), q_tile)
        return head, rows, (head if n_kv > 1 else 0)

    def scores(u, s_buf, m_buf):
        head, rows, kv = unit(u)
        q = q_ref[0, rows, pl.ds(pl.multiple_of(head * dq, dq), dq)]
        s_t = _dot_nt(k_scr[kv], q)
        s_buf[...] = s_t
        m_buf[...] = jnp.max(s_t, axis=0, keepdims=True)

    def softmax_pv(u, s_buf, m_buf):
        head, rows, kv = unit(u)
        p = jnp.exp2(s_buf[...] - m_buf[...])
        l = jnp.sum(p, axis=0, keepdims=True)
        o_t = _dot(v_scr[kv], p.astype(BF16)) / l
        o_ref[0, rows, pl.ds(pl.multiple_of(head * LANES, LANES), LANES)] = o_t.T.astype(BF16)

    scores(0, s_a, m_a)
    if n_units == 1:
        softmax_pv(0, s_a, m_a)
        return

    s_rows = s_a.shape[0]

    def phase(u_next, s_next, m_next, u_cur, s_cur, m_cur_buf):
        head_n, rows_n, kv_n = unit(u_next)
        head_c, rows_c, kv_c = unit(u_cur)
        q = q_ref[0, rows_n, pl.ds(pl.multiple_of(head_n * dq, dq), dq)]
        m_cur = m_cur_buf[...]
        m8 = l8 = acc = None
        for c0 in range(0, s_rows, KEY_CHUNK):
            rows = slice(c0, min(c0 + KEY_CHUNK, s_rows))
            s_c = _dot_nt(k_scr[kv_n, rows, :], q)
            s_next[rows, :] = s_c
            mc = jnp.max(s_c.reshape(-1, 8, q_tile), axis=0)
            m8 = mc if m8 is None else jnp.maximum(m8, mc)
            p = jnp.exp2(s_cur[rows, :] - m_cur)
            lc = jnp.sum(p.reshape(-1, 8, q_tile), axis=0)
            l8 = lc if l8 is None else l8 + lc
            a = _dot(v_scr[kv_c, :, rows], p.astype(BF16))
            acc = a if acc is None else acc + a
        m_next[...] = jnp.max(m8, axis=0, keepdims=True)
        l = jnp.sum(l8, axis=0, keepdims=True)
        o_ref[0, rows_c, pl.ds(pl.multiple_of(head_c * LANES, LANES), LANES)] = (acc / l).T.astype(BF16)

    def pair(i, carry):
        u = 2 * i
        phase(u + 1, s_b, m_b, u, s_a, m_a)
        phase(jnp.minimum(u + 2, n_units - 1), s_a, m_a, u + 1, s_b, m_b)
        return carry

    lax.fori_loop(0, n_units // 2, pair, 0)


def _attention(q, k_groups, v_groups, *, heads, hps, per_head, share_kv):
    b, tq, qw = q.shape
    dq = qw // heads
    q_tile = min(Q_TILE, tq)
    n_tiles = tq // q_tile
    n_kv = 1 if share_kv else hps
    assert heads % hps == 0 and (hps * n_tiles == 1 or (hps * n_tiles) % 2 == 0)
    key_rows = tuple(g[0].shape[1] for g in k_groups)
    k_pieces = len(per_head)
    s_total = sum(key_rows)
    in_specs = [pl.BlockSpec((1, tq, hps * dq), lambda bi, hg: (bi, 0, hg))]
    args = [q]
    for grp in k_groups:
        for pi, arr in enumerate(grp):
            width = n_kv * LANES if per_head[pi] else LANES
            moves = per_head[pi] or share_kv
            in_specs.append(pl.BlockSpec((1, arr.shape[1], width),
                                         lambda bi, hg, moves=moves: (bi, 0, hg if moves else 0)))
            args.append(arr)
    for arr in v_groups:
        in_specs.append(pl.BlockSpec((1, n_kv * LANES, arr.shape[2]), lambda bi, hg: (bi, hg, 0)))
        args.append(arr)
    return pl.pallas_call(
        functools.partial(_attn_kernel, key_rows=key_rows, per_head=tuple(per_head), n_kv=n_kv, hps=hps,
                          dq=dq, q_tile=q_tile, n_tiles=n_tiles),
        grid=(b, heads // hps),
        in_specs=in_specs,
        out_specs=pl.BlockSpec((1, tq, hps * LANES), lambda bi, hg: (bi, 0, hg)),
        out_shape=jax.ShapeDtypeStruct((b, tq, heads * LANES), BF16),
        scratch_shapes=[
            pltpu.VMEM((n_kv, s_total, k_pieces * LANES), BF16), pltpu.VMEM((n_kv, LANES, s_total), BF16),
            pltpu.VMEM((s_total, q_tile), F32), pltpu.VMEM((s_total, q_tile), F32),
            pltpu.VMEM((1, q_tile), F32), pltpu.VMEM((1, q_tile), F32),
        ],
        compiler_params=_params("parallel", "parallel"),
        name="attention",
    )(*args)


def _nb_attn_kernel(*refs, has_window, rows, tile, n_groups):
    it = iter(refs)
    q_ref, kc_ref, vc_ref = next(it), next(it), next(it)
    if has_window:
        kl_ref, vl_ref, bias_ref = next(it), next(it), next(it)
    o_ref = next(it)
    s_a, s_b, m_a, m_b = (next(it) for _ in range(4))
    c = kc_ref.shape[1]
    win = NB_KEY_ROWS * GRID_W
    lane = lax.broadcasted_iota(jnp.int32, (1, LANES), 1)

    def tile_rows(g):
        return pl.ds(pl.multiple_of(g * tile, tile), tile)

    def window(g):
        first_row = jnp.clip(NB_ROWS * g - C_WIN_ROWS // 2, 0, rows - NB_KEY_ROWS)
        return pl.ds(pl.multiple_of(first_row * GRID_W, 2 * LANES), win)

    def scores(g, s_buf, m_buf):
        q = q_ref[0, tile_rows(g), :]
        zero = jnp.zeros_like(q)
        q2 = jnp.concatenate([jnp.where(lane < C_HEAD_DIM, q, zero), jnp.where(lane >= C_HEAD_DIM, q, zero)],
                             axis=0)
        s_c = _dot_nt(kc_ref[0], q2)
        s_buf[0:c, :] = s_c
        m = jnp.max(s_c, axis=0, keepdims=True)
        if has_window:
            variant = jnp.where(g == 0, 0, jnp.where(g == n_groups - 1, 2, 1))
            s_w = _dot_nt(kl_ref[0, window(g), :], q2)
            for e in range(2):
                s_e = s_w[:, e * tile:(e + 1) * tile] + bias_ref[variant, e]
                s_buf[c:, e * tile:(e + 1) * tile] = s_e
                m_e = jnp.maximum(m[:, e * tile:(e + 1) * tile], jnp.max(s_e, axis=0, keepdims=True))
                m_buf[:, e * tile:(e + 1) * tile] = m_e
        else:
            m_buf[...] = m

    def softmax_pv(g, s_buf, m_buf):
        p = jnp.exp2(s_buf[...] - m_buf[...])
        l = jnp.sum(p, axis=0, keepdims=True)
        pb = p.astype(BF16)
        o2 = _dot(vc_ref[0], pb[0:c])
        if has_window:
            o2 = o2 + _dot(vl_ref[0, :, window(g)], pb[c:])
        o_t = jnp.concatenate([o2[e * C_HEAD_DIM:(e + 1) * C_HEAD_DIM, e * tile:(e + 1) * tile]
                               / l[:, e * tile:(e + 1) * tile] for e in range(2)], axis=0)
        o_ref[0, tile_rows(g), :] = o_t.T.astype(BF16)

    scores(0, s_a, m_a)
    if n_groups == 1:
        softmax_pv(0, s_a, m_a)
        return

    def phase(g_next, s_next, m_next, g_cur, s_cur, m_cur_buf):
        q = q_ref[0, tile_rows(g_next), :]
        zero = jnp.zeros_like(q)
        q2 = jnp.concatenate([jnp.where(lane < C_HEAD_DIM, q, zero), jnp.where(lane >= C_HEAD_DIM, q, zero)],
                             axis=0)
        m_cur = m_cur_buf[...]
        variant = jnp.where(g_next == 0, 0, jnp.where(g_next == n_groups - 1, 2, 1))
        w_next, w_cur = window(g_next), window(g_cur)
        s_c = _dot_nt(kc_ref[0], q2)
        s_next[0:c, :] = s_c
        m8 = jnp.max(s_c.reshape(-1, 8, 2 * tile), axis=0)
        p = jnp.exp2(s_cur[0:c, :] - m_cur)
        l8 = jnp.sum(p.reshape(-1, 8, 2 * tile), axis=0)
        o2 = _dot(vc_ref[0], p.astype(BF16))
        for c0 in range(0, win, NB_CHUNK):
            s_w = _dot_nt(kl_ref[0, pl.ds(w_next.start + c0, NB_CHUNK), :], q2)
            s_w = jnp.concatenate([s_w[:, e * tile:(e + 1) * tile] + bias_ref[variant, e, c0:c0 + NB_CHUNK, :]
                                   for e in range(2)], axis=1)
            s_next[c + c0:c + c0 + NB_CHUNK, :] = s_w
            m8 = jnp.maximum(m8, jnp.max(s_w.reshape(-1, 8, 2 * tile), axis=0))
            p = jnp.exp2(s_cur[c + c0:c + c0 + NB_CHUNK, :] - m_cur)
            l8 = l8 + jnp.sum(p.reshape(-1, 8, 2 * tile), axis=0)
            o2 = o2 + _dot(vl_ref[0, :, pl.ds(w_cur.start + c0, NB_CHUNK)], p.astype(BF16))
        m_next[...] = jnp.max(m8, axis=0, keepdims=True)
        l = jnp.sum(l8, axis=0, keepdims=True)
        o_t = jnp.concatenate([o2[e * C_HEAD_DIM:(e + 1) * C_HEAD_DIM, e * tile:(e + 1) * tile]
                               / l[:, e * tile:(e + 1) * tile] for e in range(2)], axis=0)
        o_ref[0, tile_rows(g_cur), :] = o_t.T.astype(BF16)

    def pair(i, carry):
        g = 2 * i
        phase(g + 1, s_b, m_b, g, s_a, m_a)
        phase(jnp.minimum(g + 2, n_groups - 1), s_a, m_a, g + 1, s_b, m_b)
        return carry

    lax.fori_loop(0, n_groups // 2, pair, 0)


def _nb_attention(q, k_ctx, vt_ctx, k_lat=None, vt_lat=None, bias=None):
    b, tq, hw = q.shape
    pairs = hw // LANES
    has_window = k_lat is not None
    c = k_ctx.shape[1]
    if has_window:
        t = k_lat.shape[1]
        rows = t // GRID_W
        tile = NB_ROWS * GRID_W
        keys = c + NB_KEY_ROWS * GRID_W
    else:
        rows, tile, keys = 0, tq, c
    n_groups = tq // tile
    assert n_groups == 1 or n_groups % 2 == 0
    in_specs = [
        pl.BlockSpec((1, tq, LANES), lambda bi, p: (bi, 0, p)),
        pl.BlockSpec((1, c, LANES), lambda bi, p: (bi, 0, p)),
        pl.BlockSpec((1, LANES, c), lambda bi, p: (bi, p, 0)),
    ]
    args = [q, k_ctx, vt_ctx]
    if has_window:
        in_specs += [
            pl.BlockSpec((1, t, LANES), lambda bi, p: (bi, 0, p)),
            pl.BlockSpec((1, LANES, t), lambda bi, p: (bi, p, 0)),
            pl.BlockSpec((3, 2, NB_KEY_ROWS * GRID_W, tile), lambda bi, p: (0, p, 0, 0)),
        ]
        args += [k_lat, vt_lat, bias]
    return pl.pallas_call(
        functools.partial(_nb_attn_kernel, has_window=has_window, rows=rows, tile=tile, n_groups=n_groups),
        grid=(b, pairs),
        in_specs=in_specs,
        out_specs=pl.BlockSpec((1, tq, LANES), lambda bi, p: (bi, 0, p)),
        out_shape=jax.ShapeDtypeStruct((b, tq, hw), BF16),
        scratch_shapes=[pltpu.VMEM((keys, 2 * tile), F32), pltpu.VMEM((keys, 2 * tile), F32),
                        pltpu.VMEM((1, 2 * tile), F32), pltpu.VMEM((1, 2 * tile), F32)],
        compiler_params=_params("parallel", "parallel"),
        name="nb_attention",
    )(*args)


def _nb_bias_table(rpb, rows):
    heads = rpb.shape[0]
    tile_q = NB_ROWS * GRID_W
    return pl.pallas_call(
        functools.partial(_nb_bias_kernel, rows=rows),
        grid=(heads,),
        in_specs=[pl.BlockSpec(memory_space=pltpu.SMEM)],
        out_specs=pl.BlockSpec((3, 1, NB_KEY_ROWS * GRID_W, tile_q), lambda h: (0, h, 0, 0)),
        out_shape=jax.ShapeDtypeStruct((3, heads, NB_KEY_ROWS * GRID_W, tile_q), F32),
        compiler_params=_params("parallel"),
        name="nb_bias",
    )(rpb.reshape(-1))


def _nb_bias_kernel(rpb_ref, o_ref, *, rows):
    n_a, n_b = 2 * C_WIN_ROWS - 1, 2 * C_WIN_COLS - 1
    tile_q = NB_ROWS * GRID_W
    shape = (GRID_W, tile_q)
    kc = lax.broadcasted_iota(jnp.int32, shape, 0)
    lane = lax.broadcasted_iota(jnp.int32, shape, 1)
    qc = lane % GRID_W
    qi = lane // GRID_W
    c0 = jnp.clip(qc - C_WIN_COLS // 2, 0, GRID_W - C_WIN_COLS)
    col_ok = (kc >= c0) & (kc < c0 + C_WIN_COLS)
    dcol = kc - qc + (C_WIN_COLS - 1)
    base = pl.program_id(0) * (n_a * n_b)
    masked = jnp.full(shape, MASK_VALUE, F32)
    planes = []
    for a in range(n_a):
        acc = masked
        for bb in range(n_b):
            acc = jnp.where(dcol == bb, rpb_ref[base + a * n_b + bb], acc)
        planes.append(jnp.where(col_ok, acc * LOG2E, MASK_VALUE))
    groups = rows // NB_ROWS
    for v, g in enumerate((0, 1, groups - 1)):
        first_key_row = min(max(NB_ROWS * g - C_WIN_ROWS // 2, 0), rows - NB_KEY_ROWS)
        for j in range(NB_KEY_ROWS):
            kr = first_key_row + j
            blk = masked
            for i in range(NB_ROWS):
                qr = NB_ROWS * g + i
                r0 = min(max(qr - C_WIN_ROWS // 2, 0), rows - C_WIN_ROWS)
                if r0 <= kr < r0 + C_WIN_ROWS:
                    blk = jnp.where(qi == i, planes[kr - qr + C_WIN_ROWS - 1], blk)
            o_ref[v, 0, j * GRID_W:(j + 1) * GRID_W, :] = blk


def _rope_tables(rows, rot_dim):
    n = rot_dim // 4
    inv_freq = ROPE_THETA ** (-jnp.arange(n, dtype=F32) / n)
    t = jnp.arange(rows * GRID_W, dtype=jnp.int32)
    r = (t // GRID_W).astype(F32)
    col = (t % GRID_W).astype(F32)
    ang = jnp.concatenate([r[:, None] * inv_freq[None, :], col[:, None] * inv_freq[None, :]], axis=-1)
    cos, sin = jnp.cos(ang), jnp.sin(ang)
    reps = LANES // rot_dim
    return (jnp.tile(jnp.concatenate([cos, cos], axis=-1), (1, reps)),
            jnp.tile(jnp.concatenate([-sin, sin], axis=-1), (1, reps)))


def _mla_weights(w_in, q_norm, kv_norm, w_uq, w_ukv, w_o):
    rank = A_Q_RANK + A_KV_RANK
    k_r = w_in[:, rank:]
    uq = w_uq.reshape(A_Q_RANK, A_HEADS, A_NOPE + A_ROPE)
    ukv = w_ukv.reshape(A_KV_RANK, A_HEADS, A_NOPE + A_V)
    return {
        "w_in": jnp.concatenate([w_in[:, :rank], k_r, k_r], axis=1).astype(BF16),
        "q_norm": q_norm, "kv_norm": kv_norm,
        "w_uq": jnp.concatenate([uq[:, :, :A_NOPE].reshape(A_Q_RANK, -1),
                                 uq[:, :, A_NOPE:].reshape(A_Q_RANK, -1)], axis=1).astype(BF16),
        "w_ukv": jnp.concatenate([ukv[:, :, :A_NOPE].reshape(A_KV_RANK, -1),
                                  ukv[:, :, A_NOPE:].reshape(A_KV_RANK, -1)], axis=1).astype(BF16),
        "w_o": w_o.astype(BF16),
    }


def kernel(x, c, ctx, c_ctx, norm_g, w_mod, b_mod, ffn1_w13, ffn1_w2, ffn2_w13, ffn2_w2, a_w_in, a_q_norm, a_kv_norm, a_w_uq, a_w_ukv, a_w_o, b_w_qkv, b_q_norm, b_k_norm, b_w_o, c_w_qkv, c_rpb, c_w_o, final_norm_g):
    b, t, d = x.shape
    depth = w_mod.shape[0]
    rows = t // GRID_W
    assert b < MOD_ROWS and t % (NB_ROWS * GRID_W) == 0 and rows >= NB_KEY_ROWS + NB_ROWS

    c_rows = jnp.zeros((MOD_ROWS, d), F32).at[:b].set(c).at[b].set(c_ctx)
    mod_all = _modulation(c_rows, w_mod, b_mod).reshape(depth, MOD_ROWS, N_MOD, d)
    lat_row = lambda bi: bi
    ctx_row = lambda bi: b

    rope_a = _rope_tables(rows, A_ROPE)
    rope_b = _rope_tables(rows, B_HEAD_DIM)

    xc = ctx
    for i in range(depth):
        ctx_out = i < depth - 1
        last = i == depth - 1
        mod = mod_all[i]
        kind, j = i % N_MIXERS, i // N_MIXERS
        w13_1, w2_1 = ffn1_w13[i].astype(BF16), ffn1_w2[i].astype(BF16)
        w13_2, w2_2 = ffn2_w13[i].astype(BF16), ffn2_w2[i].astype(BF16)

        x = _ffn(x, mod, lat_row, norm_g[i, 0], w13_1, w2_1, k0=0)
        xc = _ffn(xc, mod, ctx_row, norm_g[i, 0], w13_1, w2_1, k0=0)

        if kind == 0:
            w = _mla_weights(a_w_in[j], a_q_norm[j], a_kv_norm[j], a_w_uq[j], a_w_ukv[j], a_w_o[j])
            q, kn, kr, vt = _mla_proj(x, mod, lat_row, norm_g[i, 1], w, rope_a)
            qc, knc, krc, vtc = _mla_proj(xc, mod, ctx_row, norm_g[i, 1], w, None)
            cfg = dict(heads=A_HEADS, hps=A_HEADS_PER_STEP, per_head=(True, False), share_kv=False)
            o = _attention(q, [[knc, krc], [kn, kr]], [vtc, vt], **cfg)
            if ctx_out:
                oc = _attention(qc, [[knc, krc]], [vtc], **dict(cfg, hps=A_HEADS))
            w_o = w["w_o"]
        elif kind == 1:
            w = {"w_qkv": b_w_qkv[j].astype(BF16), "q_norm": b_q_norm[j], "k_norm": b_k_norm[j]}
            q, k, vt = _gqa_proj(x, mod, lat_row, norm_g[i, 1], w, rope_b)
            qc, kc, vtc = _gqa_proj(xc, mod, ctx_row, norm_g[i, 1], w, None)
            cfg = dict(heads=B_HEADS, hps=B_HEADS // B_KV_HEADS, per_head=(False,), share_kv=True)
            o = _attention(q, [[kc], [k]], [vtc, vt], **cfg)
            if ctx_out:
                oc = _attention(qc, [[kc]], [vtc], **cfg)
            w_o = b_w_o[j].astype(BF16)
        else:
            w = {"w_qkv": c_w_qkv[j].astype(BF16)}
            q, k, vt = _nb_proj(x, mod, lat_row, norm_g[i, 1], w)
            qc, kc, vtc = _nb_proj(xc, mod, ctx_row, norm_g[i, 1], w)
            o = _nb_attention(q, kc, vtc, k, vt, _nb_bias_table(c_rpb[j], rows))
            if ctx_out:
                oc = _nb_attention(qc, kc, vtc)
            w_o = c_w_o[j].astype(BF16)

        x = _ffn(x, mod, lat_row, norm_g[i, 2], w13_2, w2_2, k0=6, attn=o, w_o=w_o,
                 final_g=final_norm_g if last else None)
        if ctx_out:
            xc = _ffn(xc, mod, ctx_row, norm_g[i, 2], w13_2, w2_2, k0=6, attn=oc, w_o=w_o)
    return x
```

```python
import functools

import jax
import jax.numpy as jnp
from jax import lax
from jax.experimental import pallas as pl
from jax.experimental.pallas import tpu as pltpu

F32 = jnp.float32
BF16 = jnp.bfloat16

GRID_W = 64
N_MIXERS = 3
N_MOD = 9
EPS = 1e-6
ROPE_THETA = 10000.0

A_HEADS, A_NOPE, A_ROPE, A_V = 8, 128, 64, 128
A_Q_RANK, A_KV_RANK = 384, 256
B_HEADS, B_KV_HEADS, B_HEAD_DIM = 8, 2, 128
C_HEADS, C_HEAD_DIM, C_WIN_ROWS, C_WIN_COLS = 16, 64, 8, 16

LANES = 128
MOD_ROWS = 16
MASK_VALUE = -1e30
LOG2E = 1.4426950408889634
VMEM_LIMIT = 56 * 1024 * 1024

ROW_TILE = 512
GQA_ROW_TILE = 256
A_HEADS_PER_STEP = 2
Q_TILE = 512
GQA_KEY_CHUNK = 512
MLA_KEY_CHUNK = 0
NB_CHUNK = 256
NB_ROWS = 4
NB_KEY_ROWS = 12


def _params(*sem):
    return pltpu.CompilerParams(dimension_semantics=sem, vmem_limit_bytes=VMEM_LIMIT)


def _const_spec(shape):
    nd = len(shape)
    return pl.BlockSpec(shape, lambda *_: (0,) * nd, pipeline_mode=pl.Buffered(1))


def _dot(a, b):
    return jnp.dot(a, b, preferred_element_type=F32)


def _dot_nt(a, b):
    return lax.dot_general(a, b, (((1,), (1,)), ((), ())), preferred_element_type=F32)


def _rms(x):
    return x * lax.rsqrt(jnp.mean(x * x, axis=-1, keepdims=True) + EPS)


def _modulated_norm(x, g, scale, shift):
    return _rms(x) * (g * (1.0 + scale)) + shift


def _silu(x):
    return x / (1.0 + jnp.exp(-x))


def _mod_kernel(c_ref, w_ref, b_ref, o_ref):
    sc = _silu(c_ref[...]).astype(BF16)
    o_ref[0] = _dot(sc, w_ref[0].astype(BF16)) + b_ref[0]


def _modulation(c_rows, w_mod, b_mod):
    depth, d, n = w_mod.shape
    tn = d
    return pl.pallas_call(
        _mod_kernel,
        grid=(depth, n // tn),
        in_specs=[
            pl.BlockSpec((MOD_ROWS, d), lambda i, j: (0, 0)),
            pl.BlockSpec((1, d, tn), lambda i, j: (i, 0, j)),
            pl.BlockSpec((1, 1, tn), lambda i, j: (i, 0, j)),
        ],
        out_specs=pl.BlockSpec((1, MOD_ROWS, tn), lambda i, j: (i, 0, j)),
        out_shape=jax.ShapeDtypeStruct((depth, MOD_ROWS, n), F32),
        compiler_params=_params("parallel", "parallel"),
        name="modulation",
    )(c_rows, w_mod, b_mod.reshape(depth, 1, n))


def _ffn_kernel(*refs, k0, d_ff, has_oproj, final_norm):
    it = iter(refs)
    x_ref, mod_ref, g_ref, w13_ref, w2_ref = (next(it) for _ in range(5))
    if has_oproj:
        a_ref, wo_ref = next(it), next(it)
    if final_norm:
        fg_ref = next(it)
    out_ref = next(it)

    x = x_ref[0]
    mod = mod_ref[0]
    if has_oproj:
        x = x + mod[5:6] * _dot(a_ref[0], wo_ref[...])
    h = _modulated_norm(x, g_ref[...], mod[k0 + 1:k0 + 2], mod[k0:k0 + 1]).astype(BF16)
    hgu = _dot(h, w13_ref[...])
    act = (_silu(hgu[:, :d_ff]) * hgu[:, d_ff:]).astype(BF16)
    y = x + (0.5 * mod[k0 + 2:k0 + 3]) * _dot(act, w2_ref[...])
    if final_norm:
        y = _rms(y) * fg_ref[...]
    out_ref[0] = y


def _ffn(x, mod, mod_row, g, w13, w2, *, k0, attn=None, w_o=None, final_g=None):
    b, t, d = x.shape
    d_ff = w2.shape[0]
    tm = min(ROW_TILE, t)
    row = lambda bi, ti: (bi, ti, 0)
    in_specs = [
        pl.BlockSpec((1, tm, d), row),
        pl.BlockSpec((1, N_MOD, d), lambda bi, ti: (mod_row(bi), 0, 0)),
        _const_spec((1, d)),
        _const_spec(w13.shape),
        _const_spec(w2.shape),
    ]
    args = [x, mod, g.reshape(1, d), w13, w2]
    if attn is not None:
        in_specs += [pl.BlockSpec((1, tm, attn.shape[2]), row), _const_spec(w_o.shape)]
        args += [attn, w_o]
    if final_g is not None:
        in_specs.append(_const_spec((1, d)))
        args.append(final_g.reshape(1, d))
    return pl.pallas_call(
        functools.partial(_ffn_kernel, k0=k0, d_ff=d_ff, has_oproj=attn is not None,
                          final_norm=final_g is not None),
        grid=(b, t // tm),
        in_specs=in_specs,
        out_specs=pl.BlockSpec((1, tm, d), row),
        out_shape=jax.ShapeDtypeStruct((b, t, d), F32),
        compiler_params=_params("parallel", "parallel"),
        name="half_ffn",
    )(*args)


def _rope_pairs(x, cos, sin_signed, half):
    if 2 * half == LANES:
        rot = pltpu.roll(x, half, 1)
    else:
        lane = lax.broadcasted_iota(jnp.int32, x.shape, 1)
        first = (lane % (2 * half)) < half
        rot = jnp.where(first, pltpu.roll(x, LANES - half, 1), pltpu.roll(x, half, 1))
    return x * cos + rot * sin_signed


def _mla_proj_kernel(*refs, use_rope, sm_scale):
    it = iter(refs)
    x_ref, mod_ref, g_ref, w_in_ref, qg_ref, kvg_ref, w_uq_ref, w_ukv_ref = (next(it) for _ in range(8))
    if use_rope:
        cos_ref, sin_ref = next(it), next(it)
    q_ref, kn_ref, kr_ref, vt_ref = (next(it) for _ in range(4))

    mod = mod_ref[0]
    h = _modulated_norm(x_ref[0], g_ref[...], mod[4:5], mod[3:4]).astype(BF16)
    proj = _dot(h, w_in_ref[...])
    c_q = (_rms(proj[:, :A_Q_RANK]) * qg_ref[...]).astype(BF16)
    c_kv = (_rms(proj[:, A_Q_RANK:A_Q_RANK + A_KV_RANK]) * kvg_ref[...]).astype(BF16)
    k_r = proj[:, A_Q_RANK + A_KV_RANK:]
    q_all = _dot(c_q, w_uq_ref[...])
    kv = _dot(c_kv, w_ukv_ref[...])
    n_nope = A_HEADS * A_NOPE
    if use_rope:
        cos, sin = cos_ref[...], sin_ref[...]
        k_r = _rope_pairs(k_r, cos, sin, A_ROPE // 2)
    lane = lax.broadcasted_iota(jnp.int32, (1, LANES), 1)
    for j in range(A_HEADS // 2):
        qr = q_all[:, n_nope + j * LANES:n_nope + (j + 1) * LANES]
        if use_rope:
            qr = _rope_pairs(qr, cos, sin, A_ROPE // 2)
        qr = qr * sm_scale
        for e in range(2):
            hd = 2 * j + e
            keep = (lane < A_ROPE) if e == 0 else (lane >= A_ROPE)
            q_ref[0, :, 2 * hd * LANES:(2 * hd + 1) * LANES] = (
                q_all[:, hd * A_NOPE:(hd + 1) * A_NOPE] * sm_scale).astype(BF16)
            q_ref[0, :, (2 * hd + 1) * LANES:(2 * hd + 2) * LANES] = jnp.where(keep, qr, 0.0).astype(BF16)
    kn_ref[0] = kv[:, :n_nope].astype(BF16)
    kr_ref[0] = k_r.astype(BF16)
    vt_ref[0] = kv[:, n_nope:].T.astype(BF16)


def _mla_proj(x, mod, mod_row, g, w, rope):
    b, t, d = x.shape
    tm = min(ROW_TILE, t)
    row = lambda bi, ti: (bi, ti, 0)
    in_specs = [
        pl.BlockSpec((1, tm, d), row),
        pl.BlockSpec((1, N_MOD, d), lambda bi, ti: (mod_row(bi), 0, 0)),
        _const_spec((1, d)),
        _const_spec(w["w_in"].shape), _const_spec((1, A_Q_RANK)), _const_spec((1, A_KV_RANK)),
        _const_spec(w["w_uq"].shape), _const_spec(w["w_ukv"].shape),
    ]
    args = [x, mod, g.reshape(1, d), w["w_in"], w["q_norm"].reshape(1, -1), w["kv_norm"].reshape(1, -1),
            w["w_uq"], w["w_ukv"]]
    if rope is not None:
        in_specs += [pl.BlockSpec((tm, LANES), lambda bi, ti: (ti, 0))] * 2
        args += list(rope)
    hv = A_HEADS * A_V
    return pl.pallas_call(
        functools.partial(_mla_proj_kernel, use_rope=rope is not None,
                          sm_scale=float((A_NOPE + A_ROPE) ** -0.5 * LOG2E)),
        grid=(b, t // tm),
        in_specs=in_specs,
        out_specs=[
            pl.BlockSpec((1, tm, 2 * LANES * A_HEADS), row),
            pl.BlockSpec((1, tm, A_HEADS * A_NOPE), row),
            pl.BlockSpec((1, tm, LANES), row),
            pl.BlockSpec((1, hv, tm), lambda bi, ti: (bi, 0, ti)),
        ],
        out_shape=[
            jax.ShapeDtypeStruct((b, t, 2 * LANES * A_HEADS), BF16),
            jax.ShapeDtypeStruct((b, t, A_HEADS * A_NOPE), BF16),
            jax.ShapeDtypeStruct((b, t, LANES), BF16),
            jax.ShapeDtypeStruct((b, hv, t), BF16),
        ],
        compiler_params=_params("parallel", "parallel"),
        name="mla_proj",
    )(*args)


def _gqa_proj_kernel(*refs, use_rope, sm_scale):
    it = iter(refs)
    x_ref, mod_ref, g_ref, w_ref, qg_ref, kg_ref = (next(it) for _ in range(6))
    if use_rope:
        cos_ref, sin_ref = next(it), next(it)
    q_ref, k_ref, vt_ref = (next(it) for _ in range(3))

    mod = mod_ref[0]
    h = _modulated_norm(x_ref[0], g_ref[...], mod[4:5], mod[3:4]).astype(BF16)
    proj = _dot(h, w_ref[...])
    qw = B_HEADS * B_HEAD_DIM
    kw = B_KV_HEADS * B_HEAD_DIM
    if use_rope:
        cos, sin = cos_ref[...], sin_ref[...]

    def head(col, gain, scale):
        y = _rms(proj[:, col:col + B_HEAD_DIM]) * gain
        if use_rope:
            y = _rope_pairs(y, cos, sin, B_HEAD_DIM // 2)
        return (y * scale).astype(BF16) if scale != 1.0 else y.astype(BF16)

    for hd in range(B_HEADS):
        q_ref[0, :, hd * B_HEAD_DIM:(hd + 1) * B_HEAD_DIM] = head(hd * B_HEAD_DIM, qg_ref[...], sm_scale)
    for hd in range(B_KV_HEADS):
        k_ref[0, :, hd * B_HEAD_DIM:(hd + 1) * B_HEAD_DIM] = head(qw + hd * B_HEAD_DIM, kg_ref[...], 1.0)
    vt_ref[0] = proj[:, qw + kw:].T.astype(BF16)


def _gqa_proj(x, mod, mod_row, g, w, rope):
    b, t, d = x.shape
    tm = min(GQA_ROW_TILE, t)
    row = lambda bi, ti: (bi, ti, 0)
    qw, kw = B_HEADS * B_HEAD_DIM, B_KV_HEADS * B_HEAD_DIM
    in_specs = [
        pl.BlockSpec((1, tm, d), row),
        pl.BlockSpec((1, N_MOD, d), lambda bi, ti: (mod_row(bi), 0, 0)),
        _const_spec((1, d)),
        _const_spec(w["w_qkv"].shape), _const_spec((1, B_HEAD_DIM)), _const_spec((1, B_HEAD_DIM)),
    ]
    args = [x, mod, g.reshape(1, d), w["w_qkv"], w["q_norm"].reshape(1, -1), w["k_norm"].reshape(1, -1)]
    if rope is not None:
        in_specs += [pl.BlockSpec((tm, LANES), lambda bi, ti: (ti, 0))] * 2
        args += list(rope)
    return pl.pallas_call(
        functools.partial(_gqa_proj_kernel, use_rope=rope is not None, sm_scale=float(B_HEAD_DIM ** -0.5 * LOG2E)),
        grid=(b, t // tm),
        in_specs=in_specs,
        out_specs=[
            pl.BlockSpec((1, tm, qw), row),
            pl.BlockSpec((1, tm, kw), row),
            pl.BlockSpec((1, kw, tm), lambda bi, ti: (bi, 0, ti)),
        ],
        out_shape=[
            jax.ShapeDtypeStruct((b, t, qw), BF16),
            jax.ShapeDtypeStruct((b, t, kw), BF16),
            jax.ShapeDtypeStruct((b, kw, t), BF16),
        ],
        compiler_params=_params("parallel", "parallel"),
        name="gqa_proj",
    )(*args)


def _nb_proj_kernel(x_ref, mod_ref, g_ref, w_ref, q_ref, k_ref, vt_ref, *, sm_scale):
    mod = mod_ref[0]
    h = _modulated_norm(x_ref[0], g_ref[...], mod[4:5], mod[3:4]).astype(BF16)
    proj = _dot(h, w_ref[...])
    hd = C_HEADS * C_HEAD_DIM
    q_ref[0] = (proj[:, :hd] * sm_scale).astype(BF16)
    k_ref[0] = proj[:, hd:2 * hd].astype(BF16)
    vt_ref[0] = proj[:, 2 * hd:].T.astype(BF16)


def _nb_proj(x, mod, mod_row, g, w):
    b, t, d = x.shape
    tm = min(ROW_TILE, t)
    row = lambda bi, ti: (bi, ti, 0)
    hd = C_HEADS * C_HEAD_DIM
    return pl.pallas_call(
        functools.partial(_nb_proj_kernel, sm_scale=float(C_HEAD_DIM ** -0.5 * LOG2E)),
        grid=(b, t // tm),
        in_specs=[
            pl.BlockSpec((1, tm, d), row),
            pl.BlockSpec((1, N_MOD, d), lambda bi, ti: (mod_row(bi), 0, 0)),
            _const_spec((1, d)),
            _const_spec(w["w_qkv"].shape),
        ],
        out_specs=[
            pl.BlockSpec((1, tm, hd), row),
            pl.BlockSpec((1, tm, hd), row),
            pl.BlockSpec((1, hd, tm), lambda bi, ti: (bi, 0, ti)),
        ],
        out_shape=[
            jax.ShapeDtypeStruct((b, t, hd), BF16),
            jax.ShapeDtypeStruct((b, t, hd), BF16),
            jax.ShapeDtypeStruct((b, hd, t), BF16),
        ],
        compiler_params=_params("parallel", "parallel"),
        name="nb_proj",
    )(x, mod, g.reshape(1, d), w["w_qkv"])


def _attn_kernel(*refs, key_rows, per_head, n_kv, hps, dq, q_tile, n_tiles, key_chunk):
    n_groups, k_pieces = len(key_rows), len(per_head)
    it = iter(refs)
    q_ref = next(it)
    k_refs = [[next(it) for _ in range(k_pieces)] for _ in range(n_groups)]
    v_refs = [next(it) for _ in range(n_groups)]
    o_ref = next(it)
    k_scr, v_scr, s_a, s_b, m_a, m_b = (next(it) for _ in range(6))

    r0 = 0
    for gi, rows in enumerate(key_rows):
        for j in range(n_kv):
            for pi in range(k_pieces):
                lanes = slice(j * LANES, (j + 1) * LANES) if per_head[pi] else slice(0, LANES)
                k_scr[j, r0:r0 + rows, pi * LANES:(pi + 1) * LANES] = k_refs[gi][pi][0, :, lanes]
            v_scr[j, :, r0:r0 + rows] = v_refs[gi][0, j * LANES:(j + 1) * LANES, :]
        r0 += rows

    n_units = hps * n_tiles

    def unit(u):
        if hps == 1:
            head, tile = 0, u
        else:
            head, tile = u // n_tiles, u % n_tiles
        rows = pl.ds(pl.multiple_of(tile * q_tile, q_tile), q_tile)
        return head, rows, (head if n_kv > 1 else 0)

    def scores(u, s_buf, m_buf):
        head, rows, kv = unit(u)
        q = q_ref[0, rows, pl.ds(pl.multiple_of(head * dq, dq), dq)]
        s_t = _dot_nt(k_scr[kv], q)
        s_buf[...] = s_t
        m_buf[...] = jnp.max(s_t, axis=0, keepdims=True)

    def softmax_pv(u, s_buf, m_buf):
        head, rows, kv = unit(u)
        p = jnp.exp2(s_buf[...] - m_buf[...])
        l = jnp.sum(p, axis=0, keepdims=True)
        o_t = _dot(v_scr[kv], p.astype(BF16)) / l
        o_ref[0, rows, pl.ds(pl.multiple_of(head * LANES, LANES), LANES)] = o_t.T.astype(BF16)

    scores(0, s_a, m_a)
    if n_units == 1:
        softmax_pv(0, s_a, m_a)
        return

    s_rows = s_a.shape[0]

    def phase(u_next, s_next, m_next, u_cur, s_cur, m_cur_buf):
        if not key_chunk:
            scores(u_next, s_next, m_next)
            softmax_pv(u_cur, s_cur, m_cur_buf)
            return
        head_n, rows_n, kv_n = unit(u_next)
        head_c, rows_c, kv_c = unit(u_cur)
        q = q_ref[0, rows_n, pl.ds(pl.multiple_of(head_n * dq, dq), dq)]
        m_cur = m_cur_buf[...]
        m8 = l8 = acc = None
        for c0 in range(0, s_rows, key_chunk):
            rows = slice(c0, min(c0 + key_chunk, s_rows))
            s_c = _dot_nt(k_scr[kv_n, rows, :], q)
            s_next[rows, :] = s_c
            mc = jnp.max(s_c.reshape(-1, 8, q_tile), axis=0)
            m8 = mc if m8 is None else jnp.maximum(m8, mc)
            p = jnp.exp2(s_cur[rows, :] - m_cur)
            lc = jnp.sum(p.reshape(-1, 8, q_tile), axis=0)
            l8 = lc if l8 is None else l8 + lc
            a = _dot(v_scr[kv_c, :, rows], p.astype(BF16))
            acc = a if acc is None else acc + a
        m_next[...] = jnp.max(m8, axis=0, keepdims=True)
        l = jnp.sum(l8, axis=0, keepdims=True)
        o_ref[0, rows_c, pl.ds(pl.multiple_of(head_c * LANES, LANES), LANES)] = (acc / l).T.astype(BF16)

    def pair(i, carry):
        u = 2 * i
        phase(u + 1, s_b, m_b, u, s_a, m_a)
        phase(jnp.minimum(u + 2, n_units - 1), s_a, m_a, u + 1, s_b, m_b)
        return carry

    lax.fori_loop(0, n_units // 2, pair, 0)


def _attention(q, k_groups, v_groups, *, heads, hps, per_head, share_kv, key_chunk):
    b, tq, qw = q.shape
    dq = qw // heads
    q_tile = min(Q_TILE, tq)
    n_tiles = tq // q_tile
    n_kv = 1 if share_kv else hps
    assert heads % hps == 0 and (hps * n_tiles == 1 or (hps * n_tiles) % 2 == 0)
    key_rows = tuple(g[0].shape[1] for g in k_groups)
    k_pieces = len(per_head)
    s_total = sum(key_rows)
    in_specs = [pl.BlockSpec((1, tq, hps * dq), lambda bi, hg: (bi, 0, hg))]
    args = [q]
    for grp in k_groups:
        for pi, arr in enumerate(grp):
            width = n_kv * LANES if per_head[pi] else LANES
            moves = per_head[pi] or share_kv
            in_specs.append(pl.BlockSpec((1, arr.shape[1], width),
                                         lambda bi, hg, moves=moves: (bi, 0, hg if moves else 0)))
            args.append(arr)
    for arr in v_groups:
        in_specs.append(pl.BlockSpec((1, n_kv * LANES, arr.shape[2]), lambda bi, hg: (bi, hg, 0)))
        args.append(arr)
    return pl.pallas_call(
        functools.partial(_attn_kernel, key_rows=key_rows, per_head=tuple(per_head), n_kv=n_kv, hps=hps,
                          dq=dq, q_tile=q_tile, n_tiles=n_tiles, key_chunk=key_chunk),
        grid=(b, heads // hps),
        in_specs=in_specs,
        out_specs=pl.BlockSpec((1, tq, hps * LANES), lambda bi, hg: (bi, 0, hg)),
        out_shape=jax.ShapeDtypeStruct((b, tq, heads * LANES), BF16),
        scratch_shapes=[
            pltpu.VMEM((n_kv, s_total, k_pieces * LANES), BF16), pltpu.VMEM((n_kv, LANES, s_total), BF16),
            pltpu.VMEM((s_total, q_tile), F32), pltpu.VMEM((s_total, q_tile), F32),
            pltpu.VMEM((1, q_tile), F32), pltpu.VMEM((1, q_tile), F32),
        ],
        compiler_params=_params("parallel", "parallel"),
        name="attention",
    )(*args)


def _nb_attn_kernel(*refs, has_window, rows, tile, n_groups):
    it = iter(refs)
    q_ref, kc_ref, vc_ref = next(it), next(it), next(it)
    if has_window:
        kl_ref, vl_ref, bias_ref = next(it), next(it), next(it)
    o_ref = next(it)
    s_a, s_b, m_a, m_b = (next(it) for _ in range(4))
    c = kc_ref.shape[1]
    win = NB_KEY_ROWS * GRID_W
    lane = lax.broadcasted_iota(jnp.int32, (1, LANES), 1)

    def tile_rows(g):
        return pl.ds(pl.multiple_of(g * tile, tile), tile)

    def window(g):
        first_row = jnp.clip(NB_ROWS * g - C_WIN_ROWS // 2, 0, rows - NB_KEY_ROWS)
        return pl.ds(pl.multiple_of(first_row * GRID_W, 2 * LANES), win)

    def scores(g, s_buf, m_buf):
        q = q_ref[0, tile_rows(g), :]
        zero = jnp.zeros_like(q)
        q2 = jnp.concatenate([jnp.where(lane < C_HEAD_DIM, q, zero), jnp.where(lane >= C_HEAD_DIM, q, zero)],
                             axis=0)
        s_c = _dot_nt(kc_ref[0], q2)
        s_buf[0:c, :] = s_c
        m = jnp.max(s_c, axis=0, keepdims=True)
        if has_window:
            variant = jnp.where(g == 0, 0, jnp.where(g == n_groups - 1, 2, 1))
            s_w = _dot_nt(kl_ref[0, window(g), :], q2)
            for e in range(2):
                s_e = s_w[:, e * tile:(e + 1) * tile] + bias_ref[variant, e]
                s_buf[c:, e * tile:(e + 1) * tile] = s_e
                m_e = jnp.maximum(m[:, e * tile:(e + 1) * tile], jnp.max(s_e, axis=0, keepdims=True))
                m_buf[:, e * tile:(e + 1) * tile] = m_e
        else:
            m_buf[...] = m

    def softmax_pv(g, s_buf, m_buf):
        p = jnp.exp2(s_buf[...] - m_buf[...])
        l = jnp.sum(p, axis=0, keepdims=True)
        pb = p.astype(BF16)
        o2 = _dot(vc_ref[0], pb[0:c])
        if has_window:
            o2 = o2 + _dot(vl_ref[0, :, window(g)], pb[c:])
        o_t = jnp.concatenate([o2[e * C_HEAD_DIM:(e + 1) * C_HEAD_DIM, e * tile:(e + 1) * tile]
                               / l[:, e * tile:(e + 1) * tile] for e in range(2)], axis=0)
        o_ref[0, tile_rows(g), :] = o_t.T.astype(BF16)

    scores(0, s_a, m_a)
    if n_groups == 1:
        softmax_pv(0, s_a, m_a)
        return

    def phase(g_next, s_next, m_next, g_cur, s_cur, m_cur_buf):
        q = q_ref[0, tile_rows(g_next), :]
        zero = jnp.zeros_like(q)
        q2 = jnp.concatenate([jnp.where(lane < C_HEAD_DIM, q, zero), jnp.where(lane >= C_HEAD_DIM, q, zero)],
                             axis=0)
        m_cur = m_cur_buf[...]
        variant = jnp.where(g_next == 0, 0, jnp.where(g_next == n_groups - 1, 2, 1))
        w_next, w_cur = window(g_next), window(g_cur)
        s_c = _dot_nt(kc_ref[0], q2)
        s_next[0:c, :] = s_c
        m8 = jnp.max(s_c.reshape(-1, 8, 2 * tile), axis=0)
        p = jnp.exp2(s_cur[0:c, :] - m_cur)
        l8 = jnp.sum(p.reshape(-1, 8, 2 * tile), axis=0)
        o2 = _dot(vc_ref[0], p.astype(BF16))
        for c0 in range(0, win, NB_CHUNK):
            s_w = _dot_nt(kl_ref[0, pl.ds(w_next.start + c0, NB_CHUNK), :], q2)
            s_w = jnp.concatenate([s_w[:, e * tile:(e + 1) * tile] + bias_ref[variant, e, c0:c0 + NB_CHUNK, :]
                                   for e in range(2)], axis=1)
            s_next[c + c0:c + c0 + NB_CHUNK, :] = s_w
            m8 = jnp.maximum(m8, jnp.max(s_w.reshape(-1, 8, 2 * tile), axis=0))
            p = jnp.exp2(s_cur[c + c0:c + c0 + NB_CHUNK, :] - m_cur)
            l8 = l8 + jnp.sum(p.reshape(-1, 8, 2 * tile), axis=0)
            o2 = o2 + _dot(vl_ref[0, :, pl.ds(w_cur.start + c0, NB_CHUNK)], p.astype(BF16))
        m_next[...] = jnp.max(m8, axis=0, keepdims=True)
        l = jnp.sum(l8, axis=0, keepdims=True)
        o_t = jnp.concatenate([o2[e * C_HEAD_DIM:(e + 1) * C_HEAD_DIM, e * tile:(e + 1) * tile]
                               / l[:, e * tile:(e + 1) * tile] for e in range(2)], axis=0)
        o_ref[0, tile_rows(g_cur), :] = o_t.T.astype(BF16)

    def pair(i, carry):
        g = 2 * i
        phase(g + 1, s_b, m_b, g, s_a, m_a)
        phase(jnp.minimum(g + 2, n_groups - 1), s_a, m_a, g + 1, s_b, m_b)
        return carry

    lax.fori_loop(0, n_groups // 2, pair, 0)


def _nb_attention(q, k_ctx, vt_ctx, k_lat=None, vt_lat=None, bias=None):
    b, tq, hw = q.shape
    pairs = hw // LANES
    has_window = k_lat is not None
    c = k_ctx.shape[1]
    if has_window:
        t = k_lat.shape[1]
        rows = t // GRID_W
        tile = NB_ROWS * GRID_W
        keys = c + NB_KEY_ROWS * GRID_W
    else:
        rows, tile, keys = 0, tq, c
    n_groups = tq // tile
    assert n_groups == 1 or n_groups % 2 == 0
    in_specs = [
        pl.BlockSpec((1, tq, LANES), lambda p, bi: (bi, 0, p)),
        pl.BlockSpec((1, c, LANES), lambda p, bi: (bi, 0, p)),
        pl.BlockSpec((1, LANES, c), lambda p, bi: (bi, p, 0)),
    ]
    args = [q, k_ctx, vt_ctx]
    if has_window:
        in_specs += [
            pl.BlockSpec((1, t, LANES), lambda p, bi: (bi, 0, p)),
            pl.BlockSpec((1, LANES, t), lambda p, bi: (bi, p, 0)),
            pl.BlockSpec((3, 2, NB_KEY_ROWS * GRID_W, tile), lambda p, bi: (0, p, 0, 0)),
        ]
        args += [k_lat, vt_lat, bias]
    return pl.pallas_call(
        functools.partial(_nb_attn_kernel, has_window=has_window, rows=rows, tile=tile, n_groups=n_groups),
        grid=(pairs, b),
        in_specs=in_specs,
        out_specs=pl.BlockSpec((1, tq, LANES), lambda p, bi: (bi, 0, p)),
        out_shape=jax.ShapeDtypeStruct((b, tq, hw), BF16),
        scratch_shapes=[pltpu.VMEM((keys, 2 * tile), F32), pltpu.VMEM((keys, 2 * tile), F32),
                        pltpu.VMEM((1, 2 * tile), F32), pltpu.VMEM((1, 2 * tile), F32)],
        compiler_params=_params("parallel", "parallel"),
        name="nb_attention",
    )(*args)


def _nb_bias_table(rpb, rows):
    heads = rpb.shape[0]
    tile_q = NB_ROWS * GRID_W
    return pl.pallas_call(
        functools.partial(_nb_bias_kernel, rows=rows),
        grid=(heads,),
        in_specs=[pl.BlockSpec(memory_space=pltpu.SMEM)],
        out_specs=pl.BlockSpec((3, 1, NB_KEY_ROWS * GRID_W, tile_q), lambda h: (0, h, 0, 0)),
        out_shape=jax.ShapeDtypeStruct((3, heads, NB_KEY_ROWS * GRID_W, tile_q), F32),
        compiler_params=_params("parallel"),
        name="nb_bias",
    )(rpb.reshape(-1))


def _nb_bias_kernel(rpb_ref, o_ref, *, rows):
    n_a, n_b = 2 * C_WIN_ROWS - 1, 2 * C_WIN_COLS - 1
    tile_q = NB_ROWS * GRID_W
    shape = (GRID_W, tile_q)
    kc = lax.broadcasted_iota(jnp.int32, shape, 0)
    lane = lax.broadcasted_iota(jnp.int32, shape, 1)
    qc = lane % GRID_W
    qi = lane // GRID_W
    c0 = jnp.clip(qc - C_WIN_COLS // 2, 0, GRID_W - C_WIN_COLS)
    col_ok = (kc >= c0) & (kc < c0 + C_WIN_COLS)
    dcol = kc - qc + (C_WIN_COLS - 1)
    base = pl.program_id(0) * (n_a * n_b)
    masked = jnp.full(shape, MASK_VALUE, F32)
    planes = []
    for a in range(n_a):
        acc = masked
        for bb in range(n_b):
            acc = jnp.where(dcol == bb, rpb_ref[base + a * n_b + bb], acc)
        planes.append(jnp.where(col_ok, acc * LOG2E, MASK_VALUE))
    groups = rows // NB_ROWS
    for v, g in enumerate((0, 1, groups - 1)):
        first_key_row = min(max(NB_ROWS * g - C_WIN_ROWS // 2, 0), rows - NB_KEY_ROWS)
        for j in range(NB_KEY_ROWS):
            kr = first_key_row + j
            blk = masked
            for i in range(NB_ROWS):
                qr = NB_ROWS * g + i
                r0 = min(max(qr - C_WIN_ROWS // 2, 0), rows - C_WIN_ROWS)
                if r0 <= kr < r0 + C_WIN_ROWS:
                    blk = jnp.where(qi == i, planes[kr - qr + C_WIN_ROWS - 1], blk)
            o_ref[v, 0, j * GRID_W:(j + 1) * GRID_W, :] = blk


def _rope_tables(rows, rot_dim):
    n = rot_dim // 4
    inv_freq = ROPE_THETA ** (-jnp.arange(n, dtype=F32) / n)
    t = jnp.arange(rows * GRID_W, dtype=jnp.int32)
    r = (t // GRID_W).astype(F32)
    col = (t % GRID_W).astype(F32)
    ang = jnp.concatenate([r[:, None] * inv_freq[None, :], col[:, None] * inv_freq[None, :]], axis=-1)
    cos, sin = jnp.cos(ang), jnp.sin(ang)
    reps = LANES // rot_dim
    return (jnp.tile(jnp.concatenate([cos, cos], axis=-1), (1, reps)),
            jnp.tile(jnp.concatenate([-sin, sin], axis=-1), (1, reps)))


def _mla_weights(w_in, q_norm, kv_norm, w_uq, w_ukv, w_o):
    rank = A_Q_RANK + A_KV_RANK
    k_r = w_in[:, rank:]
    uq = w_uq.reshape(A_Q_RANK, A_HEADS, A_NOPE + A_ROPE)
    ukv = w_ukv.reshape(A_KV_RANK, A_HEADS, A_NOPE + A_V)
    return {
        "w_in": jnp.concatenate([w_in[:, :rank], k_r, k_r], axis=1).astype(BF16),
        "q_norm": q_norm, "kv_norm": kv_norm,
        "w_uq": jnp.concatenate([uq[:, :, :A_NOPE].reshape(A_Q_RANK, -1),
                                 uq[:, :, A_NOPE:].reshape(A_Q_RANK, -1)], axis=1).astype(BF16),
        "w_ukv": jnp.concatenate([ukv[:, :, :A_NOPE].reshape(A_KV_RANK, -1),
                                  ukv[:, :, A_NOPE:].reshape(A_KV_RANK, -1)], axis=1).astype(BF16),
        "w_o": w_o.astype(BF16),
    }


def kernel(x, c, ctx, c_ctx, norm_g, w_mod, b_mod, ffn1_w13, ffn1_w2, ffn2_w13, ffn2_w2, a_w_in, a_q_norm, a_kv_norm, a_w_uq, a_w_ukv, a_w_o, b_w_qkv, b_q_norm, b_k_norm, b_w_o, c_w_qkv, c_rpb, c_w_o, final_norm_g):
    b, t, d = x.shape
    depth = w_mod.shape[0]
    rows = t // GRID_W
    assert b < MOD_ROWS and t % (NB_ROWS * GRID_W) == 0 and rows >= NB_KEY_ROWS + NB_ROWS

    c_rows = jnp.zeros((MOD_ROWS, d), F32).at[:b].set(c).at[b].set(c_ctx)
    mod_all = _modulation(c_rows, w_mod, b_mod).reshape(depth, MOD_ROWS, N_MOD, d)
    lat_row = lambda bi: bi
    ctx_row = lambda bi: b

    rope_a = _rope_tables(rows, A_ROPE)
    rope_b = _rope_tables(rows, B_HEAD_DIM)

    xc = ctx
    for i in range(depth):
        ctx_out = i < depth - 1
        last = i == depth - 1
        mod = mod_all[i]
        kind, j = i % N_MIXERS, i // N_MIXERS
        w13_1, w2_1 = ffn1_w13[i].astype(BF16), ffn1_w2[i].astype(BF16)
        w13_2, w2_2 = ffn2_w13[i].astype(BF16), ffn2_w2[i].astype(BF16)

        x = _ffn(x, mod, lat_row, norm_g[i, 0], w13_1, w2_1, k0=0)
        xc = _ffn(xc, mod, ctx_row, norm_g[i, 0], w13_1, w2_1, k0=0)

        if kind == 0:
            w = _mla_weights(a_w_in[j], a_q_norm[j], a_kv_norm[j], a_w_uq[j], a_w_ukv[j], a_w_o[j])
            q, kn, kr, vt = _mla_proj(x, mod, lat_row, norm_g[i, 1], w, rope_a)
            qc, knc, krc, vtc = _mla_proj(xc, mod, ctx_row, norm_g[i, 1], w, None)
            cfg = dict(heads=A_HEADS, hps=A_HEADS_PER_STEP, per_head=(True, False), share_kv=False,
                       key_chunk=MLA_KEY_CHUNK)
            o = _attention(q, [[knc, krc], [kn, kr]], [vtc, vt], **cfg)
            if ctx_out:
                oc = _attention(qc, [[knc, krc]], [vtc], **dict(cfg, hps=A_HEADS))
            w_o = w["w_o"]
        elif kind == 1:
            w = {"w_qkv": b_w_qkv[j].astype(BF16), "q_norm": b_q_norm[j], "k_norm": b_k_norm[j]}
            q, k, vt = _gqa_proj(x, mod, lat_row, norm_g[i, 1], w, rope_b)
            qc, kc, vtc = _gqa_proj(xc, mod, ctx_row, norm_g[i, 1], w, None)
            cfg = dict(heads=B_HEADS, hps=B_HEADS // B_KV_HEADS, per_head=(False,), share_kv=True,
                       key_chunk=GQA_KEY_CHUNK)
            o = _attention(q, [[kc], [k]], [vtc, vt], **cfg)
            if ctx_out:
                oc = _attention(qc, [[kc]], [vtc], **cfg)
            w_o = b_w_o[j].astype(BF16)
        else:
            w = {"w_qkv": c_w_qkv[j].astype(BF16)}
            q, k, vt = _nb_proj(x, mod, lat_row, norm_g[i, 1], w)
            qc, kc, vtc = _nb_proj(xc, mod, ctx_row, norm_g[i, 1], w)
            o = _nb_attention(q, kc, vtc, k, vt, _nb_bias_table(c_rpb[j], rows))
            if ctx_out:
                oc = _nb_attention(qc, kc, vtc)
            w_o = c_w_o[j].astype(BF16)

        x = _ffn(x, mod, lat_row, norm_g[i, 2], w13_2, w2_2, k0=6, attn=o, w_o=w_o,
                 final_g=final_norm_g if last else None)
        if ctx_out:
            xc = _ffn(xc, mod, ctx_row, norm_g[i, 2], w13_2, w2_2, k0=6, attn=oc, w_o=w_o)
    return x
```

```python
import functools

import jax
import jax.numpy as jnp
from jax import lax
from jax.experimental import pallas as pl
from jax.experimental.pallas import tpu as pltpu

F32 = jnp.float32
BF16 = jnp.bfloat16

GRID_W = 64
N_MIXERS = 3
N_MOD = 9
EPS = 1e-6
ROPE_THETA = 10000.0

A_HEADS, A_NOPE, A_ROPE, A_V = 8, 128, 64, 128
A_Q_RANK, A_KV_RANK = 384, 256
B_HEADS, B_KV_HEADS, B_HEAD_DIM = 8, 2, 128
C_HEADS, C_HEAD_DIM, C_WIN_ROWS, C_WIN_COLS = 16, 64, 8, 16

LANES = 128
MOD_ROWS = 16
MASK_VALUE = -1e30
LOG2E = 1.4426950408889634
VMEM_LIMIT = 56 * 1024 * 1024

ROW_TILE = 512
GQA_ROW_TILE = 256
A_HEADS_PER_STEP = 2
Q_TILE = 512
GQA_KEY_CHUNK = 512
MLA_KEY_CHUNK = 0
NB_CHUNK = 256
NB_ROWS = 4
NB_KEY_ROWS = 12


def _params(*sem):
    return pltpu.CompilerParams(dimension_semantics=sem, vmem_limit_bytes=VMEM_LIMIT)


def _const_spec(shape):
    nd = len(shape)
    return pl.BlockSpec(shape, lambda *_: (0,) * nd, pipeline_mode=pl.Buffered(1))


def _dot(a, b):
    return jnp.dot(a, b, preferred_element_type=F32)


def _dot_nt(a, b):
    return lax.dot_general(a, b, (((1,), (1,)), ((), ())), preferred_element_type=F32)


def _rms(x):
    return x * lax.rsqrt(jnp.mean(x * x, axis=-1, keepdims=True) + EPS)


def _modulated_norm(x, g, scale, shift):
    return _rms(x) * (g * (1.0 + scale)) + shift


def _silu(x):
    return x / (1.0 + jnp.exp(-x))


def _mod_kernel(c_ref, w_ref, b_ref, o_ref):
    sc = _silu(c_ref[...]).astype(BF16)
    o_ref[0] = _dot(sc, w_ref[0].astype(BF16)) + b_ref[0]


def _modulation(c_rows, w_mod, b_mod):
    depth, d, n = w_mod.shape
    tn = d
    return pl.pallas_call(
        _mod_kernel,
        grid=(depth, n // tn),
        in_specs=[
            pl.BlockSpec((MOD_ROWS, d), lambda i, j: (0, 0)),
            pl.BlockSpec((1, d, tn), lambda i, j: (i, 0, j)),
            pl.BlockSpec((1, 1, tn), lambda i, j: (i, 0, j)),
        ],
        out_specs=pl.BlockSpec((1, MOD_ROWS, tn), lambda i, j: (i, 0, j)),
        out_shape=jax.ShapeDtypeStruct((depth, MOD_ROWS, n), F32),
        compiler_params=_params("parallel", "parallel"),
        name="modulation",
    )(c_rows, w_mod, b_mod.reshape(depth, 1, n))


def _ffn_kernel(*refs, k0, d_ff, has_oproj, final_norm):
    it = iter(refs)
    x_ref, mod_ref, g_ref, w13_ref, w2_ref = (next(it) for _ in range(5))
    if has_oproj:
        a_ref, wo_ref = next(it), next(it)
    if final_norm:
        fg_ref = next(it)
    out_ref = next(it)

    x = x_ref[0]
    mod = mod_ref[0]
    if has_oproj:
        x = x + mod[5:6] * _dot(a_ref[0], wo_ref[...])
    h = _modulated_norm(x, g_ref[...], mod[k0 + 1:k0 + 2], mod[k0:k0 + 1]).astype(BF16)
    hgu = _dot(h, w13_ref[...])
    act = (_silu(hgu[:, :d_ff]) * hgu[:, d_ff:]).astype(BF16)
    y = x + (0.5 * mod[k0 + 2:k0 + 3]) * _dot(act, w2_ref[...])
    if final_norm:
        y = _rms(y) * fg_ref[...]
    out_ref[0] = y


def _ffn(x, mod, mod_row, g, w13, w2, *, k0, attn=None, w_o=None, final_g=None):
    b, t, d = x.shape
    d_ff = w2.shape[0]
    tm = min(ROW_TILE, t)
    row = lambda bi, ti: (bi, ti, 0)
    in_specs = [
        pl.BlockSpec((1, tm, d), row),
        pl.BlockSpec((1, N_MOD, d), lambda bi, ti: (mod_row(bi), 0, 0)),
        _const_spec((1, d)),
        _const_spec(w13.shape),
        _const_spec(w2.shape),
    ]
    args = [x, mod, g.reshape(1, d), w13, w2]
    if attn is not None:
        in_specs += [pl.BlockSpec((1, tm, attn.shape[2]), row), _const_spec(w_o.shape)]
        args += [attn, w_o]
    if final_g is not None:
        in_specs.append(_const_spec((1, d)))
        args.append(final_g.reshape(1, d))
    return pl.pallas_call(
        functools.partial(_ffn_kernel, k0=k0, d_ff=d_ff, has_oproj=attn is not None,
                          final_norm=final_g is not None),
        grid=(b, t // tm),
        in_specs=in_specs,
        out_specs=pl.BlockSpec((1, tm, d), row),
        out_shape=jax.ShapeDtypeStruct((b, t, d), F32),
        compiler_params=_params("parallel", "parallel"),
        name="half_ffn",
    )(*args)


def _rope_pairs(x, cos, sin_signed, half):
    if 2 * half == LANES:
        rot = pltpu.roll(x, half, 1)
    else:
        lane = lax.broadcasted_iota(jnp.int32, x.shape, 1)
        first = (lane % (2 * half)) < half
        rot = jnp.where(first, pltpu.roll(x, LANES - half, 1), pltpu.roll(x, half, 1))
    return x * cos + rot * sin_signed


def _mla_proj_kernel(*refs, use_rope, sm_scale):
    it = iter(refs)
    x_ref, mod_ref, g_ref, w_in_ref, qg_ref, kvg_ref, w_uq_ref, w_ukv_ref = (next(it) for _ in range(8))
    if use_rope:
        cos_ref, sin_ref = next(it), next(it)
    q_ref, kn_ref, kr_ref, vt_ref = (next(it) for _ in range(4))

    mod = mod_ref[0]
    h = _modulated_norm(x_ref[0], g_ref[...], mod[4:5], mod[3:4]).astype(BF16)
    proj = _dot(h, w_in_ref[...])
    c_q = (_rms(proj[:, :A_Q_RANK]) * qg_ref[...]).astype(BF16)
    c_kv = (_rms(proj[:, A_Q_RANK:A_Q_RANK + A_KV_RANK]) * kvg_ref[...]).astype(BF16)
    k_r = proj[:, A_Q_RANK + A_KV_RANK:]
    q_all = _dot(c_q, w_uq_ref[...])
    kv = _dot(c_kv, w_ukv_ref[...])
    n_nope = A_HEADS * A_NOPE
    if use_rope:
        cos, sin = cos_ref[...], sin_ref[...]
        k_r = _rope_pairs(k_r, cos, sin, A_ROPE // 2)
    lane = lax.broadcasted_iota(jnp.int32, (1, LANES), 1)
    for j in range(A_HEADS // 2):
        qr = q_all[:, n_nope + j * LANES:n_nope + (j + 1) * LANES]
        if use_rope:
            qr = _rope_pairs(qr, cos, sin, A_ROPE // 2)
        qr = qr * sm_scale
        for e in range(2):
            hd = 2 * j + e
            keep = (lane < A_ROPE) if e == 0 else (lane >= A_ROPE)
            q_ref[0, :, 2 * hd * LANES:(2 * hd + 1) * LANES] = (
                q_all[:, hd * A_NOPE:(hd + 1) * A_NOPE] * sm_scale).astype(BF16)
            q_ref[0, :, (2 * hd + 1) * LANES:(2 * hd + 2) * LANES] = jnp.where(keep, qr, 0.0).astype(BF16)
    kn_ref[0] = kv[:, :n_nope].astype(BF16)
    kr_ref[0] = k_r.astype(BF16)
    vt_ref[0] = kv[:, n_nope:].T.astype(BF16)


def _mla_proj(x, mod, mod_row, g, w, rope):
    b, t, d = x.shape
    tm = min(ROW_TILE, t)
    row = lambda bi, ti: (bi, ti, 0)
    in_specs = [
        pl.BlockSpec((1, tm, d), row),
        pl.BlockSpec((1, N_MOD, d), lambda bi, ti: (mod_row(bi), 0, 0)),
        _const_spec((1, d)),
        _const_spec(w["w_in"].shape), _const_spec((1, A_Q_RANK)), _const_spec((1, A_KV_RANK)),
        _const_spec(w["w_uq"].shape), _const_spec(w["w_ukv"].shape),
    ]
    args = [x, mod, g.reshape(1, d), w["w_in"], w["q_norm"].reshape(1, -1), w["kv_norm"].reshape(1, -1),
            w["w_uq"], w["w_ukv"]]
    if rope is not None:
        in_specs += [pl.BlockSpec((tm, LANES), lambda bi, ti: (ti, 0))] * 2
        args += list(rope)
    hv = A_HEADS * A_V
    return pl.pallas_call(
        functools.partial(_mla_proj_kernel, use_rope=rope is not None,
                          sm_scale=float((A_NOPE + A_ROPE) ** -0.5 * LOG2E)),
        grid=(b, t // tm),
        in_specs=in_specs,
        out_specs=[
            pl.BlockSpec((1, tm, 2 * LANES * A_HEADS), row),
            pl.BlockSpec((1, tm, A_HEADS * A_NOPE), row),
            pl.BlockSpec((1, tm, LANES), row),
            pl.BlockSpec((1, hv, tm), lambda bi, ti: (bi, 0, ti)),
        ],
        out_shape=[
            jax.ShapeDtypeStruct((b, t, 2 * LANES * A_HEADS), BF16),
            jax.ShapeDtypeStruct((b, t, A_HEADS * A_NOPE), BF16),
            jax.ShapeDtypeStruct((b, t, LANES), BF16),
            jax.ShapeDtypeStruct((b, hv, t), BF16),
        ],
        compiler_params=_params("parallel", "parallel"),
        name="mla_proj",
    )(*args)


def _gqa_proj_kernel(*refs, use_rope, sm_scale):
    it = iter(refs)
    x_ref, mod_ref, g_ref, w_ref, qg_ref, kg_ref = (next(it) for _ in range(6))
    if use_rope:
        cos_ref, sin_ref = next(it), next(it)
    q_ref, k_ref, vt_ref = (next(it) for _ in range(3))

    mod = mod_ref[0]
    h = _modulated_norm(x_ref[0], g_ref[...], mod[4:5], mod[3:4]).astype(BF16)
    proj = _dot(h, w_ref[...])
    qw = B_HEADS * B_HEAD_DIM
    kw = B_KV_HEADS * B_HEAD_DIM
    if use_rope:
        cos, sin = cos_ref[...], sin_ref[...]

    def head(col, gain, scale):
        y = _rms(proj[:, col:col + B_HEAD_DIM]) * gain
        if use_rope:
            y = _rope_pairs(y, cos, sin, B_HEAD_DIM // 2)
        return (y * scale).astype(BF16) if scale != 1.0 else y.astype(BF16)

    for hd in range(B_HEADS):
        q_ref[0, :, hd * B_HEAD_DIM:(hd + 1) * B_HEAD_DIM] = head(hd * B_HEAD_DIM, qg_ref[...], sm_scale)
    for hd in range(B_KV_HEADS):
        k_ref[0, :, hd * B_HEAD_DIM:(hd + 1) * B_HEAD_DIM] = head(qw + hd * B_HEAD_DIM, kg_ref[...], 1.0)
    vt_ref[0] = proj[:, qw + kw:].T.astype(BF16)


def _gqa_proj(x, mod, mod_row, g, w, rope):
    b, t, d = x.shape
    tm = min(GQA_ROW_TILE, t)
    row = lambda bi, ti: (bi, ti, 0)
    qw, kw = B_HEADS * B_HEAD_DIM, B_KV_HEADS * B_HEAD_DIM
    in_specs = [
        pl.BlockSpec((1, tm, d), row),
        pl.BlockSpec((1, N_MOD, d), lambda bi, ti: (mod_row(bi), 0, 0)),
        _const_spec((1, d)),
        _const_spec(w["w_qkv"].shape), _const_spec((1, B_HEAD_DIM)), _const_spec((1, B_HEAD_DIM)),
    ]
    args = [x, mod, g.reshape(1, d), w["w_qkv"], w["q_norm"].reshape(1, -1), w["k_norm"].reshape(1, -1)]
    if rope is not None:
        in_specs += [pl.BlockSpec((tm, LANES), lambda bi, ti: (ti, 0))] * 2
        args += list(rope)
    return pl.pallas_call(
        functools.partial(_gqa_proj_kernel, use_rope=rope is not None, sm_scale=float(B_HEAD_DIM ** -0.5 * LOG2E)),
        grid=(b, t // tm),
        in_specs=in_specs,
        out_specs=[
            pl.BlockSpec((1, tm, qw), row),
            pl.BlockSpec((1, tm, kw), row),
            pl.BlockSpec((1, kw, tm), lambda bi, ti: (bi, 0, ti)),
        ],
        out_shape=[
            jax.ShapeDtypeStruct((b, t, qw), BF16),
            jax.ShapeDtypeStruct((b, t, kw), BF16),
            jax.ShapeDtypeStruct((b, kw, t), BF16),
        ],
        compiler_params=_params("parallel", "parallel"),
        name="gqa_proj",
    )(*args)


def _nb_proj_kernel(x_ref, mod_ref, g_ref, w_ref, q_ref, k_ref, vt_ref, *, sm_scale):
    mod = mod_ref[0]
    h = _modulated_norm(x_ref[0], g_ref[...], mod[4:5], mod[3:4]).astype(BF16)
    proj = _dot(h, w_ref[...])
    hd = C_HEADS * C_HEAD_DIM
    q_ref[0] = (proj[:, :hd] * sm_scale).astype(BF16)
    k_ref[0] = proj[:, hd:2 * hd].astype(BF16)
    vt_ref[0] = proj[:, 2 * hd:].T.astype(BF16)


def _nb_proj(x, mod, mod_row, g, w):
    b, t, d = x.shape
    tm = min(ROW_TILE, t)
    row = lambda bi, ti: (bi, ti, 0)
    hd = C_HEADS * C_HEAD_DIM
    return pl.pallas_call(
        functools.partial(_nb_proj_kernel, sm_scale=float(C_HEAD_DIM ** -0.5 * LOG2E)),
        grid=(b, t // tm),
        in_specs=[
            pl.BlockSpec((1, tm, d), row),
            pl.BlockSpec((1, N_MOD, d), lambda bi, ti: (mod_row(bi), 0, 0)),
            _const_spec((1, d)),
            _const_spec(w["w_qkv"].shape),
        ],
        out_specs=[
            pl.BlockSpec((1, tm, hd), row),
            pl.BlockSpec((1, tm, hd), row),
            pl.BlockSpec((1, hd, tm), lambda bi, ti: (bi, 0, ti)),
        ],
        out_shape=[
            jax.ShapeDtypeStruct((b, t, hd), BF16),
            jax.ShapeDtypeStruct((b, t, hd), BF16),
            jax.ShapeDtypeStruct((b, hd, t), BF16),
        ],
        compiler_params=_params("parallel", "parallel"),
        name="nb_proj",
    )(x, mod, g.reshape(1, d), w["w_qkv"])


def _attn_kernel(*refs, key_rows, per_head, n_kv, hps, dq, q_tile, n_tiles, key_chunk):
    n_groups, k_pieces = len(key_rows), len(per_head)
    it = iter(refs)
    q_ref = next(it)
    k_refs = [[next(it) for _ in range(k_pieces)] for _ in range(n_groups)]
    v_refs = [next(it) for _ in range(n_groups)]
    o_ref = next(it)
    k_scr, v_scr, s_a, s_b, m_a, m_b = (next(it) for _ in range(6))

    r0 = 0
    for gi, rows in enumerate(key_rows):
        for j in range(n_kv):
            for pi in range(k_pieces):
                lanes = slice(j * LANES, (j + 1) * LANES) if per_head[pi] else slice(0, LANES)
                k_scr[j, r0:r0 + rows, pi * LANES:(pi + 1) * LANES] = k_refs[gi][pi][0, :, lanes]
            v_scr[j, :, r0:r0 + rows] = v_refs[gi][0, j * LANES:(j + 1) * LANES, :]
        r0 += rows

    def rows_of(tile):
        return pl.ds(pl.multiple_of(tile * q_tile, q_tile), q_tile)

    def kv_of(head):
        return head if n_kv > 1 else 0

    def scores(tile, head, s_buf, m_buf):
        q = q_ref[0, rows_of(tile), head * dq:(head + 1) * dq]
        s_t = _dot_nt(k_scr[kv_of(head)], q)
        s_buf[...] = s_t
        m_buf[...] = jnp.max(s_t, axis=0, keepdims=True)

    def softmax_pv(tile, head, s_buf, m_buf):
        p = jnp.exp2(s_buf[...] - m_buf[...])
        l = jnp.sum(p, axis=0, keepdims=True)
        o_t = _dot(v_scr[kv_of(head)], p.astype(BF16)) / l
        o_ref[0, rows_of(tile), head * LANES:(head + 1) * LANES] = o_t.T.astype(BF16)

    scores(0, 0, s_a, m_a)
    if hps * n_tiles == 1:
        softmax_pv(0, 0, s_a, m_a)
        return

    s_rows = s_a.shape[0]
    bufs = ((s_a, m_a), (s_b, m_b))

    def phase(tile_n, head_n, s_next, m_next, tile_c, head_c, s_cur, m_cur_buf):
        if not key_chunk:
            scores(tile_n, head_n, s_next, m_next)
            softmax_pv(tile_c, head_c, s_cur, m_cur_buf)
            return
        q = q_ref[0, rows_of(tile_n), head_n * dq:(head_n + 1) * dq]
        m_cur = m_cur_buf[...]
        m8 = l8 = acc = None
        for c0 in range(0, s_rows, key_chunk):
            rows = slice(c0, min(c0 + key_chunk, s_rows))
            s_c = _dot_nt(k_scr[kv_of(head_n), rows, :], q)
            s_next[rows, :] = s_c
            mc = jnp.max(s_c.reshape(-1, 8, q_tile), axis=0)
            m8 = mc if m8 is None else jnp.maximum(m8, mc)
            p = jnp.exp2(s_cur[rows, :] - m_cur)
            lc = jnp.sum(p.reshape(-1, 8, q_tile), axis=0)
            l8 = lc if l8 is None else l8 + lc
            a = _dot(v_scr[kv_of(head_c), :, rows], p.astype(BF16))
            acc = a if acc is None else acc + a
        m_next[...] = jnp.max(m8, axis=0, keepdims=True)
        l = jnp.sum(l8, axis=0, keepdims=True)
        o_ref[0, rows_of(tile_c), head_c * LANES:(head_c + 1) * LANES] = (acc / l).T.astype(BF16)

    def one_tile(tile, carry):
        for head in range(hps):
            if head + 1 < hps:
                nxt = (tile, head + 1)
            else:
                nxt = (jnp.minimum(tile + 1, n_tiles - 1), 0)
            phase(*nxt, *bufs[(head + 1) % 2], tile, head, *bufs[head % 2])
        return carry

    assert hps % 2 == 0
    lax.fori_loop(0, n_tiles, one_tile, 0)


def _attention(q, k_groups, v_groups, *, heads, hps, per_head, share_kv, key_chunk):
    b, tq, qw = q.shape
    dq = qw // heads
    q_tile = min(Q_TILE, tq)
    n_tiles = tq // q_tile
    n_kv = 1 if share_kv else hps
    assert heads % hps == 0 and (hps * n_tiles == 1 or hps % 2 == 0)
    key_rows = tuple(g[0].shape[1] for g in k_groups)
    k_pieces = len(per_head)
    s_total = sum(key_rows)
    in_specs = [pl.BlockSpec((1, tq, hps * dq), lambda bi, hg: (bi, 0, hg))]
    args = [q]
    for grp in k_groups:
        for pi, arr in enumerate(grp):
            width = n_kv * LANES if per_head[pi] else LANES
            moves = per_head[pi] or share_kv
            in_specs.append(pl.BlockSpec((1, arr.shape[1], width),
                                         lambda bi, hg, moves=moves: (bi, 0, hg if moves else 0)))
            args.append(arr)
    for arr in v_groups:
        in_specs.append(pl.BlockSpec((1, n_kv * LANES, arr.shape[2]), lambda bi, hg: (bi, hg, 0)))
        args.append(arr)
    return pl.pallas_call(
        functools.partial(_attn_kernel, key_rows=key_rows, per_head=tuple(per_head), n_kv=n_kv, hps=hps,
                          dq=dq, q_tile=q_tile, n_tiles=n_tiles, key_chunk=key_chunk),
        grid=(b, heads // hps),
        in_specs=in_specs,
        out_specs=pl.BlockSpec((1, tq, hps * LANES), lambda bi, hg: (bi, 0, hg)),
        out_shape=jax.ShapeDtypeStruct((b, tq, heads * LANES), BF16),
        scratch_shapes=[
            pltpu.VMEM((n_kv, s_total, k_pieces * LANES), BF16), pltpu.VMEM((n_kv, LANES, s_total), BF16),
            pltpu.VMEM((s_total, q_tile), F32), pltpu.VMEM((s_total, q_tile), F32),
            pltpu.VMEM((1, q_tile), F32), pltpu.VMEM((1, q_tile), F32),
        ],
        compiler_params=_params("parallel", "parallel"),
        name="attention",
    )(*args)


def _nb_attn_kernel(*refs, has_window, rows, tile, n_groups):
    it = iter(refs)
    q_ref, kc_ref, vc_ref = next(it), next(it), next(it)
    if has_window:
        kl_ref, vl_ref, bias_ref = next(it), next(it), next(it)
    o_ref = next(it)
    s_a, s_b, m_a, m_b = (next(it) for _ in range(4))
    c = kc_ref.shape[1]
    win = NB_KEY_ROWS * GRID_W
    lane = lax.broadcasted_iota(jnp.int32, (1, LANES), 1)

    def tile_rows(g):
        return pl.ds(pl.multiple_of(g * tile, tile), tile)

    def window(g):
        first_row = jnp.clip(NB_ROWS * g - C_WIN_ROWS // 2, 0, rows - NB_KEY_ROWS)
        return pl.ds(pl.multiple_of(first_row * GRID_W, 2 * LANES), win)

    def scores(g, s_buf, m_buf):
        q = q_ref[0, tile_rows(g), :]
        zero = jnp.zeros_like(q)
        q2 = jnp.concatenate([jnp.where(lane < C_HEAD_DIM, q, zero), jnp.where(lane >= C_HEAD_DIM, q, zero)],
                             axis=0)
        s_c = _dot_nt(kc_ref[0], q2)
        s_buf[0:c, :] = s_c
        m = jnp.max(s_c, axis=0, keepdims=True)
        if has_window:
            variant = jnp.where(g == 0, 0, jnp.where(g == n_groups - 1, 2, 1))
            s_w = _dot_nt(kl_ref[0, window(g), :], q2)
            for e in range(2):
                s_e = s_w[:, e * tile:(e + 1) * tile] + bias_ref[variant, e]
                s_buf[c:, e * tile:(e + 1) * tile] = s_e
                m_e = jnp.maximum(m[:, e * tile:(e + 1) * tile], jnp.max(s_e, axis=0, keepdims=True))
                m_buf[:, e * tile:(e + 1) * tile] = m_e
        else:
            m_buf[...] = m

    def softmax_pv(g, s_buf, m_buf):
        p = jnp.exp2(s_buf[...] - m_buf[...])
        l = jnp.sum(p, axis=0, keepdims=True)
        pb = p.astype(BF16)
        o2 = _dot(vc_ref[0], pb[0:c])
        if has_window:
            o2 = o2 + _dot(vl_ref[0, :, window(g)], pb[c:])
        o_t = jnp.concatenate([o2[e * C_HEAD_DIM:(e + 1) * C_HEAD_DIM, e * tile:(e + 1) * tile]
                               / l[:, e * tile:(e + 1) * tile] for e in range(2)], axis=0)
        o_ref[0, tile_rows(g), :] = o_t.T.astype(BF16)

    scores(0, s_a, m_a)
    if n_groups == 1:
        softmax_pv(0, s_a, m_a)
        return

    def phase(g_next, s_next, m_next, g_cur, s_cur, m_cur_buf):
        q = q_ref[0, tile_rows(g_next), :]
        zero = jnp.zeros_like(q)
        q2 = jnp.concatenate([jnp.where(lane < C_HEAD_DIM, q, zero), jnp.where(lane >= C_HEAD_DIM, q, zero)],
                             axis=0)
        m_cur = m_cur_buf[...]
        variant = jnp.where(g_next == 0, 0, jnp.where(g_next == n_groups - 1, 2, 1))
        w_next, w_cur = window(g_next), window(g_cur)
        s_c = _dot_nt(kc_ref[0], q2)
        s_next[0:c, :] = s_c
        m8 = jnp.max(s_c.reshape(-1, 8, 2 * tile), axis=0)
        p = jnp.exp2(s_cur[0:c, :] - m_cur)
        l8 = jnp.sum(p.reshape(-1, 8, 2 * tile), axis=0)
        o2 = _dot(vc_ref[0], p.astype(BF16))
        for c0 in range(0, win, NB_CHUNK):
            s_w = _dot_nt(kl_ref[0, pl.ds(w_next.start + c0, NB_CHUNK), :], q2)
            s_w = jnp.concatenate([s_w[:, e * tile:(e + 1) * tile] + bias_ref[variant, e, c0:c0 + NB_CHUNK, :]
                                   for e in range(2)], axis=1)
            s_next[c + c0:c + c0 + NB_CHUNK, :] = s_w
            m8 = jnp.maximum(m8, jnp.max(s_w.reshape(-1, 8, 2 * tile), axis=0))
            p = jnp.exp2(s_cur[c + c0:c + c0 + NB_CHUNK, :] - m_cur)
            l8 = l8 + jnp.sum(p.reshape(-1, 8, 2 * tile), axis=0)
            o2 = o2 + _dot(vl_ref[0, :, pl.ds(w_cur.start + c0, NB_CHUNK)], p.astype(BF16))
        m_next[...] = jnp.max(m8, axis=0, keepdims=True)
        l = jnp.sum(l8, axis=0, keepdims=True)
        o_t = jnp.concatenate([o2[e * C_HEAD_DIM:(e + 1) * C_HEAD_DIM, e * tile:(e + 1) * tile]
                               / l[:, e * tile:(e + 1) * tile] for e in range(2)], axis=0)
        o_ref[0, tile_rows(g_cur), :] = o_t.T.astype(BF16)

    def pair(i, carry):
        g = 2 * i
        phase(g + 1, s_b, m_b, g, s_a, m_a)
        phase(jnp.minimum(g + 2, n_groups - 1), s_a, m_a, g + 1, s_b, m_b)
        return carry

    lax.fori_loop(0, n_groups // 2, pair, 0)


def _nb_attention(q, k_ctx, vt_ctx, k_lat=None, vt_lat=None, bias=None):
    b, tq, hw = q.shape
    pairs = hw // LANES
    has_window = k_lat is not None
    c = k_ctx.shape[1]
    if has_window:
        t = k_lat.shape[1]
        rows = t // GRID_W
        tile = NB_ROWS * GRID_W
        keys = c + NB_KEY_ROWS * GRID_W
    else:
        rows, tile, keys = 0, tq, c
    n_groups = tq // tile
    assert n_groups == 1 or n_groups % 2 == 0
    in_specs = [
        pl.BlockSpec((1, tq, LANES), lambda p, bi: (bi, 0, p)),
        pl.BlockSpec((1, c, LANES), lambda p, bi: (bi, 0, p)),
        pl.BlockSpec((1, LANES, c), lambda p, bi: (bi, p, 0)),
    ]
    args = [q, k_ctx, vt_ctx]
    if has_window:
        in_specs += [
            pl.BlockSpec((1, t, LANES), lambda p, bi: (bi, 0, p)),
            pl.BlockSpec((1, LANES, t), lambda p, bi: (bi, p, 0)),
            pl.BlockSpec((3, 2, NB_KEY_ROWS * GRID_W, tile), lambda p, bi: (0, p, 0, 0)),
        ]
        args += [k_lat, vt_lat, bias]
    return pl.pallas_call(
        functools.partial(_nb_attn_kernel, has_window=has_window, rows=rows, tile=tile, n_groups=n_groups),
        grid=(pairs, b),
        in_specs=in_specs,
        out_specs=pl.BlockSpec((1, tq, LANES), lambda p, bi: (bi, 0, p)),
        out_shape=jax.ShapeDtypeStruct((b, tq, hw), BF16),
        scratch_shapes=[pltpu.VMEM((keys, 2 * tile), F32), pltpu.VMEM((keys, 2 * tile), F32),
                        pltpu.VMEM((1, 2 * tile), F32), pltpu.VMEM((1, 2 * tile), F32)],
        compiler_params=_params("parallel", "parallel"),
        name="nb_attention",
    )(*args)


def _nb_bias_table(rpb, rows):
    heads = rpb.shape[0]
    tile_q = NB_ROWS * GRID_W
    return pl.pallas_call(
        functools.partial(_nb_bias_kernel, rows=rows),
        grid=(heads,),
        in_specs=[pl.BlockSpec(memory_space=pltpu.SMEM)],
        out_specs=pl.BlockSpec((3, 1, NB_KEY_ROWS * GRID_W, tile_q), lambda h: (0, h, 0, 0)),
        out_shape=jax.ShapeDtypeStruct((3, heads, NB_KEY_ROWS * GRID_W, tile_q), F32),
        compiler_params=_params("parallel"),
        name="nb_bias",
    )(rpb.reshape(-1))


def _nb_bias_kernel(rpb_ref, o_ref, *, rows):
    n_a, n_b = 2 * C_WIN_ROWS - 1, 2 * C_WIN_COLS - 1
    tile_q = NB_ROWS * GRID_W
    shape = (GRID_W, tile_q)
    kc = lax.broadcasted_iota(jnp.int32, shape, 0)
    lane = lax.broadcasted_iota(jnp.int32, shape, 1)
    qc = lane % GRID_W
    qi = lane // GRID_W
    c0 = jnp.clip(qc - C_WIN_COLS // 2, 0, GRID_W - C_WIN_COLS)
    col_ok = (kc >= c0) & (kc < c0 + C_WIN_COLS)
    dcol = kc - qc + (C_WIN_COLS - 1)
    base = pl.program_id(0) * (n_a * n_b)
    masked = jnp.full(shape, MASK_VALUE, F32)
    planes = []
    for a in range(n_a):
        acc = masked
        for bb in range(n_b):
            acc = jnp.where(dcol == bb, rpb_ref[base + a * n_b + bb], acc)
        planes.append(jnp.where(col_ok, acc * LOG2E, MASK_VALUE))
    groups = rows // NB_ROWS
    for v, g in enumerate((0, 1, groups - 1)):
        first_key_row = min(max(NB_ROWS * g - C_WIN_ROWS // 2, 0), rows - NB_KEY_ROWS)
        for j in range(NB_KEY_ROWS):
            kr = first_key_row + j
            blk = masked
            for i in range(NB_ROWS):
                qr = NB_ROWS * g + i
                r0 = min(max(qr - C_WIN_ROWS // 2, 0), rows - C_WIN_ROWS)
                if r0 <= kr < r0 + C_WIN_ROWS:
                    blk = jnp.where(qi == i, planes[kr - qr + C_WIN_ROWS - 1], blk)
            o_ref[v, 0, j * GRID_W:(j + 1) * GRID_W, :] = blk


def _rope_tables(rows, rot_dim):
    n = rot_dim // 4
    inv_freq = ROPE_THETA ** (-jnp.arange(n, dtype=F32) / n)
    t = jnp.arange(rows * GRID_W, dtype=jnp.int32)
    r = (t // GRID_W).astype(F32)
    col = (t % GRID_W).astype(F32)
    ang = jnp.concatenate([r[:, None] * inv_freq[None, :], col[:, None] * inv_freq[None, :]], axis=-1)
    cos, sin = jnp.cos(ang), jnp.sin(ang)
    reps = LANES // rot_dim
    return (jnp.tile(jnp.concatenate([cos, cos], axis=-1), (1, reps)),
            jnp.tile(jnp.concatenate([-sin, sin], axis=-1), (1, reps)))


def _mla_weights(w_in, q_norm, kv_norm, w_uq, w_ukv, w_o):
    rank = A_Q_RANK + A_KV_RANK
    k_r = w_in[:, rank:]
    uq = w_uq.reshape(A_Q_RANK, A_HEADS, A_NOPE + A_ROPE)
    ukv = w_ukv.reshape(A_KV_RANK, A_HEADS, A_NOPE + A_V)
    return {
        "w_in": jnp.concatenate([w_in[:, :rank], k_r, k_r], axis=1).astype(BF16),
        "q_norm": q_norm, "kv_norm": kv_norm,
        "w_uq": jnp.concatenate([uq[:, :, :A_NOPE].reshape(A_Q_RANK, -1),
                                 uq[:, :, A_NOPE:].reshape(A_Q_RANK, -1)], axis=1).astype(BF16),
        "w_ukv": jnp.concatenate([ukv[:, :, :A_NOPE].reshape(A_KV_RANK, -1),
                                  ukv[:, :, A_NOPE:].reshape(A_KV_RANK, -1)], axis=1).astype(BF16),
        "w_o": w_o.astype(BF16),
    }


def kernel(x, c, ctx, c_ctx, norm_g, w_mod, b_mod, ffn1_w13, ffn1_w2, ffn2_w13, ffn2_w2, a_w_in, a_q_norm, a_kv_norm, a_w_uq, a_w_ukv, a_w_o, b_w_qkv, b_q_norm, b_k_norm, b_w_o, c_w_qkv, c_rpb, c_w_o, final_norm_g):
    b, t, d = x.shape
    depth = w_mod.shape[0]
    rows = t // GRID_W
    assert b < MOD_ROWS and t % (NB_ROWS * GRID_W) == 0 and rows >= NB_KEY_ROWS + NB_ROWS

    c_rows = jnp.zeros((MOD_ROWS, d), F32).at[:b].set(c).at[b].set(c_ctx)
    mod_all = _modulation(c_rows, w_mod, b_mod).reshape(depth, MOD_ROWS, N_MOD, d)
    lat_row = lambda bi: bi
    ctx_row = lambda bi: b

    rope_a = _rope_tables(rows, A_ROPE)
    rope_b = _rope_tables(rows, B_HEAD_DIM)

    xc = ctx
    for i in range(depth):
        ctx_out = i < depth - 1
        last = i == depth - 1
        mod = mod_all[i]
        kind, j = i % N_MIXERS, i // N_MIXERS
        w13_1, w2_1 = ffn1_w13[i].astype(BF16), ffn1_w2[i].astype(BF16)
        w13_2, w2_2 = ffn2_w13[i].astype(BF16), ffn2_w2[i].astype(BF16)

        x = _ffn(x, mod, lat_row, norm_g[i, 0], w13_1, w2_1, k0=0)
        xc = _ffn(xc, mod, ctx_row, norm_g[i, 0], w13_1, w2_1, k0=0)

        if kind == 0:
            w = _mla_weights(a_w_in[j], a_q_norm[j], a_kv_norm[j], a_w_uq[j], a_w_ukv[j], a_w_o[j])
            q, kn, kr, vt = _mla_proj(x, mod, lat_row, norm_g[i, 1], w, rope_a)
            qc, knc, krc, vtc = _mla_proj(xc, mod, ctx_row, norm_g[i, 1], w, None)
            cfg = dict(heads=A_HEADS, hps=A_HEADS_PER_STEP, per_head=(True, False), share_kv=False,
                       key_chunk=MLA_KEY_CHUNK)
            o = _attention(q, [[knc, krc], [kn, kr]], [vtc, vt], **cfg)
            if ctx_out:
                oc = _attention(qc, [[knc, krc]], [vtc], **dict(cfg, hps=A_HEADS))
            w_o = w["w_o"]
        elif kind == 1:
            w = {"w_qkv": b_w_qkv[j].astype(BF16), "q_norm": b_q_norm[j], "k_norm": b_k_norm[j]}
            q, k, vt = _gqa_proj(x, mod, lat_row, norm_g[i, 1], w, rope_b)
            qc, kc, vtc = _gqa_proj(xc, mod, ctx_row, norm_g[i, 1], w, None)
            cfg = dict(heads=B_HEADS, hps=B_HEADS // B_KV_HEADS, per_head=(False,), share_kv=True,
                       key_chunk=GQA_KEY_CHUNK)
            o = _attention(q, [[kc], [k]], [vtc, vt], **cfg)
            if ctx_out:
                oc = _attention(qc, [[kc]], [vtc], **cfg)
            w_o = b_w_o[j].astype(BF16)
        else:
            w = {"w_qkv": c_w_qkv[j].astype(BF16)}
            q, k, vt = _nb_proj(x, mod, lat_row, norm_g[i, 1], w)
            qc, kc, vtc = _nb_proj(xc, mod, ctx_row, norm_g[i, 1], w)
            o = _nb_attention(q, kc, vtc, k, vt, _nb_bias_table(c_rpb[j], rows))
            if ctx_out:
                oc = _nb_attention(qc, kc, vtc)
            w_o = c_w_o[j].astype(BF16)

        x = _ffn(x, mod, lat_row, norm_g[i, 2], w13_2, w2_2, k0=6, attn=o, w_o=w_o,
                 final_g=final_norm_g if last else None)
        if ctx_out:
            xc = _ffn(xc, mod, ctx_row, norm_g[i, 2], w13_2, w2_2, k0=6, attn=oc, w_o=w_o)
    return x
```

```python
import functools

import jax
import jax.numpy as jnp
from jax import lax
from jax.experimental import pallas as pl
from jax.experimental.pallas import tpu as pltpu

F32 = jnp.float32
BF16 = jnp.bfloat16

GRID_W = 64
N_MIXERS = 3
N_MOD = 9
EPS = 1e-6
ROPE_THETA = 10000.0

A_HEADS, A_NOPE, A_ROPE, A_V = 8, 128, 64, 128
A_Q_RANK, A_KV_RANK = 384, 256
B_HEADS, B_KV_HEADS, B_HEAD_DIM = 8, 2, 128
C_HEADS, C_HEAD_DIM, C_WIN_ROWS, C_WIN_COLS = 16, 64, 8, 16

LANES = 128
MOD_ROWS = 16
MASK_VALUE = -1e30
LOG2E = 1.4426950408889634
VMEM_LIMIT = 56 * 1024 * 1024

ROW_TILE = 512
GQA_ROW_TILE = 256
A_HEADS_PER_STEP = 2
Q_TILE = 512
GQA_KEY_CHUNK = 512
MLA_KEY_CHUNK = 512
NB_CHUNK = 256
NB_ROWS = 4
NB_KEY_ROWS = 12


def _params(*sem):
    return pltpu.CompilerParams(dimension_semantics=sem, vmem_limit_bytes=VMEM_LIMIT)


def _const_spec(shape):
    nd = len(shape)
    return pl.BlockSpec(shape, lambda *_: (0,) * nd, pipeline_mode=pl.Buffered(1))


def _dot(a, b):
    return jnp.dot(a, b, preferred_element_type=F32)


def _dot_nt(a, b):
    return lax.dot_general(a, b, (((1,), (1,)), ((), ())), preferred_element_type=F32)


def _rms(x):
    return x * lax.rsqrt(jnp.mean(x * x, axis=-1, keepdims=True) + EPS)


def _modulated_norm(x, g, scale, shift):
    return _rms(x) * (g * (1.0 + scale)) + shift


def _silu(x):
    return x / (1.0 + jnp.exp(-x))


def _mod_kernel(c_ref, w_ref, b_ref, o_ref):
    sc = _silu(c_ref[...]).astype(BF16)
    o_ref[0] = _dot(sc, w_ref[0].astype(BF16)) + b_ref[0]


def _modulation(c_rows, w_mod, b_mod):
    depth, d, n = w_mod.shape
    tn = d
    return pl.pallas_call(
        _mod_kernel,
        grid=(depth, n // tn),
        in_specs=[
            pl.BlockSpec((MOD_ROWS, d), lambda i, j: (0, 0)),
            pl.BlockSpec((1, d, tn), lambda i, j: (i, 0, j)),
            pl.BlockSpec((1, 1, tn), lambda i, j: (i, 0, j)),
        ],
        out_specs=pl.BlockSpec((1, MOD_ROWS, tn), lambda i, j: (i, 0, j)),
        out_shape=jax.ShapeDtypeStruct((depth, MOD_ROWS, n), F32),
        compiler_params=_params("parallel", "parallel"),
        name="modulation",
    )(c_rows, w_mod, b_mod.reshape(depth, 1, n))


def _ffn_kernel(*refs, k0, d_ff, has_oproj, final_norm):
    it = iter(refs)
    x_ref, mod_ref, g_ref, w13_ref, w2_ref = (next(it) for _ in range(5))
    if has_oproj:
        a_ref, wo_ref = next(it), next(it)
    if final_norm:
        fg_ref = next(it)
    out_ref = next(it)

    x = x_ref[0]
    mod = mod_ref[0]
    if has_oproj:
        x = x + mod[5:6] * _dot(a_ref[0], wo_ref[...])
    h = _modulated_norm(x, g_ref[...], mod[k0 + 1:k0 + 2], mod[k0:k0 + 1]).astype(BF16)
    hgu = _dot(h, w13_ref[...])
    blk = 2 * LANES
    act = jnp.concatenate([(_silu(hgu[:, 2 * c0:2 * c0 + blk]) * hgu[:, 2 * c0 + blk:2 * c0 + 2 * blk]).astype(BF16)
                           for c0 in range(0, d_ff, blk)], axis=1)
    y = x + (0.5 * mod[k0 + 2:k0 + 3]) * _dot(act, w2_ref[...])
    if final_norm:
        y = _rms(y) * fg_ref[...]
    out_ref[0] = y


def _ffn(x, mod, mod_row, g, w13, w2, layer, *, k0, attn=None, w_o=None, final_g=None):
    b, t, d = x.shape
    d_ff = w2.shape[1]
    tm = min(ROW_TILE, t)
    row = lambda bi, ti: (bi, ti, 0)
    this_layer = lambda *_: (layer, 0, 0)
    in_specs = [
        pl.BlockSpec((1, tm, d), row),
        pl.BlockSpec((1, N_MOD, d), lambda bi, ti: (mod_row(bi), 0, 0)),
        _const_spec((1, d)),
        pl.BlockSpec((None,) + w13.shape[1:], this_layer, pipeline_mode=pl.Buffered(1)),
        pl.BlockSpec((None,) + w2.shape[1:], this_layer, pipeline_mode=pl.Buffered(1)),
    ]
    args = [x, mod, g.reshape(1, d), w13, w2]
    if attn is not None:
        in_specs += [pl.BlockSpec((1, tm, attn.shape[2]), row), _const_spec(w_o.shape)]
        args += [attn, w_o]
    if final_g is not None:
        in_specs.append(_const_spec((1, d)))
        args.append(final_g.reshape(1, d))
    return pl.pallas_call(
        functools.partial(_ffn_kernel, k0=k0, d_ff=d_ff, has_oproj=attn is not None,
                          final_norm=final_g is not None),
        grid=(b, t // tm),
        in_specs=in_specs,
        out_specs=pl.BlockSpec((1, tm, d), row),
        out_shape=jax.ShapeDtypeStruct((b, t, d), F32),
        compiler_params=_params("parallel", "parallel"),
        name="half_ffn",
    )(*args)


def _rope_pairs(x, cos, sin_signed, half):
    if 2 * half == LANES:
        rot = pltpu.roll(x, half, 1)
    else:
        lane = lax.broadcasted_iota(jnp.int32, x.shape, 1)
        first = (lane % (2 * half)) < half
        rot = jnp.where(first, pltpu.roll(x, LANES - half, 1), pltpu.roll(x, half, 1))
    return x * cos + rot * sin_signed


def _mla_proj_kernel(*refs, use_rope, sm_scale):
    it = iter(refs)
    x_ref, mod_ref, g_ref, w_in_ref, qg_ref, kvg_ref, w_uq_ref, w_ukv_ref = (next(it) for _ in range(8))
    if use_rope:
        cos_ref, sin_ref = next(it), next(it)
    q_ref, kn_ref, kr_ref, vt_ref = (next(it) for _ in range(4))

    mod = mod_ref[0]
    h = _modulated_norm(x_ref[0], g_ref[...], mod[4:5], mod[3:4]).astype(BF16)
    proj = _dot(h, w_in_ref[...])
    c_q = (_rms(proj[:, :A_Q_RANK]) * qg_ref[...]).astype(BF16)
    c_kv = (_rms(proj[:, A_Q_RANK:A_Q_RANK + A_KV_RANK]) * kvg_ref[...]).astype(BF16)
    k_r = proj[:, A_Q_RANK + A_KV_RANK:]
    q_all = _dot(c_q, w_uq_ref[...])
    kv = _dot(c_kv, w_ukv_ref[...])
    n_nope = A_HEADS * A_NOPE
    if use_rope:
        cos, sin = cos_ref[...], sin_ref[...]
        k_r = _rope_pairs(k_r, cos, sin, A_ROPE // 2)
    lane = lax.broadcasted_iota(jnp.int32, (1, LANES), 1)
    for j in range(A_HEADS // 2):
        qr = q_all[:, n_nope + j * LANES:n_nope + (j + 1) * LANES]
        if use_rope:
            qr = _rope_pairs(qr, cos, sin, A_ROPE // 2)
        qr = qr * sm_scale
        for e in range(2):
            hd = 2 * j + e
            keep = (lane < A_ROPE) if e == 0 else (lane >= A_ROPE)
            q_ref[0, :, 2 * hd * LANES:(2 * hd + 1) * LANES] = (
                q_all[:, hd * A_NOPE:(hd + 1) * A_NOPE] * sm_scale).astype(BF16)
            q_ref[0, :, (2 * hd + 1) * LANES:(2 * hd + 2) * LANES] = jnp.where(keep, qr, 0.0).astype(BF16)
    kn_ref[0] = kv[:, :n_nope].astype(BF16)
    kr_ref[0] = k_r.astype(BF16)
    vt_ref[0] = kv[:, n_nope:].T.astype(BF16)


def _mla_proj(x, mod, mod_row, g, w, rope):
    b, t, d = x.shape
    tm = min(ROW_TILE, t)
    row = lambda bi, ti: (bi, ti, 0)
    in_specs = [
        pl.BlockSpec((1, tm, d), row),
        pl.BlockSpec((1, N_MOD, d), lambda bi, ti: (mod_row(bi), 0, 0)),
        _const_spec((1, d)),
        _const_spec(w["w_in"].shape), _const_spec((1, A_Q_RANK)), _const_spec((1, A_KV_RANK)),
        _const_spec(w["w_uq"].shape), _const_spec(w["w_ukv"].shape),
    ]
    args = [x, mod, g.reshape(1, d), w["w_in"], w["q_norm"].reshape(1, -1), w["kv_norm"].reshape(1, -1),
            w["w_uq"], w["w_ukv"]]
    if rope is not None:
        in_specs += [pl.BlockSpec((tm, LANES), lambda bi, ti: (ti, 0))] * 2
        args += list(rope)
    hv = A_HEADS * A_V
    return pl.pallas_call(
        functools.partial(_mla_proj_kernel, use_rope=rope is not None,
                          sm_scale=float((A_NOPE + A_ROPE) ** -0.5 * LOG2E)),
        grid=(b, t // tm),
        in_specs=in_specs,
        out_specs=[
            pl.BlockSpec((1, tm, 2 * LANES * A_HEADS), row),
            pl.BlockSpec((1, tm, A_HEADS * A_NOPE), row),
            pl.BlockSpec((1, tm, LANES), row),
            pl.BlockSpec((1, hv, tm), lambda bi, ti: (bi, 0, ti)),
        ],
        out_shape=[
            jax.ShapeDtypeStruct((b, t, 2 * LANES * A_HEADS), BF16),
            jax.ShapeDtypeStruct((b, t, A_HEADS * A_NOPE), BF16),
            jax.ShapeDtypeStruct((b, t, LANES), BF16),
            jax.ShapeDtypeStruct((b, hv, t), BF16),
        ],
        compiler_params=_params("parallel", "parallel"),
        name="mla_proj",
    )(*args)


def _gqa_proj_kernel(*refs, use_rope, sm_scale):
    it = iter(refs)
    x_ref, mod_ref, g_ref, w_ref, qg_ref, kg_ref = (next(it) for _ in range(6))
    if use_rope:
        cos_ref, sin_ref = next(it), next(it)
    q_ref, k_ref, vt_ref = (next(it) for _ in range(3))

    mod = mod_ref[0]
    h = _modulated_norm(x_ref[0], g_ref[...], mod[4:5], mod[3:4]).astype(BF16)
    proj = _dot(h, w_ref[...])
    qw = B_HEADS * B_HEAD_DIM
    kw = B_KV_HEADS * B_HEAD_DIM
    if use_rope:
        cos, sin = cos_ref[...], sin_ref[...]

    def head(col, gain, scale):
        y = _rms(proj[:, col:col + B_HEAD_DIM]) * gain
        if use_rope:
            y = _rope_pairs(y, cos, sin, B_HEAD_DIM // 2)
        return (y * scale).astype(BF16) if scale != 1.0 else y.astype(BF16)

    for hd in range(B_HEADS):
        q_ref[0, :, hd * B_HEAD_DIM:(hd + 1) * B_HEAD_DIM] = head(hd * B_HEAD_DIM, qg_ref[...], sm_scale)
    for hd in range(B_KV_HEADS):
        k_ref[0, :, hd * B_HEAD_DIM:(hd + 1) * B_HEAD_DIM] = head(qw + hd * B_HEAD_DIM, kg_ref[...], 1.0)
    vt_ref[0] = proj[:, qw + kw:].T.astype(BF16)


def _gqa_proj(x, mod, mod_row, g, w, rope):
    b, t, d = x.shape
    tm = min(GQA_ROW_TILE, t)
    row = lambda bi, ti: (bi, ti, 0)
    qw, kw = B_HEADS * B_HEAD_DIM, B_KV_HEADS * B_HEAD_DIM
    in_specs = [
        pl.BlockSpec((1, tm, d), row),
        pl.BlockSpec((1, N_MOD, d), lambda bi, ti: (mod_row(bi), 0, 0)),
        _const_spec((1, d)),
        _const_spec(w["w_qkv"].shape), _const_spec((1, B_HEAD_DIM)), _const_spec((1, B_HEAD_DIM)),
    ]
    args = [x, mod, g.reshape(1, d), w["w_qkv"], w["q_norm"].reshape(1, -1), w["k_norm"].reshape(1, -1)]
    if rope is not None:
        in_specs += [pl.BlockSpec((tm, LANES), lambda bi, ti: (ti, 0))] * 2
        args += list(rope)
    return pl.pallas_call(
        functools.partial(_gqa_proj_kernel, use_rope=rope is not None, sm_scale=float(B_HEAD_DIM ** -0.5 * LOG2E)),
        grid=(b, t // tm),
        in_specs=in_specs,
        out_specs=[
            pl.BlockSpec((1, tm, qw), row),
            pl.BlockSpec((1, tm, kw), row),
            pl.BlockSpec((1, kw, tm), lambda bi, ti: (bi, 0, ti)),
        ],
        out_shape=[
            jax.ShapeDtypeStruct((b, t, qw), BF16),
            jax.ShapeDtypeStruct((b, t, kw), BF16),
            jax.ShapeDtypeStruct((b, kw, t), BF16),
        ],
        compiler_params=_params("parallel", "parallel"),
        name="gqa_proj",
    )(*args)


def _nb_proj_kernel(x_ref, mod_ref, g_ref, w_ref, q_ref, k_ref, vt_ref, *, sm_scale):
    mod = mod_ref[0]
    h = _modulated_norm(x_ref[0], g_ref[...], mod[4:5], mod[3:4]).astype(BF16)
    proj = _dot(h, w_ref[...])
    hd = C_HEADS * C_HEAD_DIM
    q_ref[0] = (proj[:, :hd] * sm_scale).astype(BF16)
    k_ref[0] = proj[:, hd:2 * hd].astype(BF16)
    vt_ref[0] = proj[:, 2 * hd:].T.astype(BF16)


def _nb_proj(x, mod, mod_row, g, w):
    b, t, d = x.shape
    tm = min(ROW_TILE, t)
    row = lambda bi, ti: (bi, ti, 0)
    hd = C_HEADS * C_HEAD_DIM
    return pl.pallas_call(
        functools.partial(_nb_proj_kernel, sm_scale=float(C_HEAD_DIM ** -0.5 * LOG2E)),
        grid=(b, t // tm),
        in_specs=[
            pl.BlockSpec((1, tm, d), row),
            pl.BlockSpec((1, N_MOD, d), lambda bi, ti: (mod_row(bi), 0, 0)),
            _const_spec((1, d)),
            _const_spec(w["w_qkv"].shape),
        ],
        out_specs=[
            pl.BlockSpec((1, tm, hd), row),
            pl.BlockSpec((1, tm, hd), row),
            pl.BlockSpec((1, hd, tm), lambda bi, ti: (bi, 0, ti)),
        ],
        out_shape=[
            jax.ShapeDtypeStruct((b, t, hd), BF16),
            jax.ShapeDtypeStruct((b, t, hd), BF16),
            jax.ShapeDtypeStruct((b, hd, t), BF16),
        ],
        compiler_params=_params("parallel", "parallel"),
        name="nb_proj",
    )(x, mod, g.reshape(1, d), w["w_qkv"])


def _attn_kernel(*refs, key_rows, per_head, n_kv, hps, dq, q_tile, n_tiles, key_chunk):
    n_groups, k_pieces = len(key_rows), len(per_head)
    it = iter(refs)
    q_ref = next(it)
    k_refs = [[next(it) for _ in range(k_pieces)] for _ in range(n_groups)]
    v_refs = [next(it) for _ in range(n_groups)]
    o_ref = next(it)
    k_scr, v_scr, s_a, s_b, m_a, m_b = (next(it) for _ in range(6))

    r0 = 0
    for gi, rows in enumerate(key_rows):
        for j in range(n_kv):
            for pi in range(k_pieces):
                lanes = slice(j * LANES, (j + 1) * LANES) if per_head[pi] else slice(0, LANES)
                k_scr[j, r0:r0 + rows, pi * LANES:(pi + 1) * LANES] = k_refs[gi][pi][0, :, lanes]
            v_scr[j, :, r0:r0 + rows] = v_refs[gi][0, j * LANES:(j + 1) * LANES, :]
        r0 += rows

    def rows_of(tile):
        return pl.ds(pl.multiple_of(tile * q_tile, q_tile), q_tile)

    def kv_of(head):
        return head if n_kv > 1 else 0

    def scores(tile, head, s_buf, m_buf):
        q = q_ref[0, rows_of(tile), head * dq:(head + 1) * dq]
        s_t = _dot_nt(k_scr[kv_of(head)], q)
        s_buf[...] = s_t
        m_buf[...] = jnp.max(s_t, axis=0, keepdims=True)

    def softmax_pv(tile, head, s_buf, m_buf):
        p = jnp.exp2(s_buf[...] - m_buf[...])
        l = jnp.sum(p, axis=0, keepdims=True)
        o_t = _dot(v_scr[kv_of(head)], p.astype(BF16)) / l
        o_ref[0, rows_of(tile), head * LANES:(head + 1) * LANES] = o_t.T.astype(BF16)

    scores(0, 0, s_a, m_a)
    if hps * n_tiles == 1:
        softmax_pv(0, 0, s_a, m_a)
        return

    s_rows = s_a.shape[0]
    bufs = ((s_a, m_a), (s_b, m_b))

    def phase(tile_n, head_n, s_next, m_next, tile_c, head_c, s_cur, m_cur_buf):
        if not key_chunk:
            scores(tile_n, head_n, s_next, m_next)
            softmax_pv(tile_c, head_c, s_cur, m_cur_buf)
            return
        q = q_ref[0, rows_of(tile_n), head_n * dq:(head_n + 1) * dq]
        m_cur = m_cur_buf[...]
        m8 = l8 = acc = None
        for c0 in range(0, s_rows, key_chunk):
            rows = slice(c0, min(c0 + key_chunk, s_rows))
            s_c = _dot_nt(k_scr[kv_of(head_n), rows, :], q)
            s_next[rows, :] = s_c
            mc = jnp.max(s_c.reshape(-1, 8, q_tile), axis=0)
            m8 = mc if m8 is None else jnp.maximum(m8, mc)
            p = jnp.exp2(s_cur[rows, :] - m_cur)
            lc = jnp.sum(p.reshape(-1, 8, q_tile), axis=0)
            l8 = lc if l8 is None else l8 + lc
            a = _dot(v_scr[kv_of(head_c), :, rows], p.astype(BF16))
            acc = a if acc is None else acc + a
        m_next[...] = jnp.max(m8, axis=0, keepdims=True)
        l = jnp.sum(l8, axis=0, keepdims=True)
        o_ref[0, rows_of(tile_c), head_c * LANES:(head_c + 1) * LANES] = (acc / l).T.astype(BF16)

    def one_tile(tile, carry):
        for head in range(hps):
            if head + 1 < hps:
                nxt = (tile, head + 1)
            else:
                nxt = (jnp.minimum(tile + 1, n_tiles - 1), 0)
            phase(*nxt, *bufs[(head + 1) % 2], tile, head, *bufs[head % 2])
        return carry

    assert hps % 2 == 0
    lax.fori_loop(0, n_tiles, one_tile, 0)


def _attention(q, k_groups, v_groups, *, heads, hps, per_head, share_kv, key_chunk):
    b, tq, qw = q.shape
    dq = qw // heads
    q_tile = min(Q_TILE, tq)
    n_tiles = tq // q_tile
    n_kv = 1 if share_kv else hps
    assert heads % hps == 0 and (hps * n_tiles == 1 or hps % 2 == 0)
    key_rows = tuple(g[0].shape[1] for g in k_groups)
    k_pieces = len(per_head)
    s_total = sum(key_rows)
    in_specs = [pl.BlockSpec((1, tq, hps * dq), lambda bi, hg: (bi, 0, hg))]
    args = [q]
    for grp in k_groups:
        for pi, arr in enumerate(grp):
            width = n_kv * LANES if per_head[pi] else LANES
            moves = per_head[pi] or share_kv
            in_specs.append(pl.BlockSpec((1, arr.shape[1], width),
                                         lambda bi, hg, moves=moves: (bi, 0, hg if moves else 0)))
            args.append(arr)
    for arr in v_groups:
        in_specs.append(pl.BlockSpec((1, n_kv * LANES, arr.shape[2]), lambda bi, hg: (bi, hg, 0)))
        args.append(arr)
    return pl.pallas_call(
        functools.partial(_attn_kernel, key_rows=key_rows, per_head=tuple(per_head), n_kv=n_kv, hps=hps,
                          dq=dq, q_tile=q_tile, n_tiles=n_tiles, key_chunk=key_chunk),
        grid=(b, heads // hps),
        in_specs=in_specs,
        out_specs=pl.BlockSpec((1, tq, hps * LANES), lambda bi, hg: (bi, 0, hg)),
        out_shape=jax.ShapeDtypeStruct((b, tq, heads * LANES), BF16),
        scratch_shapes=[
            pltpu.VMEM((n_kv, s_total, k_pieces * LANES), BF16), pltpu.VMEM((n_kv, LANES, s_total), BF16),
            pltpu.VMEM((s_total, q_tile), F32), pltpu.VMEM((s_total, q_tile), F32),
            pltpu.VMEM((1, q_tile), F32), pltpu.VMEM((1, q_tile), F32),
        ],
        compiler_params=_params("parallel", "parallel"),
        name="attention",
    )(*args)


def _nb_attn_kernel(*refs, has_window, rows, tile, n_groups):
    it = iter(refs)
    q_ref, kc_ref, vc_ref = next(it), next(it), next(it)
    if has_window:
        kl_ref, vl_ref, bias_ref = next(it), next(it), next(it)
    o_ref = next(it)
    s_a, s_b, m_a, m_b = (next(it) for _ in range(4))
    c = kc_ref.shape[1]
    win = NB_KEY_ROWS * GRID_W
    lane = lax.broadcasted_iota(jnp.int32, (1, LANES), 1)

    def tile_rows(g):
        return pl.ds(pl.multiple_of(g * tile, tile), tile)

    def window(g):
        first_row = jnp.clip(NB_ROWS * g - C_WIN_ROWS // 2, 0, rows - NB_KEY_ROWS)
        return pl.ds(pl.multiple_of(first_row * GRID_W, 2 * LANES), win)

    def scores(g, s_buf, m_buf):
        q = q_ref[0, tile_rows(g), :]
        zero = jnp.zeros_like(q)
        q2 = jnp.concatenate([jnp.where(lane < C_HEAD_DIM, q, zero), jnp.where(lane >= C_HEAD_DIM, q, zero)],
                             axis=0)
        s_c = _dot_nt(kc_ref[0], q2)
        s_buf[0:c, :] = s_c
        m = jnp.max(s_c, axis=0, keepdims=True)
        if has_window:
            variant = jnp.where(g == 0, 0, jnp.where(g == n_groups - 1, 2, 1))
            s_w = _dot_nt(kl_ref[0, window(g), :], q2)
            for e in range(2):
                s_e = s_w[:, e * tile:(e + 1) * tile] + bias_ref[variant, e]
                s_buf[c:, e * tile:(e + 1) * tile] = s_e
                m_e = jnp.maximum(m[:, e * tile:(e + 1) * tile], jnp.max(s_e, axis=0, keepdims=True))
                m_buf[:, e * tile:(e + 1) * tile] = m_e
        else:
            m_buf[...] = m

    def softmax_pv(g, s_buf, m_buf):
        p = jnp.exp2(s_buf[...] - m_buf[...])
        l = jnp.sum(p, axis=0, keepdims=True)
        pb = p.astype(BF16)
        o2 = _dot(vc_ref[0], pb[0:c])
        if has_window:
            o2 = o2 + _dot(vl_ref[0, :, window(g)], pb[c:])
        o_t = jnp.concatenate([o2[e * C_HEAD_DIM:(e + 1) * C_HEAD_DIM, e * tile:(e + 1) * tile]
                               / l[:, e * tile:(e + 1) * tile] for e in range(2)], axis=0)
        o_ref[0, tile_rows(g), :] = o_t.T.astype(BF16)

    scores(0, s_a, m_a)
    if n_groups == 1:
        softmax_pv(0, s_a, m_a)
        return

    def phase(g_next, s_next, m_next, g_cur, s_cur, m_cur_buf):
        q = q_ref[0, tile_rows(g_next), :]
        zero = jnp.zeros_like(q)
        q2 = jnp.concatenate([jnp.where(lane < C_HEAD_DIM, q, zero), jnp.where(lane >= C_HEAD_DIM, q, zero)],
                             axis=0)
        m_cur = m_cur_buf[...]
        variant = jnp.where(g_next == 0, 0, jnp.where(g_next == n_groups - 1, 2, 1))
        w_next, w_cur = window(g_next), window(g_cur)
        s_c = _dot_nt(kc_ref[0], q2)
        s_next[0:c, :] = s_c
        m8 = jnp.max(s_c.reshape(-1, 8, 2 * tile), axis=0)
        p = jnp.exp2(s_cur[0:c, :] - m_cur)
        l8 = jnp.sum(p.reshape(-1, 8, 2 * tile), axis=0)
        o2 = _dot(vc_ref[0], p.astype(BF16))
        for c0 in range(0, win, NB_CHUNK):
            s_w = _dot_nt(kl_ref[0, pl.ds(w_next.start + c0, NB_CHUNK), :], q2)
            s_w = jnp.concatenate([s_w[:, e * tile:(e + 1) * tile] + bias_ref[variant, e, c0:c0 + NB_CHUNK, :]
                                   for e in range(2)], axis=1)
            s_next[c + c0:c + c0 + NB_CHUNK, :] = s_w
            m8 = jnp.maximum(m8, jnp.max(s_w.reshape(-1, 8, 2 * tile), axis=0))
            p = jnp.exp2(s_cur[c + c0:c + c0 + NB_CHUNK, :] - m_cur)
            l8 = l8 + jnp.sum(p.reshape(-1, 8, 2 * tile), axis=0)
            o2 = o2 + _dot(vl_ref[0, :, pl.ds(w_cur.start + c0, NB_CHUNK)], p.astype(BF16))
        m_next[...] = jnp.max(m8, axis=0, keepdims=True)
        l = jnp.sum(l8, axis=0, keepdims=True)
        o_t = jnp.concatenate([o2[e * C_HEAD_DIM:(e + 1) * C_HEAD_DIM, e * tile:(e + 1) * tile]
                               / l[:, e * tile:(e + 1) * tile] for e in range(2)], axis=0)
        o_ref[0, tile_rows(g_cur), :] = o_t.T.astype(BF16)

    def pair(i, carry):
        g = 2 * i
        phase(g + 1, s_b, m_b, g, s_a, m_a)
        phase(jnp.minimum(g + 2, n_groups - 1), s_a, m_a, g + 1, s_b, m_b)
        return carry

    lax.fori_loop(0, n_groups // 2, pair, 0)


def _nb_attention(q, k_ctx, vt_ctx, k_lat=None, vt_lat=None, bias=None):
    b, tq, hw = q.shape
    pairs = hw // LANES
    has_window = k_lat is not None
    c = k_ctx.shape[1]
    if has_window:
        t = k_lat.shape[1]
        rows = t // GRID_W
        tile = NB_ROWS * GRID_W
        keys = c + NB_KEY_ROWS * GRID_W
    else:
        rows, tile, keys = 0, tq, c
    n_groups = tq // tile
    assert n_groups == 1 or n_groups % 2 == 0
    in_specs = [
        pl.BlockSpec((1, tq, LANES), lambda p, bi: (bi, 0, p)),
        pl.BlockSpec((1, c, LANES), lambda p, bi: (bi, 0, p)),
        pl.BlockSpec((1, LANES, c), lambda p, bi: (bi, p, 0)),
    ]
    args = [q, k_ctx, vt_ctx]
    if has_window:
        in_specs += [
            pl.BlockSpec((1, t, LANES), lambda p, bi: (bi, 0, p)),
            pl.BlockSpec((1, LANES, t), lambda p, bi: (bi, p, 0)),
            pl.BlockSpec((3, 2, NB_KEY_ROWS * GRID_W, tile), lambda p, bi: (0, p, 0, 0)),
        ]
        args += [k_lat, vt_lat, bias]
    return pl.pallas_call(
        functools.partial(_nb_attn_kernel, has_window=has_window, rows=rows, tile=tile, n_groups=n_groups),
        grid=(pairs, b),
        in_specs=in_specs,
        out_specs=pl.BlockSpec((1, tq, LANES), lambda p, bi: (bi, 0, p)),
        out_shape=jax.ShapeDtypeStruct((b, tq, hw), BF16),
        scratch_shapes=[pltpu.VMEM((keys, 2 * tile), F32), pltpu.VMEM((keys, 2 * tile), F32),
                        pltpu.VMEM((1, 2 * tile), F32), pltpu.VMEM((1, 2 * tile), F32)],
        compiler_params=_params("parallel", "parallel"),
        name="nb_attention",
    )(*args)


def _nb_bias_table(rpb, rows):
    heads = rpb.shape[0]
    tile_q = NB_ROWS * GRID_W
    return pl.pallas_call(
        functools.partial(_nb_bias_kernel, rows=rows),
        grid=(heads,),
        in_specs=[pl.BlockSpec(memory_space=pltpu.SMEM)],
        out_specs=pl.BlockSpec((3, 1, NB_KEY_ROWS * GRID_W, tile_q), lambda h: (0, h, 0, 0)),
        out_shape=jax.ShapeDtypeStruct((3, heads, NB_KEY_ROWS * GRID_W, tile_q), F32),
        compiler_params=_params("parallel"),
        name="nb_bias",
    )(rpb.reshape(-1))


def _nb_bias_kernel(rpb_ref, o_ref, *, rows):
    n_a, n_b = 2 * C_WIN_ROWS - 1, 2 * C_WIN_COLS - 1
    tile_q = NB_ROWS * GRID_W
    shape = (GRID_W, tile_q)
    kc = lax.broadcasted_iota(jnp.int32, shape, 0)
    lane = lax.broadcasted_iota(jnp.int32, shape, 1)
    qc = lane % GRID_W
    qi = lane // GRID_W
    c0 = jnp.clip(qc - C_WIN_COLS // 2, 0, GRID_W - C_WIN_COLS)
    col_ok = (kc >= c0) & (kc < c0 + C_WIN_COLS)
    dcol = kc - qc + (C_WIN_COLS - 1)
    base = pl.program_id(0) * (n_a * n_b)
    masked = jnp.full(shape, MASK_VALUE, F32)
    planes = []
    for a in range(n_a):
        acc = masked
        for bb in range(n_b):
            acc = jnp.where(dcol == bb, rpb_ref[base + a * n_b + bb], acc)
        planes.append(jnp.where(col_ok, acc * LOG2E, MASK_VALUE))
    groups = rows // NB_ROWS
    for v, g in enumerate((0, 1, groups - 1)):
        first_key_row = min(max(NB_ROWS * g - C_WIN_ROWS // 2, 0), rows - NB_KEY_ROWS)
        for j in range(NB_KEY_ROWS):
            kr = first_key_row + j
            blk = masked
            for i in range(NB_ROWS):
                qr = NB_ROWS * g + i
                r0 = min(max(qr - C_WIN_ROWS // 2, 0), rows - C_WIN_ROWS)
                if r0 <= kr < r0 + C_WIN_ROWS:
                    blk = jnp.where(qi == i, planes[kr - qr + C_WIN_ROWS - 1], blk)
            o_ref[v, 0, j * GRID_W:(j + 1) * GRID_W, :] = blk


def _rope_tables(rows, rot_dim):
    n = rot_dim // 4
    inv_freq = ROPE_THETA ** (-jnp.arange(n, dtype=F32) / n)
    t = jnp.arange(rows * GRID_W, dtype=jnp.int32)
    r = (t // GRID_W).astype(F32)
    col = (t % GRID_W).astype(F32)
    ang = jnp.concatenate([r[:, None] * inv_freq[None, :], col[:, None] * inv_freq[None, :]], axis=-1)
    cos, sin = jnp.cos(ang), jnp.sin(ang)
    reps = LANES // rot_dim
    return (jnp.tile(jnp.concatenate([cos, cos], axis=-1), (1, reps)),
            jnp.tile(jnp.concatenate([-sin, sin], axis=-1), (1, reps)))


def _mla_weights(w_in, q_norm, kv_norm, w_uq, w_ukv, w_o):
    rank = A_Q_RANK + A_KV_RANK
    k_r = w_in[:, rank:]
    uq = w_uq.reshape(A_Q_RANK, A_HEADS, A_NOPE + A_ROPE)
    ukv = w_ukv.reshape(A_KV_RANK, A_HEADS, A_NOPE + A_V)
    return {
        "w_in": jnp.concatenate([w_in[:, :rank], k_r, k_r], axis=1).astype(BF16),
        "q_norm": q_norm, "kv_norm": kv_norm,
        "w_uq": jnp.concatenate([uq[:, :, :A_NOPE].reshape(A_Q_RANK, -1),
                                 uq[:, :, A_NOPE:].reshape(A_Q_RANK, -1)], axis=1).astype(BF16),
        "w_ukv": jnp.concatenate([ukv[:, :, :A_NOPE].reshape(A_KV_RANK, -1),
                                  ukv[:, :, A_NOPE:].reshape(A_KV_RANK, -1)], axis=1).astype(BF16),
        "w_o": w_o.astype(BF16),
    }


def _interleave_gate_up(w13):
    n_layers, d, n = w13.shape
    blk = 2 * LANES
    return w13.reshape(n_layers, d, 2, n // 2 // blk, blk).transpose(0, 1, 3, 2, 4).reshape(n_layers, d, n)


def kernel(x, c, ctx, c_ctx, norm_g, w_mod, b_mod, ffn1_w13, ffn1_w2, ffn2_w13, ffn2_w2, a_w_in, a_q_norm, a_kv_norm, a_w_uq, a_w_ukv, a_w_o, b_w_qkv, b_q_norm, b_k_norm, b_w_o, c_w_qkv, c_rpb, c_w_o, final_norm_g):
    b, t, d = x.shape
    depth = w_mod.shape[0]
    rows = t // GRID_W
    assert b < MOD_ROWS and t % (NB_ROWS * GRID_W) == 0 and rows >= NB_KEY_ROWS + NB_ROWS

    c_rows = jnp.zeros((MOD_ROWS, d), F32).at[:b].set(c).at[b].set(c_ctx)
    mod_all = _modulation(c_rows, w_mod, b_mod).reshape(depth, MOD_ROWS, N_MOD, d)
    lat_row = lambda bi: bi
    ctx_row = lambda bi: b

    rope_a = _rope_tables(rows, A_ROPE)
    rope_b = _rope_tables(rows, B_HEAD_DIM)

    w13_1, w2_1 = _interleave_gate_up(ffn1_w13).astype(BF16), ffn1_w2.astype(BF16)
    w13_2, w2_2 = _interleave_gate_up(ffn2_w13).astype(BF16), ffn2_w2.astype(BF16)

    xc = ctx
    for i in range(depth):
        ctx_out = i < depth - 1
        last = i == depth - 1
        mod = mod_all[i]
        kind, j = i % N_MIXERS, i // N_MIXERS
        x = _ffn(x, mod, lat_row, norm_g[i, 0], w13_1, w2_1, i, k0=0)
        xc = _ffn(xc, mod, ctx_row, norm_g[i, 0], w13_1, w2_1, i, k0=0)

        if kind == 0:
            w = _mla_weights(a_w_in[j], a_q_norm[j], a_kv_norm[j], a_w_uq[j], a_w_ukv[j], a_w_o[j])
            q, kn, kr, vt = _mla_proj(x, mod, lat_row, norm_g[i, 1], w, rope_a)
            qc, knc, krc, vtc = _mla_proj(xc, mod, ctx_row, norm_g[i, 1], w, None)
            cfg = dict(heads=A_HEADS, hps=A_HEADS_PER_STEP, per_head=(True, False), share_kv=False,
                       key_chunk=MLA_KEY_CHUNK)
            o = _attention(q, [[knc, krc], [kn, kr]], [vtc, vt], **cfg)
            if ctx_out:
                oc = _attention(qc, [[knc, krc]], [vtc], **dict(cfg, hps=A_HEADS))
            w_o = w["w_o"]
        elif kind == 1:
            w = {"w_qkv": b_w_qkv[j].astype(BF16), "q_norm": b_q_norm[j], "k_norm": b_k_norm[j]}
            q, k, vt = _gqa_proj(x, mod, lat_row, norm_g[i, 1], w, rope_b)
            qc, kc, vtc = _gqa_proj(xc, mod, ctx_row, norm_g[i, 1], w, None)
            cfg = dict(heads=B_HEADS, hps=B_HEADS // B_KV_HEADS, per_head=(False,), share_kv=True,
                       key_chunk=GQA_KEY_CHUNK)
            o = _attention(q, [[kc], [k]], [vtc, vt], **cfg)
            if ctx_out:
                oc = _attention(qc, [[kc]], [vtc], **cfg)
            w_o = b_w_o[j].astype(BF16)
        else:
            w = {"w_qkv": c_w_qkv[j].astype(BF16)}
            q, k, vt = _nb_proj(x, mod, lat_row, norm_g[i, 1], w)
            qc, kc, vtc = _nb_proj(xc, mod, ctx_row, norm_g[i, 1], w)
            o = _nb_attention(q, kc, vtc, k, vt, _nb_bias_table(c_rpb[j], rows))
            if ctx_out:
                oc = _nb_attention(qc, kc, vtc)
            w_o = c_w_o[j].astype(BF16)

        x = _ffn(x, mod, lat_row, norm_g[i, 2], w13_2, w2_2, i, k0=6, attn=o, w_o=w_o,
                 final_g=final_norm_g if last else None)
        if ctx_out:
            xc = _ffn(xc, mod, ctx_row, norm_g[i, 2], w13_2, w2_2, i, k0=6, attn=oc, w_o=w_o)
    return x
```

```python
import functools

import jax
import jax.numpy as jnp
from jax import lax
from jax.experimental import pallas as pl
from jax.experimental.pallas import tpu as pltpu

F32 = jnp.float32
BF16 = jnp.bfloat16

GRID_W = 64
N_MIXERS = 3
N_MOD = 9
EPS = 1e-6
ROPE_THETA = 10000.0

A_HEADS, A_NOPE, A_ROPE, A_V = 8, 128, 64, 128
A_Q_RANK, A_KV_RANK = 384, 256
B_HEADS, B_KV_HEADS, B_HEAD_DIM = 8, 2, 128
C_HEADS, C_HEAD_DIM, C_WIN_ROWS, C_WIN_COLS = 16, 64, 8, 16

LANES = 128
MOD_ROWS = 16
MASK_VALUE = -1e30
LOG2E = 1.4426950408889634
VMEM_LIMIT = 56 * 1024 * 1024

ROW_TILE = 512
GQA_ROW_TILE = 256
A_HEADS_PER_STEP = 2
Q_TILE = 512
GQA_KEY_CHUNK = 512
MLA_KEY_CHUNK = 0
NB_CHUNK = 256
NB_ROWS = 4
NB_KEY_ROWS = 12


def _params(*sem):
    return pltpu.CompilerParams(dimension_semantics=sem, vmem_limit_bytes=VMEM_LIMIT)


def _const_spec(shape):
    nd = len(shape)
    return pl.BlockSpec(shape, lambda *_: (0,) * nd, pipeline_mode=pl.Buffered(1))


def _dot(a, b):
    return jnp.dot(a, b, preferred_element_type=F32)


def _dot_nt(a, b):
    return lax.dot_general(a, b, (((1,), (1,)), ((), ())), preferred_element_type=F32)


def _rms(x):
    return x * lax.rsqrt(jnp.mean(x * x, axis=-1, keepdims=True) + EPS)


def _modulated_norm(x, g, scale, shift):
    return _rms(x) * (g * (1.0 + scale)) + shift


def _silu(x):
    return x / (1.0 + jnp.exp(-x))


def _mod_kernel(c_ref, w_ref, b_ref, o_ref):
    sc = _silu(c_ref[...]).astype(BF16)
    o_ref[0] = _dot(sc, w_ref[0].astype(BF16)) + b_ref[0]


def _modulation(c_rows, w_mod, b_mod):
    depth, d, n = w_mod.shape
    tn = d
    return pl.pallas_call(
        _mod_kernel,
        grid=(depth, n // tn),
        in_specs=[
            pl.BlockSpec((MOD_ROWS, d), lambda i, j: (0, 0)),
            pl.BlockSpec((1, d, tn), lambda i, j: (i, 0, j)),
            pl.BlockSpec((1, 1, tn), lambda i, j: (i, 0, j)),
        ],
        out_specs=pl.BlockSpec((1, MOD_ROWS, tn), lambda i, j: (i, 0, j)),
        out_shape=jax.ShapeDtypeStruct((depth, MOD_ROWS, n), F32),
        compiler_params=_params("parallel", "parallel"),
        name="modulation",
    )(c_rows, w_mod, b_mod.reshape(depth, 1, n))


def _ffn_kernel(*refs, k0, d_ff, has_oproj, final_norm):
    it = iter(refs)
    x_ref, mod_ref, g_ref, w13_ref, w2_ref = (next(it) for _ in range(5))
    if has_oproj:
        a_ref, wo_ref = next(it), next(it)
    if final_norm:
        fg_ref = next(it)
    out_ref = next(it)

    x = x_ref[0]
    mod = mod_ref[0]
    if has_oproj:
        x = x + mod[5:6] * _dot(a_ref[0], wo_ref[...])
    h = _modulated_norm(x, g_ref[...], mod[k0 + 1:k0 + 2], mod[k0:k0 + 1]).astype(BF16)
    hgu = _dot(h, w13_ref[...])
    act = (_silu(hgu[:, :d_ff]) * hgu[:, d_ff:]).astype(BF16)
    y = x + (0.5 * mod[k0 + 2:k0 + 3]) * _dot(act, w2_ref[...])
    if final_norm:
        y = _rms(y) * fg_ref[...]
    out_ref[0] = y


def _ffn(x, mod, mod_row, g, w13, w2, layer, *, k0, attn=None, w_o=None, final_g=None):
    b, t, d = x.shape
    d_ff = w2.shape[1]
    tm = min(ROW_TILE, t)
    row = lambda bi, ti: (bi, ti, 0)
    this_layer = lambda *_: (layer, 0, 0)
    in_specs = [
        pl.BlockSpec((1, tm, d), row),
        pl.BlockSpec((1, N_MOD, d), lambda bi, ti: (mod_row(bi), 0, 0)),
        _const_spec((1, d)),
        pl.BlockSpec((None,) + w13.shape[1:], this_layer, pipeline_mode=pl.Buffered(1)),
        pl.BlockSpec((None,) + w2.shape[1:], this_layer, pipeline_mode=pl.Buffered(1)),
    ]
    args = [x, mod, g.reshape(1, d), w13, w2]
    if attn is not None:
        in_specs += [pl.BlockSpec((1, tm, attn.shape[2]), row), _const_spec(w_o.shape)]
        args += [attn, w_o]
    if final_g is not None:
        in_specs.append(_const_spec((1, d)))
        args.append(final_g.reshape(1, d))
    return pl.pallas_call(
        functools.partial(_ffn_kernel, k0=k0, d_ff=d_ff, has_oproj=attn is not None,
                          final_norm=final_g is not None),
        grid=(b, t // tm),
        in_specs=in_specs,
        out_specs=pl.BlockSpec((1, tm, d), row),
        out_shape=jax.ShapeDtypeStruct((b, t, d), F32),
        compiler_params=_params("parallel", "parallel"),
        name="half_ffn",
    )(*args)


def _rope_pairs(x, cos, sin_signed, half):
    if 2 * half == LANES:
        rot = pltpu.roll(x, half, 1)
    else:
        lane = lax.broadcasted_iota(jnp.int32, x.shape, 1)
        first = (lane % (2 * half)) < half
        rot = jnp.where(first, pltpu.roll(x, LANES - half, 1), pltpu.roll(x, half, 1))
    return x * cos + rot * sin_signed


def _mla_proj_kernel(*refs, use_rope, sm_scale):
    it = iter(refs)
    x_ref, mod_ref, g_ref, w_in_ref, qg_ref, kvg_ref, w_uq_ref, w_ukv_ref = (next(it) for _ in range(8))
    if use_rope:
        cos_ref, sin_ref = next(it), next(it)
    q_ref, kn_ref, kr_ref, vt_ref = (next(it) for _ in range(4))

    mod = mod_ref[0]
    h = _modulated_norm(x_ref[0], g_ref[...], mod[4:5], mod[3:4]).astype(BF16)
    proj = _dot(h, w_in_ref[...])
    c_q = (_rms(proj[:, :A_Q_RANK]) * qg_ref[...]).astype(BF16)
    c_kv = (_rms(proj[:, A_Q_RANK:A_Q_RANK + A_KV_RANK]) * kvg_ref[...]).astype(BF16)
    k_r = proj[:, A_Q_RANK + A_KV_RANK:]
    q_all = _dot(c_q, w_uq_ref[...])
    kv = _dot(c_kv, w_ukv_ref[...])
    n_nope = A_HEADS * A_NOPE
    if use_rope:
        cos, sin = cos_ref[...], sin_ref[...]
        k_r = _rope_pairs(k_r, cos, sin, A_ROPE // 2)
    lane = lax.broadcasted_iota(jnp.int32, (1, LANES), 1)
    for j in range(A_HEADS // 2):
        qr = q_all[:, n_nope + j * LANES:n_nope + (j + 1) * LANES]
        if use_rope:
            qr = _rope_pairs(qr, cos, sin, A_ROPE // 2)
        qr = qr * sm_scale
        for e in range(2):
            hd = 2 * j + e
            keep = (lane < A_ROPE) if e == 0 else (lane >= A_ROPE)
            q_ref[0, :, 2 * hd * LANES:(2 * hd + 1) * LANES] = (
                q_all[:, hd * A_NOPE:(hd + 1) * A_NOPE] * sm_scale).astype(BF16)
            q_ref[0, :, (2 * hd + 1) * LANES:(2 * hd + 2) * LANES] = jnp.where(keep, qr, 0.0).astype(BF16)
    kn_ref[0] = kv[:, :n_nope].astype(BF16)
    kr_ref[0] = k_r.astype(BF16)
    vt_ref[0] = kv[:, n_nope:].T.astype(BF16)


def _mla_proj(x, mod, mod_row, g, w, rope):
    b, t, d = x.shape
    tm = min(ROW_TILE, t)
    row = lambda bi, ti: (bi, ti, 0)
    in_specs = [
        pl.BlockSpec((1, tm, d), row),
        pl.BlockSpec((1, N_MOD, d), lambda bi, ti: (mod_row(bi), 0, 0)),
        _const_spec((1, d)),
        _const_spec(w["w_in"].shape), _const_spec((1, A_Q_RANK)), _const_spec((1, A_KV_RANK)),
        _const_spec(w["w_uq"].shape), _const_spec(w["w_ukv"].shape),
    ]
    args = [x, mod, g.reshape(1, d), w["w_in"], w["q_norm"].reshape(1, -1), w["kv_norm"].reshape(1, -1),
            w["w_uq"], w["w_ukv"]]
    if rope is not None:
        in_specs += [pl.BlockSpec((tm, LANES), lambda bi, ti: (ti, 0))] * 2
        args += list(rope)
    hv = A_HEADS * A_V
    return pl.pallas_call(
        functools.partial(_mla_proj_kernel, use_rope=rope is not None,
                          sm_scale=float((A_NOPE + A_ROPE) ** -0.5 * LOG2E)),
        grid=(b, t // tm),
        in_specs=in_specs,
        out_specs=[
            pl.BlockSpec((1, tm, 2 * LANES * A_HEADS), row),
            pl.BlockSpec((1, tm, A_HEADS * A_NOPE), row),
            pl.BlockSpec((1, tm, LANES), row),
            pl.BlockSpec((1, hv, tm), lambda bi, ti: (bi, 0, ti)),
        ],
        out_shape=[
            jax.ShapeDtypeStruct((b, t, 2 * LANES * A_HEADS), BF16),
            jax.ShapeDtypeStruct((b, t, A_HEADS * A_NOPE), BF16),
            jax.ShapeDtypeStruct((b, t, LANES), BF16),
            jax.ShapeDtypeStruct((b, hv, t), BF16),
        ],
        compiler_params=_params("parallel", "parallel"),
        name="mla_proj",
    )(*args)


def _gqa_proj_kernel(*refs, use_rope, sm_scale):
    it = iter(refs)
    x_ref, mod_ref, g_ref, w_ref, qg_ref, kg_ref = (next(it) for _ in range(6))
    if use_rope:
        cos_ref, sin_ref = next(it), next(it)
    q_ref, k_ref, vt_ref = (next(it) for _ in range(3))

    mod = mod_ref[0]
    h = _modulated_norm(x_ref[0], g_ref[...], mod[4:5], mod[3:4]).astype(BF16)
    proj = _dot(h, w_ref[...])
    qw = B_HEADS * B_HEAD_DIM
    kw = B_KV_HEADS * B_HEAD_DIM
    if use_rope:
        cos, sin = cos_ref[...], sin_ref[...]

    def head(col, gain, scale):
        y = _rms(proj[:, col:col + B_HEAD_DIM]) * gain
        if use_rope:
            y = _rope_pairs(y, cos, sin, B_HEAD_DIM // 2)
        return (y * scale).astype(BF16) if scale != 1.0 else y.astype(BF16)

    for hd in range(B_HEADS):
        q_ref[0, :, hd * B_HEAD_DIM:(hd + 1) * B_HEAD_DIM] = head(hd * B_HEAD_DIM, qg_ref[...], sm_scale)
    for hd in range(B_KV_HEADS):
        k_ref[0, :, hd * B_HEAD_DIM:(hd + 1) * B_HEAD_DIM] = head(qw + hd * B_HEAD_DIM, kg_ref[...], 1.0)
    vt_ref[0] = proj[:, qw + kw:].T.astype(BF16)


def _gqa_proj(x, mod, mod_row, g, w, rope):
    b, t, d = x.shape
    tm = min(GQA_ROW_TILE, t)
    row = lambda bi, ti: (bi, ti, 0)
    qw, kw = B_HEADS * B_HEAD_DIM, B_KV_HEADS * B_HEAD_DIM
    in_specs = [
        pl.BlockSpec((1, tm, d), row),
        pl.BlockSpec((1, N_MOD, d), lambda bi, ti: (mod_row(bi), 0, 0)),
        _const_spec((1, d)),
        _const_spec(w["w_qkv"].shape), _const_spec((1, B_HEAD_DIM)), _const_spec((1, B_HEAD_DIM)),
    ]
    args = [x, mod, g.reshape(1, d), w["w_qkv"], w["q_norm"].reshape(1, -1), w["k_norm"].reshape(1, -1)]
    if rope is not None:
        in_specs += [pl.BlockSpec((tm, LANES), lambda bi, ti: (ti, 0))] * 2
        args += list(rope)
    return pl.pallas_call(
        functools.partial(_gqa_proj_kernel, use_rope=rope is not None, sm_scale=float(B_HEAD_DIM ** -0.5 * LOG2E)),
        grid=(b, t // tm),
        in_specs=in_specs,
        out_specs=[
            pl.BlockSpec((1, tm, qw), row),
            pl.BlockSpec((1, tm, kw), row),
            pl.BlockSpec((1, kw, tm), lambda bi, ti: (bi, 0, ti)),
        ],
        out_shape=[
            jax.ShapeDtypeStruct((b, t, qw), BF16),
            jax.ShapeDtypeStruct((b, t, kw), BF16),
            jax.ShapeDtypeStruct((b, kw, t), BF16),
        ],
        compiler_params=_params("parallel", "parallel"),
        name="gqa_proj",
    )(*args)


def _nb_proj_kernel(x_ref, mod_ref, g_ref, w_ref, q_ref, k_ref, vt_ref, *, sm_scale):
    mod = mod_ref[0]
    h = _modulated_norm(x_ref[0], g_ref[...], mod[4:5], mod[3:4]).astype(BF16)
    proj = _dot(h, w_ref[...])
    hd = C_HEADS * C_HEAD_DIM
    q_ref[0] = (proj[:, :hd] * sm_scale).astype(BF16)
    k_ref[0] = proj[:, hd:2 * hd].astype(BF16)
    vt_ref[0] = proj[:, 2 * hd:].T.astype(BF16)


def _nb_proj(x, mod, mod_row, g, w):
    b, t, d = x.shape
    tm = min(ROW_TILE, t)
    row = lambda bi, ti: (bi, ti, 0)
    hd = C_HEADS * C_HEAD_DIM
    return pl.pallas_call(
        functools.partial(_nb_proj_kernel, sm_scale=float(C_HEAD_DIM ** -0.5 * LOG2E)),
        grid=(b, t // tm),
        in_specs=[
            pl.BlockSpec((1, tm, d), row),
            pl.BlockSpec((1, N_MOD, d), lambda bi, ti: (mod_row(bi), 0, 0)),
            _const_spec((1, d)),
            _const_spec(w["w_qkv"].shape),
        ],
        out_specs=[
            pl.BlockSpec((1, tm, hd), row),
            pl.BlockSpec((1, tm, hd), row),
            pl.BlockSpec((1, hd, tm), lambda bi, ti: (bi, 0, ti)),
        ],
        out_shape=[
            jax.ShapeDtypeStruct((b, t, hd), BF16),
            jax.ShapeDtypeStruct((b, t, hd), BF16),
            jax.ShapeDtypeStruct((b, hd, t), BF16),
        ],
        compiler_params=_params("parallel", "parallel"),
        name="nb_proj",
    )(x, mod, g.reshape(1, d), w["w_qkv"])


def _attn_kernel(*refs, key_rows, per_head, n_kv, hps, dq, q_tile, n_tiles, key_chunk):
    n_groups, k_pieces = len(key_rows), len(per_head)
    it = iter(refs)
    q_ref = next(it)
    k_refs = [[next(it) for _ in range(k_pieces)] for _ in range(n_groups)]
    v_refs = [next(it) for _ in range(n_groups)]
    o_ref = next(it)
    k_scr, v_scr, s_a, s_b, m_a, m_b = (next(it) for _ in range(6))

    r0 = 0
    for gi, rows in enumerate(key_rows):
        for j in range(n_kv):
            for pi in range(k_pieces):
                lanes = slice(j * LANES, (j + 1) * LANES) if per_head[pi] else slice(0, LANES)
                k_scr[j, r0:r0 + rows, pi * LANES:(pi + 1) * LANES] = k_refs[gi][pi][0, :, lanes]
            v_scr[j, :, r0:r0 + rows] = v_refs[gi][0, j * LANES:(j + 1) * LANES, :]
        r0 += rows

    def rows_of(tile):
        return pl.ds(pl.multiple_of(tile * q_tile, q_tile), q_tile)

    def kv_of(head):
        return head if n_kv > 1 else 0

    def scores(tile, head, s_buf, m_buf):
        q = q_ref[0, rows_of(tile), head * dq:(head + 1) * dq]
        s_t = _dot_nt(k_scr[kv_of(head)], q)
        s_buf[...] = s_t
        m_buf[...] = jnp.max(s_t, axis=0, keepdims=True)

    def softmax_pv(tile, head, s_buf, m_buf):
        p = jnp.exp2(s_buf[...] - m_buf[...])
        l = jnp.sum(p, axis=0, keepdims=True)
        o_t = _dot(v_scr[kv_of(head)], p.astype(BF16)) / l
        o_ref[0, rows_of(tile), head * LANES:(head + 1) * LANES] = o_t.T.astype(BF16)

    scores(0, 0, s_a, m_a)
    if hps * n_tiles == 1:
        softmax_pv(0, 0, s_a, m_a)
        return

    s_rows = s_a.shape[0]
    bufs = ((s_a, m_a), (s_b, m_b))

    def phase(tile_n, head_n, s_next, m_next, tile_c, head_c, s_cur, m_cur_buf):
        if not key_chunk:
            scores(tile_n, head_n, s_next, m_next)
            softmax_pv(tile_c, head_c, s_cur, m_cur_buf)
            return
        q = q_ref[0, rows_of(tile_n), head_n * dq:(head_n + 1) * dq]
        m_cur = m_cur_buf[...]
        m8 = l8 = acc = None
        for c0 in range(0, s_rows, key_chunk):
            rows = slice(c0, min(c0 + key_chunk, s_rows))
            s_c = _dot_nt(k_scr[kv_of(head_n), rows, :], q)
            s_next[rows, :] = s_c
            mc = jnp.max(s_c.reshape(-1, 8, q_tile), axis=0)
            m8 = mc if m8 is None else jnp.maximum(m8, mc)
            p = jnp.exp2(s_cur[rows, :] - m_cur)
            lc = jnp.sum(p.reshape(-1, 8, q_tile), axis=0)
            l8 = lc if l8 is None else l8 + lc
            a = _dot(v_scr[kv_of(head_c), :, rows], p.astype(BF16))
            acc = a if acc is None else acc + a
        m_next[...] = jnp.max(m8, axis=0, keepdims=True)
        l = jnp.sum(l8, axis=0, keepdims=True)
        o_ref[0, rows_of(tile_c), head_c * LANES:(head_c + 1) * LANES] = (acc / l).T.astype(BF16)

    def one_tile(tile, carry):
        for head in range(hps):
            if head + 1 < hps:
                nxt = (tile, head + 1)
            else:
                nxt = (jnp.minimum(tile + 1, n_tiles - 1), 0)
            phase(*nxt, *bufs[(head + 1) % 2], tile, head, *bufs[head % 2])
        return carry

    assert hps % 2 == 0
    lax.fori_loop(0, n_tiles, one_tile, 0)


def _attention(q, k_groups, v_groups, *, heads, hps, per_head, share_kv, key_chunk):
    b, tq, qw = q.shape
    dq = qw // heads
    q_tile = min(Q_TILE, tq)
    n_tiles = tq // q_tile
    n_kv = 1 if share_kv else hps
    assert heads % hps == 0 and (hps * n_tiles == 1 or hps % 2 == 0)
    key_rows = tuple(g[0].shape[1] for g in k_groups)
    k_pieces = len(per_head)
    s_total = sum(key_rows)
    in_specs = [pl.BlockSpec((1, tq, hps * dq), lambda bi, hg: (bi, 0, hg))]
    args = [q]
    for grp in k_groups:
        for pi, arr in enumerate(grp):
            width = n_kv * LANES if per_head[pi] else LANES
            moves = per_head[pi] or share_kv
            in_specs.append(pl.BlockSpec((1, arr.shape[1], width),
                                         lambda bi, hg, moves=moves: (bi, 0, hg if moves else 0)))
            args.append(arr)
    for arr in v_groups:
        in_specs.append(pl.BlockSpec((1, n_kv * LANES, arr.shape[2]), lambda bi, hg: (bi, hg, 0)))
        args.append(arr)
    return pl.pallas_call(
        functools.partial(_attn_kernel, key_rows=key_rows, per_head=tuple(per_head), n_kv=n_kv, hps=hps,
                          dq=dq, q_tile=q_tile, n_tiles=n_tiles, key_chunk=key_chunk),
        grid=(b, heads // hps),
        in_specs=in_specs,
        out_specs=pl.BlockSpec((1, tq, hps * LANES), lambda bi, hg: (bi, 0, hg)),
        out_shape=jax.ShapeDtypeStruct((b, tq, heads * LANES), BF16),
        scratch_shapes=[
            pltpu.VMEM((n_kv, s_total, k_pieces * LANES), BF16), pltpu.VMEM((n_kv, LANES, s_total), BF16),
            pltpu.VMEM((s_total, q_tile), F32), pltpu.VMEM((s_total, q_tile), F32),
            pltpu.VMEM((1, q_tile), F32), pltpu.VMEM((1, q_tile), F32),
        ],
        compiler_params=_params("parallel", "parallel"),
        name="attention",
    )(*args)


def _nb_attn_kernel(*refs, has_window, rows, tile, n_groups):
    it = iter(refs)
    q_ref, kc_ref, vc_ref = next(it), next(it), next(it)
    if has_window:
        kl_ref, vl_ref, bias_ref = next(it), next(it), next(it)
    o_ref = next(it)
    s_a, s_b, m_a, m_b = (next(it) for _ in range(4))
    c = kc_ref.shape[1]
    win = NB_KEY_ROWS * GRID_W
    lane = lax.broadcasted_iota(jnp.int32, (1, LANES), 1)

    def tile_rows(g):
        return pl.ds(pl.multiple_of(g * tile, tile), tile)

    def window(g):
        first_row = jnp.clip(NB_ROWS * g - C_WIN_ROWS // 2, 0, rows - NB_KEY_ROWS)
        return pl.ds(pl.multiple_of(first_row * GRID_W, 2 * LANES), win)

    def scores(g, s_buf, m_buf):
        q = q_ref[0, tile_rows(g), :]
        zero = jnp.zeros_like(q)
        q2 = jnp.concatenate([jnp.where(lane < C_HEAD_DIM, q, zero), jnp.where(lane >= C_HEAD_DIM, q, zero)],
                             axis=0)
        s_c = _dot_nt(kc_ref[0], q2)
        s_buf[0:c, :] = s_c
        m = jnp.max(s_c, axis=0, keepdims=True)
        if has_window:
            variant = jnp.where(g == 0, 0, jnp.where(g == n_groups - 1, 2, 1))
            s_w = _dot_nt(kl_ref[0, window(g), :], q2)
            for e in range(2):
                s_e = s_w[:, e * tile:(e + 1) * tile] + bias_ref[variant, e]
                s_buf[c:, e * tile:(e + 1) * tile] = s_e
                m_e = jnp.maximum(m[:, e * tile:(e + 1) * tile], jnp.max(s_e, axis=0, keepdims=True))
                m_buf[:, e * tile:(e + 1) * tile] = m_e
        else:
            m_buf[...] = m

    def softmax_pv(g, s_buf, m_buf):
        p = jnp.exp2(s_buf[...] - m_buf[...])
        l = jnp.sum(p, axis=0, keepdims=True)
        pb = p.astype(BF16)
        o2 = _dot(vc_ref[0], pb[0:c])
        if has_window:
            o2 = o2 + _dot(vl_ref[0, :, window(g)], pb[c:])
        o_t = jnp.concatenate([o2[e * C_HEAD_DIM:(e + 1) * C_HEAD_DIM, e * tile:(e + 1) * tile]
                               / l[:, e * tile:(e + 1) * tile] for e in range(2)], axis=0)
        o_ref[0, tile_rows(g), :] = o_t.T.astype(BF16)

    scores(0, s_a, m_a)
    if n_groups == 1:
        softmax_pv(0, s_a, m_a)
        return

    def phase(g_next, s_next, m_next, g_cur, s_cur, m_cur_buf):
        q = q_ref[0, tile_rows(g_next), :]
        zero = jnp.zeros_like(q)
        q2 = jnp.concatenate([jnp.where(lane < C_HEAD_DIM, q, zero), jnp.where(lane >= C_HEAD_DIM, q, zero)],
                             axis=0)
        m_cur = m_cur_buf[...]
        variant = jnp.where(g_next == 0, 0, jnp.where(g_next == n_groups - 1, 2, 1))
        w_next, w_cur = window(g_next), window(g_cur)
        s_c = _dot_nt(kc_ref[0], q2)
        s_next[0:c, :] = s_c
        m8 = jnp.max(s_c.reshape(-1, 8, 2 * tile), axis=0)
        p = jnp.exp2(s_cur[0:c, :] - m_cur)
        l8 = jnp.sum(p.reshape(-1, 8, 2 * tile), axis=0)
        o2 = _dot(vc_ref[0], p.astype(BF16))
        for c0 in range(0, win, NB_CHUNK):
            s_w = _dot_nt(kl_ref[0, pl.ds(w_next.start + c0, NB_CHUNK), :], q2)
            s_w = jnp.concatenate([s_w[:, e * tile:(e + 1) * tile] + bias_ref[variant, e, c0:c0 + NB_CHUNK, :]
                                   for e in range(2)], axis=1)
            s_next[c + c0:c + c0 + NB_CHUNK, :] = s_w
            m8 = jnp.maximum(m8, jnp.max(s_w.reshape(-1, 8, 2 * tile), axis=0))
            p = jnp.exp2(s_cur[c + c0:c + c0 + NB_CHUNK, :] - m_cur)
            l8 = l8 + jnp.sum(p.reshape(-1, 8, 2 * tile), axis=0)
            o2 = o2 + _dot(vl_ref[0, :, pl.ds(w_cur.start + c0, NB_CHUNK)], p.astype(BF16))
        m_next[...] = jnp.max(m8, axis=0, keepdims=True)
        l = jnp.sum(l8, axis=0, keepdims=True)
        o_t = jnp.concatenate([o2[e * C_HEAD_DIM:(e + 1) * C_HEAD_DIM, e * tile:(e + 1) * tile]
                               / l[:, e * tile:(e + 1) * tile] for e in range(2)], axis=0)
        o_ref[0, tile_rows(g_cur), :] = o_t.T.astype(BF16)

    def pair(i, carry):
        g = 2 * i
        phase(g + 1, s_b, m_b, g, s_a, m_a)
        phase(jnp.minimum(g + 2, n_groups - 1), s_a, m_a, g + 1, s_b, m_b)
        return carry

    lax.fori_loop(0, n_groups // 2, pair, 0)


def _nb_attention(q, k_ctx, vt_ctx, k_lat=None, vt_lat=None, bias=None):
    b, tq, hw = q.shape
    pairs = hw // LANES
    has_window = k_lat is not None
    c = k_ctx.shape[1]
    if has_window:
        t = k_lat.shape[1]
        rows = t // GRID_W
        tile = NB_ROWS * GRID_W
        keys = c + NB_KEY_ROWS * GRID_W
    else:
        rows, tile, keys = 0, tq, c
    n_groups = tq // tile
    assert n_groups == 1 or n_groups % 2 == 0
    in_specs = [
        pl.BlockSpec((1, tq, LANES), lambda p, bi: (bi, 0, p)),
        pl.BlockSpec((1, c, LANES), lambda p, bi: (bi, 0, p)),
        pl.BlockSpec((1, LANES, c), lambda p, bi: (bi, p, 0)),
    ]
    args = [q, k_ctx, vt_ctx]
    if has_window:
        in_specs += [
            pl.BlockSpec((1, t, LANES), lambda p, bi: (bi, 0, p)),
            pl.BlockSpec((1, LANES, t), lambda p, bi: (bi, p, 0)),
            pl.BlockSpec((3, 2, NB_KEY_ROWS * GRID_W, tile), lambda p, bi: (0, p, 0, 0)),
        ]
        args += [k_lat, vt_lat, bias]
    return pl.pallas_call(
        functools.partial(_nb_attn_kernel, has_window=has_window, rows=rows, tile=tile, n_groups=n_groups),
        grid=(pairs, b),
        in_specs=in_specs,
        out_specs=pl.BlockSpec((1, tq, LANES), lambda p, bi: (bi, 0, p)),
        out_shape=jax.ShapeDtypeStruct((b, tq, hw), BF16),
        scratch_shapes=[pltpu.VMEM((keys, 2 * tile), F32), pltpu.VMEM((keys, 2 * tile), F32),
                        pltpu.VMEM((1, 2 * tile), F32), pltpu.VMEM((1, 2 * tile), F32)],
        compiler_params=_params("parallel", "parallel"),
        name="nb_attention",
    )(*args)


def _nb_bias_table(rpb, rows):
    heads = rpb.shape[0]
    tile_q = NB_ROWS * GRID_W
    return pl.pallas_call(
        functools.partial(_nb_bias_kernel, rows=rows),
        grid=(heads,),
        in_specs=[pl.BlockSpec(memory_space=pltpu.SMEM)],
        out_specs=pl.BlockSpec((3, 1, NB_KEY_ROWS * GRID_W, tile_q), lambda h: (0, h, 0, 0)),
        out_shape=jax.ShapeDtypeStruct((3, heads, NB_KEY_ROWS * GRID_W, tile_q), F32),
        compiler_params=_params("parallel"),
        name="nb_bias",
    )(rpb.reshape(-1))


def _nb_bias_kernel(rpb_ref, o_ref, *, rows):
    n_a, n_b = 2 * C_WIN_ROWS - 1, 2 * C_WIN_COLS - 1
    tile_q = NB_ROWS * GRID_W
    shape = (GRID_W, tile_q)
    kc = lax.broadcasted_iota(jnp.int32, shape, 0)
    lane = lax.broadcasted_iota(jnp.int32, shape, 1)
    qc = lane % GRID_W
    qi = lane // GRID_W
    c0 = jnp.clip(qc - C_WIN_COLS // 2, 0, GRID_W - C_WIN_COLS)
    col_ok = (kc >= c0) & (kc < c0 + C_WIN_COLS)
    dcol = kc - qc + (C_WIN_COLS - 1)
    base = pl.program_id(0) * (n_a * n_b)
    masked = jnp.full(shape, MASK_VALUE, F32)
    planes = []
    for a in range(n_a):
        acc = masked
        for bb in range(n_b):
            acc = jnp.where(dcol == bb, rpb_ref[base + a * n_b + bb], acc)
        planes.append(jnp.where(col_ok, acc * LOG2E, MASK_VALUE))
    groups = rows // NB_ROWS
    for v, g in enumerate((0, 1, groups - 1)):
        first_key_row = min(max(NB_ROWS * g - C_WIN_ROWS // 2, 0), rows - NB_KEY_ROWS)
        for j in range(NB_KEY_ROWS):
            kr = first_key_row + j
            blk = masked
            for i in range(NB_ROWS):
                qr = NB_ROWS * g + i
                r0 = min(max(qr - C_WIN_ROWS // 2, 0), rows - C_WIN_ROWS)
                if r0 <= kr < r0 + C_WIN_ROWS:
                    blk = jnp.where(qi == i, planes[kr - qr + C_WIN_ROWS - 1], blk)
            o_ref[v, 0, j * GRID_W:(j + 1) * GRID_W, :] = blk


def _rope_tables(rows, rot_dim):
    n = rot_dim // 4
    inv_freq = ROPE_THETA ** (-jnp.arange(n, dtype=F32) / n)
    t = jnp.arange(rows * GRID_W, dtype=jnp.int32)
    r = (t // GRID_W).astype(F32)
    col = (t % GRID_W).astype(F32)
    ang = jnp.concatenate([r[:, None] * inv_freq[None, :], col[:, None] * inv_freq[None, :]], axis=-1)
    cos, sin = jnp.cos(ang), jnp.sin(ang)
    reps = LANES // rot_dim
    return (jnp.tile(jnp.concatenate([cos, cos], axis=-1), (1, reps)),
            jnp.tile(jnp.concatenate([-sin, sin], axis=-1), (1, reps)))


def _mla_weights(w_in, q_norm, kv_norm, w_uq, w_ukv, w_o):
    rank = A_Q_RANK + A_KV_RANK
    k_r = w_in[:, rank:]
    uq = w_uq.reshape(A_Q_RANK, A_HEADS, A_NOPE + A_ROPE)
    ukv = w_ukv.reshape(A_KV_RANK, A_HEADS, A_NOPE + A_V)
    return {
        "w_in": jnp.concatenate([w_in[:, :rank], k_r, k_r], axis=1).astype(BF16),
        "q_norm": q_norm, "kv_norm": kv_norm,
        "w_uq": jnp.concatenate([uq[:, :, :A_NOPE].reshape(A_Q_RANK, -1),
                                 uq[:, :, A_NOPE:].reshape(A_Q_RANK, -1)], axis=1).astype(BF16),
        "w_ukv": jnp.concatenate([ukv[:, :, :A_NOPE].reshape(A_KV_RANK, -1),
                                  ukv[:, :, A_NOPE:].reshape(A_KV_RANK, -1)], axis=1).astype(BF16),
        "w_o": w_o.astype(BF16),
    }


def kernel(x, c, ctx, c_ctx, norm_g, w_mod, b_mod, ffn1_w13, ffn1_w2, ffn2_w13, ffn2_w2, a_w_in, a_q_norm, a_kv_norm, a_w_uq, a_w_ukv, a_w_o, b_w_qkv, b_q_norm, b_k_norm, b_w_o, c_w_qkv, c_rpb, c_w_o, final_norm_g):
    b, t, d = x.shape
    depth = w_mod.shape[0]
    rows = t // GRID_W
    assert b < MOD_ROWS and t % (NB_ROWS * GRID_W) == 0 and rows >= NB_KEY_ROWS + NB_ROWS

    c_rows = jnp.zeros((MOD_ROWS, d), F32).at[:b].set(c).at[b].set(c_ctx)
    mod_all = _modulation(c_rows, w_mod, b_mod).reshape(depth, MOD_ROWS, N_MOD, d)
    lat_row = lambda bi: bi
    ctx_row = lambda bi: b

    rope_a = _rope_tables(rows, A_ROPE)
    rope_b = _rope_tables(rows, B_HEAD_DIM)

    w13_1, w2_1 = ffn1_w13.astype(BF16), ffn1_w2.astype(BF16)
    w13_2, w2_2 = ffn2_w13.astype(BF16), ffn2_w2.astype(BF16)

    xc = ctx
    for i in range(depth):
        ctx_out = i < depth - 1
        last = i == depth - 1
        mod = mod_all[i]
        kind, j = i % N_MIXERS, i // N_MIXERS
        x = _ffn(x, mod, lat_row, norm_g[i, 0], w13_1, w2_1, i, k0=0)
        xc = _ffn(xc, mod, ctx_row, norm_g[i, 0], w13_1, w2_1, i, k0=0)

        if kind == 0:
            w = _mla_weights(a_w_in[j], a_q_norm[j], a_kv_norm[j], a_w_uq[j], a_w_ukv[j], a_w_o[j])
            q, kn, kr, vt = _mla_proj(x, mod, lat_row, norm_g[i, 1], w, rope_a)
            qc, knc, krc, vtc = _mla_proj(xc, mod, ctx_row, norm_g[i, 1], w, None)
            cfg = dict(heads=A_HEADS, hps=A_HEADS_PER_STEP, per_head=(True, False), share_kv=False,
                       key_chunk=MLA_KEY_CHUNK)
            o = _attention(q, [[knc, krc], [kn, kr]], [vtc, vt], **cfg)
            if ctx_out:
                oc = _attention(qc, [[knc, krc]], [vtc], **dict(cfg, hps=A_HEADS))
            w_o = w["w_o"]
        elif kind == 1:
            w = {"w_qkv": b_w_qkv[j].astype(BF16), "q_norm": b_q_norm[j], "k_norm": b_k_norm[j]}
            q, k, vt = _gqa_proj(x, mod, lat_row, norm_g[i, 1], w, rope_b)
            qc, kc, vtc = _gqa_proj(xc, mod, ctx_row, norm_g[i, 1], w, None)
            cfg = dict(heads=B_HEADS, hps=B_HEADS // B_KV_HEADS, per_head=(False,), share_kv=True,
                       key_chunk=GQA_KEY_CHUNK)
            o = _attention(q, [[kc], [k]], [vtc, vt], **cfg)
            if ctx_out:
                oc = _attention(qc, [[kc]], [vtc], **cfg)
            w_o = b_w_o[j].astype(BF16)
        else:
            w = {"w_qkv": c_w_qkv[j].astype(BF16)}
            q, k, vt = _nb_proj(x, mod, lat_row, norm_g[i, 1], w)
            qc, kc, vtc = _nb_proj(xc, mod, ctx_row, norm_g[i, 1], w)
            o = _nb_attention(q, kc, vtc, k, vt, _nb_bias_table(c_rpb[j], rows))
            if ctx_out:
                oc = _nb_attention(qc, kc, vtc)
            w_o = c_w_o[j].astype(BF16)

        x = _ffn(x, mod, lat_row, norm_g[i, 2], w13_2, w2_2, i, k0=6, attn=o, w_o=w_o,
                 final_g=final_norm_g if last else None)
        if ctx_out:
            xc = _ffn(xc, mod, ctx_row, norm_g[i, 2], w13_2, w2_2, i, k0=6, attn=oc, w_o=w_o)
    return x
```

```python
import functools

import jax
import jax.numpy as jnp
from jax import lax
from jax.experimental import pallas as pl
from jax.experimental.pallas import tpu as pltpu

F32 = jnp.float32
BF16 = jnp.bfloat16

GRID_W = 64
N_MIXERS = 3
N_MOD = 9
EPS = 1e-6
ROPE_THETA = 10000.0

A_HEADS, A_NOPE, A_ROPE, A_V = 8, 128, 64, 128
A_Q_RANK, A_KV_RANK = 384, 256
B_HEADS, B_KV_HEADS, B_HEAD_DIM = 8, 2, 128
C_HEADS, C_HEAD_DIM, C_WIN_ROWS, C_WIN_COLS = 16, 64, 8, 16

LANES = 128
MOD_ROWS = 16
MASK_VALUE = -1e30
LOG2E = 1.4426950408889634
VMEM_LIMIT = 56 * 1024 * 1024

ROW_TILE = 512
GQA_ROW_TILE = 256
A_HEADS_PER_STEP = 2
Q_TILE = 512
GQA_KEY_CHUNK = 512
MLA_KEY_CHUNK = 0
NB_CHUNK = 256
NB_ROWS = 4
NB_KEY_ROWS = 12


def _params(*sem):
    return pltpu.CompilerParams(dimension_semantics=sem, vmem_limit_bytes=VMEM_LIMIT)


def _const_spec(shape):
    nd = len(shape)
    return pl.BlockSpec(shape, lambda *_: (0,) * nd, pipeline_mode=pl.Buffered(1))


def _dot(a, b):
    return jnp.dot(a, b, preferred_element_type=F32)


def _dot_nt(a, b):
    return lax.dot_general(a, b, (((1,), (1,)), ((), ())), preferred_element_type=F32)


def _rms(x):
    return x * lax.rsqrt(jnp.mean(x * x, axis=-1, keepdims=True) + EPS)


def _modulated_norm(x, g, scale, shift):
    return _rms(x) * (g * (1.0 + scale)) + shift


def _silu(x):
    return x / (1.0 + jnp.exp(-x))


def _mod_kernel(c_ref, w_ref, b_ref, o_ref):
    sc = _silu(c_ref[...]).astype(BF16)
    o_ref[0] = _dot(sc, w_ref[0].astype(BF16)) + b_ref[0]


def _modulation(c_rows, w_mod, b_mod):
    depth, d, n = w_mod.shape
    tn = d
    return pl.pallas_call(
        _mod_kernel,
        grid=(depth, n // tn),
        in_specs=[
            pl.BlockSpec((MOD_ROWS, d), lambda i, j: (0, 0)),
            pl.BlockSpec((1, d, tn), lambda i, j: (i, 0, j)),
            pl.BlockSpec((1, 1, tn), lambda i, j: (i, 0, j)),
        ],
        out_specs=pl.BlockSpec((1, MOD_ROWS, tn), lambda i, j: (i, 0, j)),
        out_shape=jax.ShapeDtypeStruct((depth, MOD_ROWS, n), F32),
        compiler_params=_params("parallel", "parallel"),
        name="modulation",
    )(c_rows, w_mod, b_mod.reshape(depth, 1, n))


def _ffn_kernel(*refs, k0, d_ff, has_oproj, final_norm):
    it = iter(refs)
    x_ref, mod_ref, g_ref, w13_ref, w2_ref = (next(it) for _ in range(5))
    if has_oproj:
        a_ref, wo_ref = next(it), next(it)
    if final_norm:
        fg_ref = next(it)
    out_ref = next(it)

    x = x_ref[0]
    mod = mod_ref[0]
    if has_oproj:
        x = x + mod[5:6] * _dot(a_ref[0], wo_ref[...])
    h = _modulated_norm(x, g_ref[...], mod[k0 + 1:k0 + 2], mod[k0:k0 + 1]).astype(BF16)
    hgu = _dot(h, w13_ref[...])
    act = (_silu(hgu[:, :d_ff]) * hgu[:, d_ff:]).astype(BF16)
    y = x + (0.5 * mod[k0 + 2:k0 + 3]) * _dot(act, w2_ref[...])
    if final_norm:
        y = _rms(y) * fg_ref[...]
    out_ref[0] = y


def _ffn(x, mod, mod_row, g, w13, w2, layer, *, k0, attn=None, w_o=None, final_g=None):
    b, t, d = x.shape
    d_ff = w2.shape[1]
    tm = min(ROW_TILE, t)
    row = lambda bi, ti: (bi, ti, 0)
    this_layer = lambda *_: (layer, 0, 0)
    in_specs = [
        pl.BlockSpec((1, tm, d), row),
        pl.BlockSpec((1, N_MOD, d), lambda bi, ti: (mod_row(bi), 0, 0)),
        _const_spec((1, d)),
        pl.BlockSpec((None,) + w13.shape[1:], this_layer, pipeline_mode=pl.Buffered(1)),
        pl.BlockSpec((None,) + w2.shape[1:], this_layer, pipeline_mode=pl.Buffered(1)),
    ]
    args = [x, mod, g.reshape(1, d), w13, w2]
    if attn is not None:
        in_specs += [pl.BlockSpec((1, tm, attn.shape[2]), row), _const_spec(w_o.shape)]
        args += [attn, w_o]
    if final_g is not None:
        in_specs.append(_const_spec((1, d)))
        args.append(final_g.reshape(1, d))
    return pl.pallas_call(
        functools.partial(_ffn_kernel, k0=k0, d_ff=d_ff, has_oproj=attn is not None,
                          final_norm=final_g is not None),
        grid=(b, t // tm),
        in_specs=in_specs,
        out_specs=pl.BlockSpec((1, tm, d), row),
        out_shape=jax.ShapeDtypeStruct((b, t, d), F32),
        compiler_params=_params("parallel", "parallel"),
        name="half_ffn",
    )(*args)


def _rope_pairs(x, cos, sin_signed, half):
    if 2 * half == LANES:
        rot = pltpu.roll(x, half, 1)
    else:
        lane = lax.broadcasted_iota(jnp.int32, x.shape, 1)
        first = (lane % (2 * half)) < half
        rot = jnp.where(first, pltpu.roll(x, LANES - half, 1), pltpu.roll(x, half, 1))
    return x * cos + rot * sin_signed


def _mla_proj_kernel(*refs, use_rope, sm_scale):
    it = iter(refs)
    x_ref, mod_ref, g_ref, w_in_ref, qg_ref, kvg_ref, w_uq_ref, w_ukv_ref = (next(it) for _ in range(8))
    if use_rope:
        cos_ref, sin_ref = next(it), next(it)
    q_ref, kn_ref, kr_ref, vt_ref = (next(it) for _ in range(4))

    mod = mod_ref[0]
    h = _modulated_norm(x_ref[0], g_ref[...], mod[4:5], mod[3:4]).astype(BF16)
    proj = _dot(h, w_in_ref[...])
    c_q = (_rms(proj[:, :A_Q_RANK]) * qg_ref[...]).astype(BF16)
    c_kv = (_rms(proj[:, A_Q_RANK:A_Q_RANK + A_KV_RANK]) * kvg_ref[...]).astype(BF16)
    k_r = proj[:, A_Q_RANK + A_KV_RANK:]
    q_all = _dot(c_q, w_uq_ref[...])
    kv = _dot(c_kv, w_ukv_ref[...])
    n_nope = A_HEADS * A_NOPE
    if use_rope:
        cos, sin = cos_ref[...], sin_ref[...]
        k_r = _rope_pairs(k_r, cos, sin, A_ROPE // 2)
    lane = lax.broadcasted_iota(jnp.int32, (1, LANES), 1)
    for j in range(A_HEADS // 2):
        qr = q_all[:, n_nope + j * LANES:n_nope + (j + 1) * LANES]
        if use_rope:
            qr = _rope_pairs(qr, cos, sin, A_ROPE // 2)
        qr = qr * sm_scale
        for e in range(2):
            hd = 2 * j + e
            keep = (lane < A_ROPE) if e == 0 else (lane >= A_ROPE)
            q_ref[0, :, 2 * hd * LANES:(2 * hd + 1) * LANES] = (
                q_all[:, hd * A_NOPE:(hd + 1) * A_NOPE] * sm_scale).astype(BF16)
            q_ref[0, :, (2 * hd + 1) * LANES:(2 * hd + 2) * LANES] = jnp.where(keep, qr, 0.0).astype(BF16)
    kn_ref[0] = kv[:, :n_nope].astype(BF16)
    kr_ref[0] = k_r.astype(BF16)
    vt_ref[0] = kv[:, n_nope:].T.astype(BF16)


def _mla_proj(x, mod, mod_row, g, w, rope):
    b, t, d = x.shape
    tm = min(ROW_TILE, t)
    row = lambda bi, ti: (bi, ti, 0)
    in_specs = [
        pl.BlockSpec((1, tm, d), row),
        pl.BlockSpec((1, N_MOD, d), lambda bi, ti: (mod_row(bi), 0, 0)),
        _const_spec((1, d)),
        _const_spec(w["w_in"].shape), _const_spec((1, A_Q_RANK)), _const_spec((1, A_KV_RANK)),
        _const_spec(w["w_uq"].shape), _const_spec(w["w_ukv"].shape),
    ]
    args = [x, mod, g.reshape(1, d), w["w_in"], w["q_norm"].reshape(1, -1), w["kv_norm"].reshape(1, -1),
            w["w_uq"], w["w_ukv"]]
    if rope is not None:
        in_specs += [pl.BlockSpec((tm, LANES), lambda bi, ti: (ti, 0))] * 2
        args += list(rope)
    hv = A_HEADS * A_V
    return pl.pallas_call(
        functools.partial(_mla_proj_kernel, use_rope=rope is not None,
                          sm_scale=float((A_NOPE + A_ROPE) ** -0.5 * LOG2E)),
        grid=(b, t // tm),
        in_specs=in_specs,
        out_specs=[
            pl.BlockSpec((1, tm, 2 * LANES * A_HEADS), row),
            pl.BlockSpec((1, tm, A_HEADS * A_NOPE), row),
            pl.BlockSpec((1, tm, LANES), row),
            pl.BlockSpec((1, hv, tm), lambda bi, ti: (bi, 0, ti)),
        ],
        out_shape=[
            jax.ShapeDtypeStruct((b, t, 2 * LANES * A_HEADS), BF16),
            jax.ShapeDtypeStruct((b, t, A_HEADS * A_NOPE), BF16),
            jax.ShapeDtypeStruct((b, t, LANES), BF16),
            jax.ShapeDtypeStruct((b, hv, t), BF16),
        ],
        compiler_params=_params("parallel", "parallel"),
        name="mla_proj",
    )(*args)


def _gqa_proj_kernel(*refs, use_rope, sm_scale):
    it = iter(refs)
    x_ref, mod_ref, g_ref, w_ref, qg_ref, kg_ref = (next(it) for _ in range(6))
    if use_rope:
        cos_ref, sin_ref = next(it), next(it)
    q_ref, k_ref, vt_ref = (next(it) for _ in range(3))

    mod = mod_ref[0]
    h = _modulated_norm(x_ref[0], g_ref[...], mod[4:5], mod[3:4]).astype(BF16)
    proj = _dot(h, w_ref[...])
    qw = B_HEADS * B_HEAD_DIM
    kw = B_KV_HEADS * B_HEAD_DIM
    if use_rope:
        cos, sin = cos_ref[...], sin_ref[...]

    def head(col, gain, scale):
        y = _rms(proj[:, col:col + B_HEAD_DIM]) * gain
        if use_rope:
            y = _rope_pairs(y, cos, sin, B_HEAD_DIM // 2)
        return (y * scale).astype(BF16) if scale != 1.0 else y.astype(BF16)

    for hd in range(B_HEADS):
        q_ref[0, :, hd * B_HEAD_DIM:(hd + 1) * B_HEAD_DIM] = head(hd * B_HEAD_DIM, qg_ref[...], sm_scale)
    for hd in range(B_KV_HEADS):
        k_ref[0, :, hd * B_HEAD_DIM:(hd + 1) * B_HEAD_DIM] = head(qw + hd * B_HEAD_DIM, kg_ref[...], 1.0)
    vt_ref[0] = proj[:, qw + kw:].T.astype(BF16)


def _gqa_proj(x, mod, mod_row, g, w, rope):
    b, t, d = x.shape
    tm = min(GQA_ROW_TILE, t)
    row = lambda bi, ti: (bi, ti, 0)
    qw, kw = B_HEADS * B_HEAD_DIM, B_KV_HEADS * B_HEAD_DIM
    in_specs = [
        pl.BlockSpec((1, tm, d), row),
        pl.BlockSpec((1, N_MOD, d), lambda bi, ti: (mod_row(bi), 0, 0)),
        _const_spec((1, d)),
        _const_spec(w["w_qkv"].shape), _const_spec((1, B_HEAD_DIM)), _const_spec((1, B_HEAD_DIM)),
    ]
    args = [x, mod, g.reshape(1, d), w["w_qkv"], w["q_norm"].reshape(1, -1), w["k_norm"].reshape(1, -1)]
    if rope is not None:
        in_specs += [pl.BlockSpec((tm, LANES), lambda bi, ti: (ti, 0))] * 2
        args += list(rope)
    return pl.pallas_call(
        functools.partial(_gqa_proj_kernel, use_rope=rope is not None, sm_scale=float(B_HEAD_DIM ** -0.5 * LOG2E)),
        grid=(b, t // tm),
        in_specs=in_specs,
        out_specs=[
            pl.BlockSpec((1, tm, qw), row),
            pl.BlockSpec((1, tm, kw), row),
            pl.BlockSpec((1, kw, tm), lambda bi, ti: (bi, 0, ti)),
        ],
        out_shape=[
            jax.ShapeDtypeStruct((b, t, qw), BF16),
            jax.ShapeDtypeStruct((b, t, kw), BF16),
            jax.ShapeDtypeStruct((b, kw, t), BF16),
        ],
        compiler_params=_params("parallel", "parallel"),
        name="gqa_proj",
    )(*args)


def _nb_proj_kernel(x_ref, mod_ref, g_ref, w_ref, q_ref, k_ref, vt_ref, *, sm_scale):
    mod = mod_ref[0]
    h = _modulated_norm(x_ref[0], g_ref[...], mod[4:5], mod[3:4]).astype(BF16)
    proj = _dot(h, w_ref[...])
    hd = C_HEADS * C_HEAD_DIM
    q_ref[0] = (proj[:, :hd] * sm_scale).astype(BF16)
    k_ref[0] = proj[:, hd:2 * hd].astype(BF16)
    vt_ref[0] = proj[:, 2 * hd:].T.astype(BF16)


def _nb_proj(x, mod, mod_row, g, w):
    b, t, d = x.shape
    tm = min(ROW_TILE, t)
    row = lambda bi, ti: (bi, ti, 0)
    hd = C_HEADS * C_HEAD_DIM
    return pl.pallas_call(
        functools.partial(_nb_proj_kernel, sm_scale=float(C_HEAD_DIM ** -0.5 * LOG2E)),
        grid=(b, t // tm),
        in_specs=[
            pl.BlockSpec((1, tm, d), row),
            pl.BlockSpec((1, N_MOD, d), lambda bi, ti: (mod_row(bi), 0, 0)),
            _const_spec((1, d)),
            _const_spec(w["w_qkv"].shape),
        ],
        out_specs=[
            pl.BlockSpec((1, tm, hd), row),
            pl.BlockSpec((1, tm, hd), row),
            pl.BlockSpec((1, hd, tm), lambda bi, ti: (bi, 0, ti)),
        ],
        out_shape=[
            jax.ShapeDtypeStruct((b, t, hd), BF16),
            jax.ShapeDtypeStruct((b, t, hd), BF16),
            jax.ShapeDtypeStruct((b, hd, t), BF16),
        ],
        compiler_params=_params("parallel", "parallel"),
        name="nb_proj",
    )(x, mod, g.reshape(1, d), w["w_qkv"])


def _attn_kernel(*refs, key_rows, per_head, n_kv, hps, dq, q_tile, n_tiles, key_chunk):
    n_groups, k_pieces = len(key_rows), len(per_head)
    it = iter(refs)
    q_ref = next(it)
    k_refs = [[next(it) for _ in range(k_pieces)] for _ in range(n_groups)]
    v_refs = [next(it) for _ in range(n_groups)]
    o_ref = next(it)
    k_scr, v_scr, s_a, s_b, m_a, m_b = (next(it) for _ in range(6))

    r0 = 0
    for gi, rows in enumerate(key_rows):
        for j in range(n_kv):
            for pi in range(k_pieces):
                lanes = slice(j * LANES, (j + 1) * LANES) if per_head[pi] else slice(0, LANES)
                k_scr[j, r0:r0 + rows, pi * LANES:(pi + 1) * LANES] = k_refs[gi][pi][0, :, lanes]
            v_scr[j, :, r0:r0 + rows] = v_refs[gi][0, j * LANES:(j + 1) * LANES, :]
        r0 += rows

    def rows_of(tile):
        return pl.ds(pl.multiple_of(tile * q_tile, q_tile), q_tile)

    def kv_of(head):
        return head if n_kv > 1 else 0

    def scores(tile, head, s_buf, m_buf):
        q = q_ref[0, rows_of(tile), head * dq:(head + 1) * dq]
        s_t = _dot_nt(k_scr[kv_of(head)], q)
        s_buf[...] = s_t
        m_buf[...] = jnp.max(s_t, axis=0, keepdims=True)

    def softmax_pv(tile, head, s_buf, m_buf):
        p = jnp.exp2(s_buf[...] - m_buf[...])
        l = jnp.sum(p, axis=0, keepdims=True)
        o_t = _dot(v_scr[kv_of(head)], p.astype(BF16)) / l
        o_ref[0, rows_of(tile), head * LANES:(head + 1) * LANES] = o_t.T.astype(BF16)

    scores(0, 0, s_a, m_a)
    if hps * n_tiles == 1:
        softmax_pv(0, 0, s_a, m_a)
        return

    s_rows = s_a.shape[0]
    bufs = ((s_a, m_a), (s_b, m_b))

    def phase(tile_n, head_n, s_next, m_next, tile_c, head_c, s_cur, m_cur_buf):
        if not key_chunk:
            scores(tile_n, head_n, s_next, m_next)
            softmax_pv(tile_c, head_c, s_cur, m_cur_buf)
            return
        q = q_ref[0, rows_of(tile_n), head_n * dq:(head_n + 1) * dq]
        m_cur = m_cur_buf[...]
        m8 = l8 = acc = None
        for c0 in range(0, s_rows, key_chunk):
            rows = slice(c0, min(c0 + key_chunk, s_rows))
            s_c = _dot_nt(k_scr[kv_of(head_n), rows, :], q)
            s_next[rows, :] = s_c
            mc = jnp.max(s_c.reshape(-1, 8, q_tile), axis=0)
            m8 = mc if m8 is None else jnp.maximum(m8, mc)
            p = jnp.exp2(s_cur[rows, :] - m_cur)
            lc = jnp.sum(p.reshape(-1, 8, q_tile), axis=0)
            l8 = lc if l8 is None else l8 + lc
            a = _dot(v_scr[kv_of(head_c), :, rows], p.astype(BF16))
            acc = a if acc is None else acc + a
        m_next[...] = jnp.max(m8, axis=0, keepdims=True)
        l = jnp.sum(l8, axis=0, keepdims=True)
        o_ref[0, rows_of(tile_c), head_c * LANES:(head_c + 1) * LANES] = (acc / l).T.astype(BF16)

    def one_tile(tile, last):
        for head in range(hps):
            cur = (tile, head, *bufs[head % 2])
            if head + 1 < hps:
                phase(tile, head + 1, *bufs[(head + 1) % 2], *cur)
            elif not last:
                phase(tile + 1, 0, *bufs[(head + 1) % 2], *cur)
            else:
                softmax_pv(*cur)

    assert hps % 2 == 0

    def body(tile, carry):
        one_tile(tile, False)
        return carry

    lax.fori_loop(0, n_tiles - 1, body, 0)
    one_tile(n_tiles - 1, True)


def _attention(q, k_groups, v_groups, *, heads, hps, per_head, share_kv, key_chunk):
    b, tq, qw = q.shape
    dq = qw // heads
    q_tile = min(Q_TILE, tq)
    n_tiles = tq // q_tile
    n_kv = 1 if share_kv else hps
    assert heads % hps == 0 and (hps * n_tiles == 1 or hps % 2 == 0)
    key_rows = tuple(g[0].shape[1] for g in k_groups)
    k_pieces = len(per_head)
    s_total = sum(key_rows)
    in_specs = [pl.BlockSpec((1, tq, hps * dq), lambda bi, hg: (bi, 0, hg))]
    args = [q]
    for grp in k_groups:
        for pi, arr in enumerate(grp):
            width = n_kv * LANES if per_head[pi] else LANES
            moves = per_head[pi] or share_kv
            in_specs.append(pl.BlockSpec((1, arr.shape[1], width),
                                         lambda bi, hg, moves=moves: (bi, 0, hg if moves else 0)))
            args.append(arr)
    for arr in v_groups:
        in_specs.append(pl.BlockSpec((1, n_kv * LANES, arr.shape[2]), lambda bi, hg: (bi, hg, 0)))
        args.append(arr)
    return pl.pallas_call(
        functools.partial(_attn_kernel, key_rows=key_rows, per_head=tuple(per_head), n_kv=n_kv, hps=hps,
                          dq=dq, q_tile=q_tile, n_tiles=n_tiles, key_chunk=key_chunk),
        grid=(b, heads // hps),
        in_specs=in_specs,
        out_specs=pl.BlockSpec((1, tq, hps * LANES), lambda bi, hg: (bi, 0, hg)),
        out_shape=jax.ShapeDtypeStruct((b, tq, heads * LANES), BF16),
        scratch_shapes=[
            pltpu.VMEM((n_kv, s_total, k_pieces * LANES), BF16), pltpu.VMEM((n_kv, LANES, s_total), BF16),
            pltpu.VMEM((s_total, q_tile), F32), pltpu.VMEM((s_total, q_tile), F32),
            pltpu.VMEM((1, q_tile), F32), pltpu.VMEM((1, q_tile), F32),
        ],
        compiler_params=_params("parallel", "parallel"),
        name="attention",
    )(*args)


def _nb_attn_kernel(*refs, has_window, rows, tile, n_groups):
    it = iter(refs)
    q_ref, kc_ref, vc_ref = next(it), next(it), next(it)
    if has_window:
        kl_ref, vl_ref, bias_ref = next(it), next(it), next(it)
    o_ref = next(it)
    s_a, s_b, m_a, m_b = (next(it) for _ in range(4))
    c = kc_ref.shape[1]
    win = NB_KEY_ROWS * GRID_W
    lane = lax.broadcasted_iota(jnp.int32, (1, LANES), 1)

    def tile_rows(g):
        return pl.ds(pl.multiple_of(g * tile, tile), tile)

    def window(g):
        first_row = jnp.clip(NB_ROWS * g - C_WIN_ROWS // 2, 0, rows - NB_KEY_ROWS)
        return pl.ds(pl.multiple_of(first_row * GRID_W, 2 * LANES), win)

    def scores(g, s_buf, m_buf):
        q = q_ref[0, tile_rows(g), :]
        zero = jnp.zeros_like(q)
        q2 = jnp.concatenate([jnp.where(lane < C_HEAD_DIM, q, zero), jnp.where(lane >= C_HEAD_DIM, q, zero)],
                             axis=0)
        s_c = _dot_nt(kc_ref[0], q2)
        s_buf[0:c, :] = s_c
        m = jnp.max(s_c, axis=0, keepdims=True)
        if has_window:
            variant = jnp.where(g == 0, 0, jnp.where(g == n_groups - 1, 2, 1))
            s_w = _dot_nt(kl_ref[0, window(g), :], q2)
            for e in range(2):
                s_e = s_w[:, e * tile:(e + 1) * tile] + bias_ref[variant, e]
                s_buf[c:, e * tile:(e + 1) * tile] = s_e
                m_e = jnp.maximum(m[:, e * tile:(e + 1) * tile], jnp.max(s_e, axis=0, keepdims=True))
                m_buf[:, e * tile:(e + 1) * tile] = m_e
        else:
            m_buf[...] = m

    def softmax_pv(g, s_buf, m_buf):
        p = jnp.exp2(s_buf[...] - m_buf[...])
        l = jnp.sum(p, axis=0, keepdims=True)
        pb = p.astype(BF16)
        o2 = _dot(vc_ref[0], pb[0:c])
        if has_window:
            o2 = o2 + _dot(vl_ref[0, :, window(g)], pb[c:])
        o_t = jnp.concatenate([o2[e * C_HEAD_DIM:(e + 1) * C_HEAD_DIM, e * tile:(e + 1) * tile]
                               / l[:, e * tile:(e + 1) * tile] for e in range(2)], axis=0)
        o_ref[0, tile_rows(g), :] = o_t.T.astype(BF16)

    scores(0, s_a, m_a)
    if n_groups == 1:
        softmax_pv(0, s_a, m_a)
        return

    def phase(g_next, s_next, m_next, g_cur, s_cur, m_cur_buf):
        q = q_ref[0, tile_rows(g_next), :]
        zero = jnp.zeros_like(q)
        q2 = jnp.concatenate([jnp.where(lane < C_HEAD_DIM, q, zero), jnp.where(lane >= C_HEAD_DIM, q, zero)],
                             axis=0)
        m_cur = m_cur_buf[...]
        variant = jnp.where(g_next == 0, 0, jnp.where(g_next == n_groups - 1, 2, 1))
        w_next, w_cur = window(g_next), window(g_cur)
        s_c = _dot_nt(kc_ref[0], q2)
        s_next[0:c, :] = s_c
        m8 = jnp.max(s_c.reshape(-1, 8, 2 * tile), axis=0)
        p = jnp.exp2(s_cur[0:c, :] - m_cur)
        l8 = jnp.sum(p.reshape(-1, 8, 2 * tile), axis=0)
        o2 = _dot(vc_ref[0], p.astype(BF16))
        for c0 in range(0, win, NB_CHUNK):
            s_w = _dot_nt(kl_ref[0, pl.ds(w_next.start + c0, NB_CHUNK), :], q2)
            s_w = jnp.concatenate([s_w[:, e * tile:(e + 1) * tile] + bias_ref[variant, e, c0:c0 + NB_CHUNK, :]
                                   for e in range(2)], axis=1)
            s_next[c + c0:c + c0 + NB_CHUNK, :] = s_w
            m8 = jnp.maximum(m8, jnp.max(s_w.reshape(-1, 8, 2 * tile), axis=0))
            p = jnp.exp2(s_cur[c + c0:c + c0 + NB_CHUNK, :] - m_cur)
            l8 = l8 + jnp.sum(p.reshape(-1, 8, 2 * tile), axis=0)
            o2 = o2 + _dot(vl_ref[0, :, pl.ds(w_cur.start + c0, NB_CHUNK)], p.astype(BF16))
        m_next[...] = jnp.max(m8, axis=0, keepdims=True)
        l = jnp.sum(l8, axis=0, keepdims=True)
        o_t = jnp.concatenate([o2[e * C_HEAD_DIM:(e + 1) * C_HEAD_DIM, e * tile:(e + 1) * tile]
                               / l[:, e * tile:(e + 1) * tile] for e in range(2)], axis=0)
        o_ref[0, tile_rows(g_cur), :] = o_t.T.astype(BF16)

    def pair(i, carry):
        g = 2 * i
        phase(g + 1, s_b, m_b, g, s_a, m_a)
        phase(g + 2, s_a, m_a, g + 1, s_b, m_b)
        return carry

    lax.fori_loop(0, n_groups // 2 - 1, pair, 0)
    phase(n_groups - 1, s_b, m_b, n_groups - 2, s_a, m_a)
    softmax_pv(n_groups - 1, s_b, m_b)


def _nb_attention(q, k_ctx, vt_ctx, k_lat=None, vt_lat=None, bias=None):
    b, tq, hw = q.shape
    pairs = hw // LANES
    has_window = k_lat is not None
    c = k_ctx.shape[1]
    if has_window:
        t = k_lat.shape[1]
        rows = t // GRID_W
        tile = NB_ROWS * GRID_W
        keys = c + NB_KEY_ROWS * GRID_W
    else:
        rows, tile, keys = 0, tq, c
    n_groups = tq // tile
    assert n_groups == 1 or n_groups % 2 == 0
    in_specs = [
        pl.BlockSpec((1, tq, LANES), lambda p, bi: (bi, 0, p)),
        pl.BlockSpec((1, c, LANES), lambda p, bi: (bi, 0, p)),
        pl.BlockSpec((1, LANES, c), lambda p, bi: (bi, p, 0)),
    ]
    args = [q, k_ctx, vt_ctx]
    if has_window:
        in_specs += [
            pl.BlockSpec((1, t, LANES), lambda p, bi: (bi, 0, p)),
            pl.BlockSpec((1, LANES, t), lambda p, bi: (bi, p, 0)),
            pl.BlockSpec((3, 2, NB_KEY_ROWS * GRID_W, tile), lambda p, bi: (0, p, 0, 0)),
        ]
        args += [k_lat, vt_lat, bias]
    return pl.pallas_call(
        functools.partial(_nb_attn_kernel, has_window=has_window, rows=rows, tile=tile, n_groups=n_groups),
        grid=(pairs, b),
        in_specs=in_specs,
        out_specs=pl.BlockSpec((1, tq, LANES), lambda p, bi: (bi, 0, p)),
        out_shape=jax.ShapeDtypeStruct((b, tq, hw), BF16),
        scratch_shapes=[pltpu.VMEM((keys, 2 * tile), F32), pltpu.VMEM((keys, 2 * tile), F32),
                        pltpu.VMEM((1, 2 * tile), F32), pltpu.VMEM((1, 2 * tile), F32)],
        compiler_params=_params("parallel", "parallel"),
        name="nb_attention",
    )(*args)


def _nb_bias_table(rpb, rows):
    heads = rpb.shape[0]
    tile_q = NB_ROWS * GRID_W
    return pl.pallas_call(
        functools.partial(_nb_bias_kernel, rows=rows),
        grid=(heads,),
        in_specs=[pl.BlockSpec(memory_space=pltpu.SMEM)],
        out_specs=pl.BlockSpec((3, 1, NB_KEY_ROWS * GRID_W, tile_q), lambda h: (0, h, 0, 0)),
        out_shape=jax.ShapeDtypeStruct((3, heads, NB_KEY_ROWS * GRID_W, tile_q), F32),
        compiler_params=_params("parallel"),
        name="nb_bias",
    )(rpb.reshape(-1))


def _nb_bias_kernel(rpb_ref, o_ref, *, rows):
    n_a, n_b = 2 * C_WIN_ROWS - 1, 2 * C_WIN_COLS - 1
    tile_q = NB_ROWS * GRID_W
    shape = (GRID_W, tile_q)
    kc = lax.broadcasted_iota(jnp.int32, shape, 0)
    lane = lax.broadcasted_iota(jnp.int32, shape, 1)
    qc = lane % GRID_W
    qi = lane // GRID_W
    c0 = jnp.clip(qc - C_WIN_COLS // 2, 0, GRID_W - C_WIN_COLS)
    col_ok = (kc >= c0) & (kc < c0 + C_WIN_COLS)
    dcol = kc - qc + (C_WIN_COLS - 1)
    base = pl.program_id(0) * (n_a * n_b)
    masked = jnp.full(shape, MASK_VALUE, F32)
    planes = []
    for a in range(n_a):
        acc = masked
        for bb in range(n_b):
            acc = jnp.where(dcol == bb, rpb_ref[base + a * n_b + bb], acc)
        planes.append(jnp.where(col_ok, acc * LOG2E, MASK_VALUE))
    groups = rows // NB_ROWS
    for v, g in enumerate((0, 1, groups - 1)):
        first_key_row = min(max(NB_ROWS * g - C_WIN_ROWS // 2, 0), rows - NB_KEY_ROWS)
        for j in range(NB_KEY_ROWS):
            kr = first_key_row + j
            blk = masked
            for i in range(NB_ROWS):
                qr = NB_ROWS * g + i
                r0 = min(max(qr - C_WIN_ROWS // 2, 0), rows - C_WIN_ROWS)
                if r0 <= kr < r0 + C_WIN_ROWS:
                    blk = jnp.where(qi == i, planes[kr - qr + C_WIN_ROWS - 1], blk)
            o_ref[v, 0, j * GRID_W:(j + 1) * GRID_W, :] = blk


def _rope_tables(rows, rot_dim):
    n = rot_dim // 4
    inv_freq = ROPE_THETA ** (-jnp.arange(n, dtype=F32) / n)
    t = jnp.arange(rows * GRID_W, dtype=jnp.int32)
    r = (t // GRID_W).astype(F32)
    col = (t % GRID_W).astype(F32)
    ang = jnp.concatenate([r[:, None] * inv_freq[None, :], col[:, None] * inv_freq[None, :]], axis=-1)
    cos, sin = jnp.cos(ang), jnp.sin(ang)
    reps = LANES // rot_dim
    return (jnp.tile(jnp.concatenate([cos, cos], axis=-1), (1, reps)),
            jnp.tile(jnp.concatenate([-sin, sin], axis=-1), (1, reps)))


def _mla_weights(w_in, q_norm, kv_norm, w_uq, w_ukv, w_o):
    rank = A_Q_RANK + A_KV_RANK
    k_r = w_in[:, rank:]
    uq = w_uq.reshape(A_Q_RANK, A_HEADS, A_NOPE + A_ROPE)
    ukv = w_ukv.reshape(A_KV_RANK, A_HEADS, A_NOPE + A_V)
    return {
        "w_in": jnp.concatenate([w_in[:, :rank], k_r, k_r], axis=1).astype(BF16),
        "q_norm": q_norm, "kv_norm": kv_norm,
        "w_uq": jnp.concatenate([uq[:, :, :A_NOPE].reshape(A_Q_RANK, -1),
                                 uq[:, :, A_NOPE:].reshape(A_Q_RANK, -1)], axis=1).astype(BF16),
        "w_ukv": jnp.concatenate([ukv[:, :, :A_NOPE].reshape(A_KV_RANK, -1),
                                  ukv[:, :, A_NOPE:].reshape(A_KV_RANK, -1)], axis=1).astype(BF16),
        "w_o": w_o.astype(BF16),
    }


def kernel(x, c, ctx, c_ctx, norm_g, w_mod, b_mod, ffn1_w13, ffn1_w2, ffn2_w13, ffn2_w2, a_w_in, a_q_norm, a_kv_norm, a_w_uq, a_w_ukv, a_w_o, b_w_qkv, b_q_norm, b_k_norm, b_w_o, c_w_qkv, c_rpb, c_w_o, final_norm_g):
    b, t, d = x.shape
    depth = w_mod.shape[0]
    rows = t // GRID_W
    assert b < MOD_ROWS and t % (NB_ROWS * GRID_W) == 0 and rows >= NB_KEY_ROWS + NB_ROWS

    c_rows = jnp.zeros((MOD_ROWS, d), F32).at[:b].set(c).at[b].set(c_ctx)
    mod_all = _modulation(c_rows, w_mod, b_mod).reshape(depth, MOD_ROWS, N_MOD, d)
    lat_row = lambda bi: bi
    ctx_row = lambda bi: b

    rope_a = _rope_tables(rows, A_ROPE)
    rope_b = _rope_tables(rows, B_HEAD_DIM)

    w13_1, w2_1 = ffn1_w13.astype(BF16), ffn1_w2.astype(BF16)
    w13_2, w2_2 = ffn2_w13.astype(BF16), ffn2_w2.astype(BF16)

    xc = ctx
    for i in range(depth):
        ctx_out = i < depth - 1
        last = i == depth - 1
        mod = mod_all[i]
        kind, j = i % N_MIXERS, i // N_MIXERS
        x = _ffn(x, mod, lat_row, norm_g[i, 0], w13_1, w2_1, i, k0=0)
        xc = _ffn(xc, mod, ctx_row, norm_g[i, 0], w13_1, w2_1, i, k0=0)

        if kind == 0:
            w = _mla_weights(a_w_in[j], a_q_norm[j], a_kv_norm[j], a_w_uq[j], a_w_ukv[j], a_w_o[j])
            q, kn, kr, vt = _mla_proj(x, mod, lat_row, norm_g[i, 1], w, rope_a)
            qc, knc, krc, vtc = _mla_proj(xc, mod, ctx_row, norm_g[i, 1], w, None)
            cfg = dict(heads=A_HEADS, hps=A_HEADS_PER_STEP, per_head=(True, False), share_kv=False,
                       key_chunk=MLA_KEY_CHUNK)
            o = _attention(q, [[knc, krc], [kn, kr]], [vtc, vt], **cfg)
            if ctx_out:
                oc = _attention(qc, [[knc, krc]], [vtc], **dict(cfg, hps=A_HEADS))
            w_o = w["w_o"]
        elif kind == 1:
            w = {"w_qkv": b_w_qkv[j].astype(BF16), "q_norm": b_q_norm[j], "k_norm": b_k_norm[j]}
            q, k, vt = _gqa_proj(x, mod, lat_row, norm_g[i, 1], w, rope_b)
            qc, kc, vtc = _gqa_proj(xc, mod, ctx_row, norm_g[i, 1], w, None)
            cfg = dict(heads=B_HEADS, hps=B_HEADS // B_KV_HEADS, per_head=(False,), share_kv=True,
                       key_chunk=GQA_KEY_CHUNK)
            o = _attention(q, [[kc], [k]], [vtc, vt], **cfg)
            if ctx_out:
                oc = _attention(qc, [[kc]], [vtc], **cfg)
            w_o = b_w_o[j].astype(BF16)
        else:
            w = {"w_qkv": c_w_qkv[j].astype(BF16)}
            q, k, vt = _nb_proj(x, mod, lat_row, norm_g[i, 1], w)
            qc, kc, vtc = _nb_proj(xc, mod, ctx_row, norm_g[i, 1], w)
            o = _nb_attention(q, kc, vtc, k, vt, _nb_bias_table(c_rpb[j], rows))
            if ctx_out:
                oc = _nb_attention(qc, kc, vtc)
            w_o = c_w_o[j].astype(BF16)

        x = _ffn(x, mod, lat_row, norm_g[i, 2], w13_2, w2_2, i, k0=6, attn=o, w_o=w_o,
                 final_g=final_norm_g if last else None)
        if ctx_out:
            xc = _ffn(xc, mod, ctx_row, norm_g[i, 2], w13_2, w2_2, i, k0=6, attn=oc, w_o=w_o)
    return x
```

```python
import functools

import jax
import jax.numpy as jnp
from jax import lax
from jax.experimental import pallas as pl
from jax.experimental.pallas import tpu as pltpu

F32 = jnp.float32
BF16 = jnp.bfloat16

GRID_W = 64
N_MIXERS = 3
N_MOD = 9
EPS = 1e-6
ROPE_THETA = 10000.0

A_HEADS, A_NOPE, A_ROPE, A_V = 8, 128, 64, 128
A_Q_RANK, A_KV_RANK = 384, 256
B_HEADS, B_KV_HEADS, B_HEAD_DIM = 8, 2, 128
C_HEADS, C_HEAD_DIM, C_WIN_ROWS, C_WIN_COLS = 16, 64, 8, 16

LANES = 128
MOD_ROWS = 16
MASK_VALUE = -1e30
LOG2E = 1.4426950408889634
VMEM_LIMIT = 56 * 1024 * 1024

ROW_TILE = 512
GQA_ROW_TILE = 256
A_HEADS_PER_STEP = 2
Q_TILE = 512
GQA_KEY_CHUNK = 512
MLA_KEY_CHUNK = 0
NB_CHUNK = 256
NB_ROWS = 4
NB_PAIRS_PER_STEP = 2
NB_KEY_ROWS = 12


def _params(*sem):
    return pltpu.CompilerParams(dimension_semantics=sem, vmem_limit_bytes=VMEM_LIMIT)


def _const_spec(shape):
    nd = len(shape)
    return pl.BlockSpec(shape, lambda *_: (0,) * nd, pipeline_mode=pl.Buffered(1))


def _dot(a, b):
    return jnp.dot(a, b, preferred_element_type=F32)


def _dot_nt(a, b):
    return lax.dot_general(a, b, (((1,), (1,)), ((), ())), preferred_element_type=F32)


def _rms(x):
    return x * lax.rsqrt(jnp.mean(x * x, axis=-1, keepdims=True) + EPS)


def _modulated_norm(x, g, scale, shift):
    return _rms(x) * (g * (1.0 + scale)) + shift


def _silu(x):
    return x / (1.0 + jnp.exp(-x))


def _mod_kernel(c_ref, w_ref, b_ref, o_ref):
    sc = _silu(c_ref[...]).astype(BF16)
    o_ref[0] = _dot(sc, w_ref[0].astype(BF16)) + b_ref[0]


def _modulation(c_rows, w_mod, b_mod):
    depth, d, n = w_mod.shape
    tn = d
    return pl.pallas_call(
        _mod_kernel,
        grid=(depth, n // tn),
        in_specs=[
            pl.BlockSpec((MOD_ROWS, d), lambda i, j: (0, 0)),
            pl.BlockSpec((1, d, tn), lambda i, j: (i, 0, j)),
            pl.BlockSpec((1, 1, tn), lambda i, j: (i, 0, j)),
        ],
        out_specs=pl.BlockSpec((1, MOD_ROWS, tn), lambda i, j: (i, 0, j)),
        out_shape=jax.ShapeDtypeStruct((depth, MOD_ROWS, n), F32),
        compiler_params=_params("parallel", "parallel"),
        name="modulation",
    )(c_rows, w_mod, b_mod.reshape(depth, 1, n))


def _ffn_kernel(*refs, k0, d_ff, has_oproj, final_norm):
    it = iter(refs)
    x_ref, mod_ref, g_ref, w13_ref, w2_ref = (next(it) for _ in range(5))
    if has_oproj:
        a_ref, wo_ref = next(it), next(it)
    if final_norm:
        fg_ref = next(it)
    out_ref = next(it)

    x = x_ref[0]
    mod = mod_ref[0]
    if has_oproj:
        x = x + mod[5:6] * _dot(a_ref[0], wo_ref[...])
    h = _modulated_norm(x, g_ref[...], mod[k0 + 1:k0 + 2], mod[k0:k0 + 1]).astype(BF16)
    hgu = _dot(h, w13_ref[...])
    act = (_silu(hgu[:, :d_ff]) * hgu[:, d_ff:]).astype(BF16)
    y = x + (0.5 * mod[k0 + 2:k0 + 3]) * _dot(act, w2_ref[...])
    if final_norm:
        y = _rms(y) * fg_ref[...]
    out_ref[0] = y


def _ffn(x, mod, mod_row, g, w13, w2, layer, *, k0, attn=None, w_o=None, final_g=None):
    b, t, d = x.shape
    d_ff = w2.shape[1]
    tm = min(ROW_TILE, t)
    row = lambda bi, ti: (bi, ti, 0)
    this_layer = lambda *_: (layer, 0, 0)
    in_specs = [
        pl.BlockSpec((1, tm, d), row),
        pl.BlockSpec((1, N_MOD, d), lambda bi, ti: (mod_row(bi), 0, 0)),
        _const_spec((1, d)),
        pl.BlockSpec((None,) + w13.shape[1:], this_layer, pipeline_mode=pl.Buffered(1)),
        pl.BlockSpec((None,) + w2.shape[1:], this_layer, pipeline_mode=pl.Buffered(1)),
    ]
    args = [x, mod, g.reshape(1, d), w13, w2]
    if attn is not None:
        in_specs += [pl.BlockSpec((1, tm, attn.shape[2]), row), _const_spec(w_o.shape)]
        args += [attn, w_o]
    if final_g is not None:
        in_specs.append(_const_spec((1, d)))
        args.append(final_g.reshape(1, d))
    return pl.pallas_call(
        functools.partial(_ffn_kernel, k0=k0, d_ff=d_ff, has_oproj=attn is not None,
                          final_norm=final_g is not None),
        grid=(b, t // tm),
        in_specs=in_specs,
        out_specs=pl.BlockSpec((1, tm, d), row),
        out_shape=jax.ShapeDtypeStruct((b, t, d), F32),
        compiler_params=_params("parallel", "parallel"),
        name="half_ffn",
    )(*args)


def _rope_pairs(x, cos, sin_signed, half):
    if 2 * half == LANES:
        rot = pltpu.roll(x, half, 1)
    else:
        lane = lax.broadcasted_iota(jnp.int32, x.shape, 1)
        first = (lane % (2 * half)) < half
        rot = jnp.where(first, pltpu.roll(x, LANES - half, 1), pltpu.roll(x, half, 1))
    return x * cos + rot * sin_signed


def _mla_proj_kernel(*refs, use_rope, sm_scale):
    it = iter(refs)
    x_ref, mod_ref, g_ref, w_in_ref, qg_ref, kvg_ref, w_uq_ref, w_ukv_ref = (next(it) for _ in range(8))
    if use_rope:
        cos_ref, sin_ref = next(it), next(it)
    q_ref, kn_ref, kr_ref, vt_ref = (next(it) for _ in range(4))

    mod = mod_ref[0]
    h = _modulated_norm(x_ref[0], g_ref[...], mod[4:5], mod[3:4]).astype(BF16)
    proj = _dot(h, w_in_ref[...])
    c_q = (_rms(proj[:, :A_Q_RANK]) * qg_ref[...]).astype(BF16)
    c_kv = (_rms(proj[:, A_Q_RANK:A_Q_RANK + A_KV_RANK]) * kvg_ref[...]).astype(BF16)
    k_r = proj[:, A_Q_RANK + A_KV_RANK:]
    q_all = _dot(c_q, w_uq_ref[...])
    kv = _dot(c_kv, w_ukv_ref[...])
    n_nope = A_HEADS * A_NOPE
    if use_rope:
        cos, sin = cos_ref[...], sin_ref[...]
        k_r = _rope_pairs(k_r, cos, sin, A_ROPE // 2)
    lane = lax.broadcasted_iota(jnp.int32, (1, LANES), 1)
    for j in range(A_HEADS // 2):
        qr = q_all[:, n_nope + j * LANES:n_nope + (j + 1) * LANES]
        if use_rope:
            qr = _rope_pairs(qr, cos, sin, A_ROPE // 2)
        qr = qr * sm_scale
        for e in range(2):
            hd = 2 * j + e
            keep = (lane < A_ROPE) if e == 0 else (lane >= A_ROPE)
            q_ref[0, :, 2 * hd * LANES:(2 * hd + 1) * LANES] = (
                q_all[:, hd * A_NOPE:(hd + 1) * A_NOPE] * sm_scale).astype(BF16)
            q_ref[0, :, (2 * hd + 1) * LANES:(2 * hd + 2) * LANES] = jnp.where(keep, qr, 0.0).astype(BF16)
    kn_ref[0] = kv[:, :n_nope].astype(BF16)
    kr_ref[0] = k_r.astype(BF16)
    vt_ref[0] = kv[:, n_nope:].T.astype(BF16)


def _mla_proj(x, mod, mod_row, g, w, rope):
    b, t, d = x.shape
    tm = min(ROW_TILE, t)
    row = lambda bi, ti: (bi, ti, 0)
    in_specs = [
        pl.BlockSpec((1, tm, d), row),
        pl.BlockSpec((1, N_MOD, d), lambda bi, ti: (mod_row(bi), 0, 0)),
        _const_spec((1, d)),
        _const_spec(w["w_in"].shape), _const_spec((1, A_Q_RANK)), _const_spec((1, A_KV_RANK)),
        _const_spec(w["w_uq"].shape), _const_spec(w["w_ukv"].shape),
    ]
    args = [x, mod, g.reshape(1, d), w["w_in"], w["q_norm"].reshape(1, -1), w["kv_norm"].reshape(1, -1),
            w["w_uq"], w["w_ukv"]]
    if rope is not None:
        in_specs += [pl.BlockSpec((tm, LANES), lambda bi, ti: (ti, 0))] * 2
        args += list(rope)
    hv = A_HEADS * A_V
    return pl.pallas_call(
        functools.partial(_mla_proj_kernel, use_rope=rope is not None,
                          sm_scale=float((A_NOPE + A_ROPE) ** -0.5 * LOG2E)),
        grid=(b, t // tm),
        in_specs=in_specs,
        out_specs=[
            pl.BlockSpec((1, tm, 2 * LANES * A_HEADS), row),
            pl.BlockSpec((1, tm, A_HEADS * A_NOPE), row),
            pl.BlockSpec((1, tm, LANES), row),
            pl.BlockSpec((1, hv, tm), lambda bi, ti: (bi, 0, ti)),
        ],
        out_shape=[
            jax.ShapeDtypeStruct((b, t, 2 * LANES * A_HEADS), BF16),
            jax.ShapeDtypeStruct((b, t, A_HEADS * A_NOPE), BF16),
            jax.ShapeDtypeStruct((b, t, LANES), BF16),
            jax.ShapeDtypeStruct((b, hv, t), BF16),
        ],
        compiler_params=_params("parallel", "parallel"),
        name="mla_proj",
    )(*args)


def _gqa_proj_kernel(*refs, use_rope, sm_scale):
    it = iter(refs)
    x_ref, mod_ref, g_ref, w_ref, qg_ref, kg_ref = (next(it) for _ in range(6))
    if use_rope:
        cos_ref, sin_ref = next(it), next(it)
    q_ref, k_ref, vt_ref = (next(it) for _ in range(3))

    mod = mod_ref[0]
    h = _modulated_norm(x_ref[0], g_ref[...], mod[4:5], mod[3:4]).astype(BF16)
    proj = _dot(h, w_ref[...])
    qw = B_HEADS * B_HEAD_DIM
    kw = B_KV_HEADS * B_HEAD_DIM
    if use_rope:
        cos, sin = cos_ref[...], sin_ref[...]

    def head(col, gain, scale):
        y = _rms(proj[:, col:col + B_HEAD_DIM]) * gain
        if use_rope:
            y = _rope_pairs(y, cos, sin, B_HEAD_DIM // 2)
        return (y * scale).astype(BF16) if scale != 1.0 else y.astype(BF16)

    for hd in range(B_HEADS):
        q_ref[0, :, hd * B_HEAD_DIM:(hd + 1) * B_HEAD_DIM] = head(hd * B_HEAD_DIM, qg_ref[...], sm_scale)
    for hd in range(B_KV_HEADS):
        k_ref[0, :, hd * B_HEAD_DIM:(hd + 1) * B_HEAD_DIM] = head(qw + hd * B_HEAD_DIM, kg_ref[...], 1.0)
    vt_ref[0] = proj[:, qw + kw:].T.astype(BF16)


def _gqa_proj(x, mod, mod_row, g, w, rope):
    b, t, d = x.shape
    tm = min(GQA_ROW_TILE, t)
    row = lambda bi, ti: (bi, ti, 0)
    qw, kw = B_HEADS * B_HEAD_DIM, B_KV_HEADS * B_HEAD_DIM
    in_specs = [
        pl.BlockSpec((1, tm, d), row),
        pl.BlockSpec((1, N_MOD, d), lambda bi, ti: (mod_row(bi), 0, 0)),
        _const_spec((1, d)),
        _const_spec(w["w_qkv"].shape), _const_spec((1, B_HEAD_DIM)), _const_spec((1, B_HEAD_DIM)),
    ]
    args = [x, mod, g.reshape(1, d), w["w_qkv"], w["q_norm"].reshape(1, -1), w["k_norm"].reshape(1, -1)]
    if rope is not None:
        in_specs += [pl.BlockSpec((tm, LANES), lambda bi, ti: (ti, 0))] * 2
        args += list(rope)
    return pl.pallas_call(
        functools.partial(_gqa_proj_kernel, use_rope=rope is not None, sm_scale=float(B_HEAD_DIM ** -0.5 * LOG2E)),
        grid=(b, t // tm),
        in_specs=in_specs,
        out_specs=[
            pl.BlockSpec((1, tm, qw), row),
            pl.BlockSpec((1, tm, kw), row),
            pl.BlockSpec((1, kw, tm), lambda bi, ti: (bi, 0, ti)),
        ],
        out_shape=[
            jax.ShapeDtypeStruct((b, t, qw), BF16),
            jax.ShapeDtypeStruct((b, t, kw), BF16),
            jax.ShapeDtypeStruct((b, kw, t), BF16),
        ],
        compiler_params=_params("parallel", "parallel"),
        name="gqa_proj",
    )(*args)


def _nb_proj_kernel(x_ref, mod_ref, g_ref, w_ref, q_ref, k_ref, vt_ref, *, sm_scale):
    mod = mod_ref[0]
    h = _modulated_norm(x_ref[0], g_ref[...], mod[4:5], mod[3:4]).astype(BF16)
    proj = _dot(h, w_ref[...])
    hd = C_HEADS * C_HEAD_DIM
    q_ref[0] = (proj[:, :hd] * sm_scale).astype(BF16)
    k_ref[0] = proj[:, hd:2 * hd].astype(BF16)
    vt_ref[0] = proj[:, 2 * hd:].T.astype(BF16)


def _nb_proj(x, mod, mod_row, g, w):
    b, t, d = x.shape
    tm = min(ROW_TILE, t)
    row = lambda bi, ti: (bi, ti, 0)
    hd = C_HEADS * C_HEAD_DIM
    return pl.pallas_call(
        functools.partial(_nb_proj_kernel, sm_scale=float(C_HEAD_DIM ** -0.5 * LOG2E)),
        grid=(b, t // tm),
        in_specs=[
            pl.BlockSpec((1, tm, d), row),
            pl.BlockSpec((1, N_MOD, d), lambda bi, ti: (mod_row(bi), 0, 0)),
            _const_spec((1, d)),
            _const_spec(w["w_qkv"].shape),
        ],
        out_specs=[
            pl.BlockSpec((1, tm, hd), row),
            pl.BlockSpec((1, tm, hd), row),
            pl.BlockSpec((1, hd, tm), lambda bi, ti: (bi, 0, ti)),
        ],
        out_shape=[
            jax.ShapeDtypeStruct((b, t, hd), BF16),
            jax.ShapeDtypeStruct((b, t, hd), BF16),
            jax.ShapeDtypeStruct((b, hd, t), BF16),
        ],
        compiler_params=_params("parallel", "parallel"),
        name="nb_proj",
    )(x, mod, g.reshape(1, d), w["w_qkv"])


def _attn_kernel(*refs, key_rows, per_head, n_kv, hps, dq, q_tile, n_tiles, key_chunk):
    n_groups, k_pieces = len(key_rows), len(per_head)
    it = iter(refs)
    q_ref = next(it)
    k_refs = [[next(it) for _ in range(k_pieces)] for _ in range(n_groups)]
    v_refs = [next(it) for _ in range(n_groups)]
    o_ref = next(it)
    k_scr, v_scr, s_a, s_b, m_a, m_b = (next(it) for _ in range(6))

    r0 = 0
    for gi, rows in enumerate(key_rows):
        for j in range(n_kv):
            for pi in range(k_pieces):
                lanes = slice(j * LANES, (j + 1) * LANES) if per_head[pi] else slice(0, LANES)
                k_scr[j, r0:r0 + rows, pi * LANES:(pi + 1) * LANES] = k_refs[gi][pi][0, :, lanes]
            v_scr[j, :, r0:r0 + rows] = v_refs[gi][0, j * LANES:(j + 1) * LANES, :]
        r0 += rows

    def rows_of(tile):
        return pl.ds(pl.multiple_of(tile * q_tile, q_tile), q_tile)

    def kv_of(head):
        return head if n_kv > 1 else 0

    def scores(tile, head, s_buf, m_buf):
        q = q_ref[0, rows_of(tile), head * dq:(head + 1) * dq]
        s_t = _dot_nt(k_scr[kv_of(head)], q)
        s_buf[...] = s_t
        m_buf[...] = jnp.max(s_t, axis=0, keepdims=True)

    def softmax_pv(tile, head, s_buf, m_buf):
        p = jnp.exp2(s_buf[...] - m_buf[...])
        l = jnp.sum(p, axis=0, keepdims=True)
        o_t = _dot(v_scr[kv_of(head)], p.astype(BF16)) / l
        o_ref[0, rows_of(tile), head * LANES:(head + 1) * LANES] = o_t.T.astype(BF16)

    scores(0, 0, s_a, m_a)
    if hps * n_tiles == 1:
        softmax_pv(0, 0, s_a, m_a)
        return

    s_rows = s_a.shape[0]
    bufs = ((s_a, m_a), (s_b, m_b))

    def phase(tile_n, head_n, s_next, m_next, tile_c, head_c, s_cur, m_cur_buf):
        if not key_chunk:
            scores(tile_n, head_n, s_next, m_next)
            softmax_pv(tile_c, head_c, s_cur, m_cur_buf)
            return
        q = q_ref[0, rows_of(tile_n), head_n * dq:(head_n + 1) * dq]
        m_cur = m_cur_buf[...]
        m8 = l8 = acc = None
        for c0 in range(0, s_rows, key_chunk):
            rows = slice(c0, min(c0 + key_chunk, s_rows))
            s_c = _dot_nt(k_scr[kv_of(head_n), rows, :], q)
            s_next[rows, :] = s_c
            mc = jnp.max(s_c.reshape(-1, 8, q_tile), axis=0)
            m8 = mc if m8 is None else jnp.maximum(m8, mc)
            p = jnp.exp2(s_cur[rows, :] - m_cur)
            lc = jnp.sum(p.reshape(-1, 8, q_tile), axis=0)
            l8 = lc if l8 is None else l8 + lc
            a = _dot(v_scr[kv_of(head_c), :, rows], p.astype(BF16))
            acc = a if acc is None else acc + a
        m_next[...] = jnp.max(m8, axis=0, keepdims=True)
        l = jnp.sum(l8, axis=0, keepdims=True)
        o_ref[0, rows_of(tile_c), head_c * LANES:(head_c + 1) * LANES] = (acc / l).T.astype(BF16)

    def one_tile(tile, last):
        for head in range(hps):
            cur = (tile, head, *bufs[head % 2])
            if head + 1 < hps:
                phase(tile, head + 1, *bufs[(head + 1) % 2], *cur)
            elif not last:
                phase(tile + 1, 0, *bufs[(head + 1) % 2], *cur)
            else:
                softmax_pv(*cur)

    assert hps % 2 == 0

    def body(tile, carry):
        one_tile(tile, False)
        return carry

    lax.fori_loop(0, n_tiles - 1, body, 0)
    one_tile(n_tiles - 1, True)


def _attention(q, k_groups, v_groups, *, heads, hps, per_head, share_kv, key_chunk):
    b, tq, qw = q.shape
    dq = qw // heads
    q_tile = min(Q_TILE, tq)
    n_tiles = tq // q_tile
    n_kv = 1 if share_kv else hps
    assert heads % hps == 0 and (hps * n_tiles == 1 or hps % 2 == 0)
    key_rows = tuple(g[0].shape[1] for g in k_groups)
    k_pieces = len(per_head)
    s_total = sum(key_rows)
    in_specs = [pl.BlockSpec((1, tq, hps * dq), lambda bi, hg: (bi, 0, hg))]
    args = [q]
    for grp in k_groups:
        for pi, arr in enumerate(grp):
            width = n_kv * LANES if per_head[pi] else LANES
            moves = per_head[pi] or share_kv
            in_specs.append(pl.BlockSpec((1, arr.shape[1], width),
                                         lambda bi, hg, moves=moves: (bi, 0, hg if moves else 0)))
            args.append(arr)
    for arr in v_groups:
        in_specs.append(pl.BlockSpec((1, n_kv * LANES, arr.shape[2]), lambda bi, hg: (bi, hg, 0)))
        args.append(arr)
    return pl.pallas_call(
        functools.partial(_attn_kernel, key_rows=key_rows, per_head=tuple(per_head), n_kv=n_kv, hps=hps,
                          dq=dq, q_tile=q_tile, n_tiles=n_tiles, key_chunk=key_chunk),
        grid=(b, heads // hps),
        in_specs=in_specs,
        out_specs=pl.BlockSpec((1, tq, hps * LANES), lambda bi, hg: (bi, 0, hg)),
        out_shape=jax.ShapeDtypeStruct((b, tq, heads * LANES), BF16),
        scratch_shapes=[
            pltpu.VMEM((n_kv, s_total, k_pieces * LANES), BF16), pltpu.VMEM((n_kv, LANES, s_total), BF16),
            pltpu.VMEM((s_total, q_tile), F32), pltpu.VMEM((s_total, q_tile), F32),
            pltpu.VMEM((1, q_tile), F32), pltpu.VMEM((1, q_tile), F32),
        ],
        compiler_params=_params("parallel", "parallel"),
        name="attention",
    )(*args)


def _nb_attn_kernel(*refs, has_window, rows, tile, n_groups, pps):
    it = iter(refs)
    q_ref, kc_ref, vc_ref = next(it), next(it), next(it)
    if has_window:
        kl_ref, vl_ref, bias_ref = next(it), next(it), next(it)
    o_ref = next(it)
    s_a, s_b, m_a, m_b = (next(it) for _ in range(4))
    bufs = ((s_a, m_a), (s_b, m_b))
    c = kc_ref.shape[1]
    win = NB_KEY_ROWS * GRID_W
    lane = lax.broadcasted_iota(jnp.int32, (1, LANES), 1)

    def tile_rows(g):
        return pl.ds(pl.multiple_of(g * tile, tile), tile)

    def lanes_of(j):
        return slice(j * LANES, (j + 1) * LANES)

    def window_start(g):
        first_row = jnp.clip(NB_ROWS * g - C_WIN_ROWS // 2, 0, rows - NB_KEY_ROWS)
        return pl.multiple_of(first_row * GRID_W, 2 * LANES)

    def two_heads(g, j):
        q = q_ref[0, tile_rows(g), lanes_of(j)]
        zero = jnp.zeros_like(q)
        return jnp.concatenate([jnp.where(lane < C_HEAD_DIM, q, zero), jnp.where(lane >= C_HEAD_DIM, q, zero)],
                               axis=0)

    def finish(o2, l, g, j):
        o_t = jnp.concatenate([o2[e * C_HEAD_DIM:(e + 1) * C_HEAD_DIM, e * tile:(e + 1) * tile]
                               / l[:, e * tile:(e + 1) * tile] for e in range(2)], axis=0)
        o_ref[0, tile_rows(g), lanes_of(j)] = o_t.T.astype(BF16)

    def chunks(g_next, j_next, g_cur, j_cur):
        yield kc_ref[0, :, lanes_of(j_next)], None, slice(0, c), vc_ref[0, lanes_of(j_cur), :]
        if has_window:
            variant = jnp.where(g_next == 0, 0, jnp.where(g_next == n_groups - 1, 2, 1))
            w_next, w_cur = window_start(g_next), window_start(g_cur)
            for c0 in range(0, win, NB_CHUNK):
                bias = [bias_ref[variant, 2 * j_next + e, c0:c0 + NB_CHUNK, :] for e in range(2)]
                yield (kl_ref[0, pl.ds(w_next + c0, NB_CHUNK), lanes_of(j_next)], bias,
                       slice(c + c0, c + c0 + NB_CHUNK), vl_ref[0, lanes_of(j_cur), pl.ds(w_cur + c0, NB_CHUNK)])

    def phase(g_next, j_next, s_next, m_next, g_cur, j_cur, s_cur, m_cur_buf):
        q2 = two_heads(g_next, j_next) if g_next is not None else None
        m_cur = m_cur_buf[...] if g_cur is not None else None
        m8 = l8 = o2 = None
        gn = g_next if g_next is not None else g_cur
        gc = g_cur if g_cur is not None else g_next
        for k, bias, rws, v_t in chunks(gn, j_next if g_next is not None else j_cur,
                                        gc, j_cur if g_cur is not None else j_next):
            if g_next is not None:
                s_k = _dot_nt(k, q2)
                if bias is not None:
                    s_k = jnp.concatenate([s_k[:, e * tile:(e + 1) * tile] + bias[e] for e in range(2)], axis=1)
                s_next[rws, :] = s_k
                mk = jnp.max(s_k.reshape(-1, 8, 2 * tile), axis=0)
                m8 = mk if m8 is None else jnp.maximum(m8, mk)
            if g_cur is not None:
                p = jnp.exp2(s_cur[rws, :] - m_cur)
                lk = jnp.sum(p.reshape(-1, 8, 2 * tile), axis=0)
                l8 = lk if l8 is None else l8 + lk
                ok = _dot(v_t, p.astype(BF16))
                o2 = ok if o2 is None else o2 + ok
        if g_next is not None:
            m_next[...] = jnp.max(m8, axis=0, keepdims=True)
        if g_cur is not None:
            finish(o2, jnp.sum(l8, axis=0, keepdims=True), g_cur, j_cur)

    def one_group(g, last):
        for j in range(pps):
            cur = (g, j, *bufs[j % 2])
            nxt_bufs = bufs[(j + 1) % 2]
            if j + 1 < pps:
                phase(g, j + 1, *nxt_bufs, *cur)
            elif not last:
                phase(g + 1, 0, *nxt_bufs, *cur)
            else:
                phase(None, None, None, None, *cur)

    assert pps % 2 == 0
    phase(0, 0, *bufs[0], None, None, None, None)

    def body(g, carry):
        one_group(g, False)
        return carry

    lax.fori_loop(0, n_groups - 1, body, 0)
    one_group(n_groups - 1, True)


def _nb_attention(q, k_ctx, vt_ctx, k_lat=None, vt_lat=None, bias=None):
    b, tq, hw = q.shape
    pairs = hw // LANES
    has_window = k_lat is not None
    c = k_ctx.shape[1]
    if has_window:
        t = k_lat.shape[1]
        rows = t // GRID_W
        tile = NB_ROWS * GRID_W
        keys = c + NB_KEY_ROWS * GRID_W
        pps = NB_PAIRS_PER_STEP
    else:
        rows, tile, keys, pps = 0, tq, c, pairs
    n_groups = tq // tile
    w = pps * LANES
    in_specs = [
        pl.BlockSpec((1, tq, w), lambda p, bi: (bi, 0, p)),
        pl.BlockSpec((1, c, w), lambda p, bi: (bi, 0, p)),
        pl.BlockSpec((1, w, c), lambda p, bi: (bi, p, 0)),
    ]
    args = [q, k_ctx, vt_ctx]
    if has_window:
        in_specs += [
            pl.BlockSpec((1, t, w), lambda p, bi: (bi, 0, p)),
            pl.BlockSpec((1, w, t), lambda p, bi: (bi, p, 0)),
            pl.BlockSpec((3, 2 * pps, NB_KEY_ROWS * GRID_W, tile), lambda p, bi: (0, p, 0, 0),
                         pipeline_mode=pl.Buffered(1)),
        ]
        args += [k_lat, vt_lat, bias]
    return pl.pallas_call(
        functools.partial(_nb_attn_kernel, has_window=has_window, rows=rows, tile=tile, n_groups=n_groups, pps=pps),
        grid=(pairs // pps, b),
        in_specs=in_specs,
        out_specs=pl.BlockSpec((1, tq, w), lambda p, bi: (bi, 0, p)),
        out_shape=jax.ShapeDtypeStruct((b, tq, hw), BF16),
        scratch_shapes=[pltpu.VMEM((keys, 2 * tile), F32), pltpu.VMEM((keys, 2 * tile), F32),
                        pltpu.VMEM((1, 2 * tile), F32), pltpu.VMEM((1, 2 * tile), F32)],
        compiler_params=_params("parallel", "parallel"),
        name="nb_attention",
    )(*args)


def _nb_bias_table(rpb, rows):
    heads = rpb.shape[0]
    tile_q = NB_ROWS * GRID_W
    return pl.pallas_call(
        functools.partial(_nb_bias_kernel, rows=rows),
        grid=(heads,),
        in_specs=[pl.BlockSpec(memory_space=pltpu.SMEM)],
        out_specs=pl.BlockSpec((3, 1, NB_KEY_ROWS * GRID_W, tile_q), lambda h: (0, h, 0, 0)),
        out_shape=jax.ShapeDtypeStruct((3, heads, NB_KEY_ROWS * GRID_W, tile_q), F32),
        compiler_params=_params("parallel"),
        name="nb_bias",
    )(rpb.reshape(-1))


def _nb_bias_kernel(rpb_ref, o_ref, *, rows):
    n_a, n_b = 2 * C_WIN_ROWS - 1, 2 * C_WIN_COLS - 1
    tile_q = NB_ROWS * GRID_W
    shape = (GRID_W, tile_q)
    kc = lax.broadcasted_iota(jnp.int32, shape, 0)
    lane = lax.broadcasted_iota(jnp.int32, shape, 1)
    qc = lane % GRID_W
    qi = lane // GRID_W
    c0 = jnp.clip(qc - C_WIN_COLS // 2, 0, GRID_W - C_WIN_COLS)
    col_ok = (kc >= c0) & (kc < c0 + C_WIN_COLS)
    dcol = kc - qc + (C_WIN_COLS - 1)
    base = pl.program_id(0) * (n_a * n_b)
    masked = jnp.full(shape, MASK_VALUE, F32)
    planes = []
    for a in range(n_a):
        acc = masked
        for bb in range(n_b):
            acc = jnp.where(dcol == bb, rpb_ref[base + a * n_b + bb], acc)
        planes.append(jnp.where(col_ok, acc * LOG2E, MASK_VALUE))
    groups = rows // NB_ROWS
    for v, g in enumerate((0, 1, groups - 1)):
        first_key_row = min(max(NB_ROWS * g - C_WIN_ROWS // 2, 0), rows - NB_KEY_ROWS)
        for j in range(NB_KEY_ROWS):
            kr = first_key_row + j
            blk = masked
            for i in range(NB_ROWS):
                qr = NB_ROWS * g + i
                r0 = min(max(qr - C_WIN_ROWS // 2, 0), rows - C_WIN_ROWS)
                if r0 <= kr < r0 + C_WIN_ROWS:
                    blk = jnp.where(qi == i, planes[kr - qr + C_WIN_ROWS - 1], blk)
            o_ref[v, 0, j * GRID_W:(j + 1) * GRID_W, :] = blk


def _rope_tables(rows, rot_dim):
    n = rot_dim // 4
    inv_freq = ROPE_THETA ** (-jnp.arange(n, dtype=F32) / n)
    t = jnp.arange(rows * GRID_W, dtype=jnp.int32)
    r = (t // GRID_W).astype(F32)
    col = (t % GRID_W).astype(F32)
    ang = jnp.concatenate([r[:, None] * inv_freq[None, :], col[:, None] * inv_freq[None, :]], axis=-1)
    cos, sin = jnp.cos(ang), jnp.sin(ang)
    reps = LANES // rot_dim
    return (jnp.tile(jnp.concatenate([cos, cos], axis=-1), (1, reps)),
            jnp.tile(jnp.concatenate([-sin, sin], axis=-1), (1, reps)))


def _mla_weights(w_in, q_norm, kv_norm, w_uq, w_ukv, w_o):
    rank = A_Q_RANK + A_KV_RANK
    k_r = w_in[:, rank:]
    uq = w_uq.reshape(A_Q_RANK, A_HEADS, A_NOPE + A_ROPE)
    ukv = w_ukv.reshape(A_KV_RANK, A_HEADS, A_NOPE + A_V)
    return {
        "w_in": jnp.concatenate([w_in[:, :rank], k_r, k_r], axis=1).astype(BF16),
        "q_norm": q_norm, "kv_norm": kv_norm,
        "w_uq": jnp.concatenate([uq[:, :, :A_NOPE].reshape(A_Q_RANK, -1),
                                 uq[:, :, A_NOPE:].reshape(A_Q_RANK, -1)], axis=1).astype(BF16),
        "w_ukv": jnp.concatenate([ukv[:, :, :A_NOPE].reshape(A_KV_RANK, -1),
                                  ukv[:, :, A_NOPE:].reshape(A_KV_RANK, -1)], axis=1).astype(BF16),
        "w_o": w_o.astype(BF16),
    }


def kernel(x, c, ctx, c_ctx, norm_g, w_mod, b_mod, ffn1_w13, ffn1_w2, ffn2_w13, ffn2_w2, a_w_in, a_q_norm, a_kv_norm, a_w_uq, a_w_ukv, a_w_o, b_w_qkv, b_q_norm, b_k_norm, b_w_o, c_w_qkv, c_rpb, c_w_o, final_norm_g):
    b, t, d = x.shape
    depth = w_mod.shape[0]
    rows = t // GRID_W
    assert b < MOD_ROWS and t % (NB_ROWS * GRID_W) == 0 and rows >= NB_KEY_ROWS + NB_ROWS

    c_rows = jnp.zeros((MOD_ROWS, d), F32).at[:b].set(c).at[b].set(c_ctx)
    mod_all = _modulation(c_rows, w_mod, b_mod).reshape(depth, MOD_ROWS, N_MOD, d)
    lat_row = lambda bi: bi
    ctx_row = lambda bi: b

    rope_a = _rope_tables(rows, A_ROPE)
    rope_b = _rope_tables(rows, B_HEAD_DIM)

    w13_1, w2_1 = ffn1_w13.astype(BF16), ffn1_w2.astype(BF16)
    w13_2, w2_2 = ffn2_w13.astype(BF16), ffn2_w2.astype(BF16)

    xc = ctx
    for i in range(depth):
        ctx_out = i < depth - 1
        last = i == depth - 1
        mod = mod_all[i]
        kind, j = i % N_MIXERS, i // N_MIXERS
        x = _ffn(x, mod, lat_row, norm_g[i, 0], w13_1, w2_1, i, k0=0)
        xc = _ffn(xc, mod, ctx_row, norm_g[i, 0], w13_1, w2_1, i, k0=0)

        if kind == 0:
            w = _mla_weights(a_w_in[j], a_q_norm[j], a_kv_norm[j], a_w_uq[j], a_w_ukv[j], a_w_o[j])
            q, kn, kr, vt = _mla_proj(x, mod, lat_row, norm_g[i, 1], w, rope_a)
            qc, knc, krc, vtc = _mla_proj(xc, mod, ctx_row, norm_g[i, 1], w, None)
            cfg = dict(heads=A_HEADS, hps=A_HEADS_PER_STEP, per_head=(True, False), share_kv=False,
                       key_chunk=MLA_KEY_CHUNK)
            o = _attention(q, [[knc, krc], [kn, kr]], [vtc, vt], **cfg)
            if ctx_out:
                oc = _attention(qc, [[knc, krc]], [vtc], **dict(cfg, hps=A_HEADS))
            w_o = w["w_o"]
        elif kind == 1:
            w = {"w_qkv": b_w_qkv[j].astype(BF16), "q_norm": b_q_norm[j], "k_norm": b_k_norm[j]}
            q, k, vt = _gqa_proj(x, mod, lat_row, norm_g[i, 1], w, rope_b)
            qc, kc, vtc = _gqa_proj(xc, mod, ctx_row, norm_g[i, 1], w, None)
            cfg = dict(heads=B_HEADS, hps=B_HEADS // B_KV_HEADS, per_head=(False,), share_kv=True,
                       key_chunk=GQA_KEY_CHUNK)
            o = _attention(q, [[kc], [k]], [vtc, vt], **cfg)
            if ctx_out:
                oc = _attention(qc, [[kc]], [vtc], **cfg)
            w_o = b_w_o[j].astype(BF16)
        else:
            w = {"w_qkv": c_w_qkv[j].astype(BF16)}
            q, k, vt = _nb_proj(x, mod, lat_row, norm_g[i, 1], w)
            qc, kc, vtc = _nb_proj(xc, mod, ctx_row, norm_g[i, 1], w)
            o = _nb_attention(q, kc, vtc, k, vt, _nb_bias_table(c_rpb[j], rows))
            if ctx_out:
                oc = _nb_attention(qc, kc, vtc)
            w_o = c_w_o[j].astype(BF16)

        x = _ffn(x, mod, lat_row, norm_g[i, 2], w13_2, w2_2, i, k0=6, attn=o, w_o=w_o,
                 final_g=final_norm_g if last else None)
        if ctx_out:
            xc = _ffn(xc, mod, ctx_row, norm_g[i, 2], w13_2, w2_2, i, k0=6, attn=oc, w_o=w_o)
    return x
```

```python
import functools

import jax
import jax.numpy as jnp
from jax import lax
from jax.experimental import pallas as pl
from jax.experimental.pallas import tpu as pltpu

F32 = jnp.float32
BF16 = jnp.bfloat16

GRID_W = 64
N_MIXERS = 3
N_MOD = 9
EPS = 1e-6
ROPE_THETA = 10000.0

A_HEADS, A_NOPE, A_ROPE, A_V = 8, 128, 64, 128
A_Q_RANK, A_KV_RANK = 384, 256
B_HEADS, B_KV_HEADS, B_HEAD_DIM = 8, 2, 128
C_HEADS, C_HEAD_DIM, C_WIN_ROWS, C_WIN_COLS = 16, 64, 8, 16

LANES = 128
MOD_ROWS = 16
MASK_VALUE = -1e30
LOG2E = 1.4426950408889634
VMEM_LIMIT = 56 * 1024 * 1024

ROW_TILE = 512
GQA_ROW_TILE = 256
A_HEADS_PER_STEP = 2
Q_TILE = 512
GQA_KEY_CHUNK = 512
MLA_KEY_CHUNK = 0
NB_CHUNK = 256
NB_ROWS = 4
NB_PAIRS_PER_STEP = 2
NB_KEY_ROWS = 12


def _params(*sem):
    return pltpu.CompilerParams(dimension_semantics=sem, vmem_limit_bytes=VMEM_LIMIT)


def _const_spec(shape):
    nd = len(shape)
    return pl.BlockSpec(shape, lambda *_: (0,) * nd, pipeline_mode=pl.Buffered(1))


def _dot(a, b):
    return jnp.dot(a, b, preferred_element_type=F32)


def _dot_nt(a, b):
    return lax.dot_general(a, b, (((1,), (1,)), ((), ())), preferred_element_type=F32)


def _rms(x):
    return x * lax.rsqrt(jnp.mean(x * x, axis=-1, keepdims=True) + EPS)


def _modulated_norm(x, g, scale, shift):
    return _rms(x) * (g * (1.0 + scale)) + shift


def _silu(x):
    return x / (1.0 + jnp.exp(-x))


def _mod_kernel(c_ref, w_ref, b_ref, o_ref):
    sc = _silu(c_ref[...]).astype(BF16)
    o_ref[0] = _dot(sc, w_ref[0].astype(BF16)) + b_ref[0]


def _modulation(c_rows, w_mod, b_mod):
    depth, d, n = w_mod.shape
    tn = d
    return pl.pallas_call(
        _mod_kernel,
        grid=(depth, n // tn),
        in_specs=[
            pl.BlockSpec((MOD_ROWS, d), lambda i, j: (0, 0)),
            pl.BlockSpec((1, d, tn), lambda i, j: (i, 0, j)),
            pl.BlockSpec((1, 1, tn), lambda i, j: (i, 0, j)),
        ],
        out_specs=pl.BlockSpec((1, MOD_ROWS, tn), lambda i, j: (i, 0, j)),
        out_shape=jax.ShapeDtypeStruct((depth, MOD_ROWS, n), F32),
        compiler_params=_params("parallel", "parallel"),
        name="modulation",
    )(c_rows, w_mod, b_mod.reshape(depth, 1, n))


def _ffn_kernel(*refs, k0, d_ff, has_oproj, final_norm):
    it = iter(refs)
    x_ref, mod_ref, g_ref, w13_ref, w2_ref = (next(it) for _ in range(5))
    if has_oproj:
        a_ref, wo_ref = next(it), next(it)
    if final_norm:
        fg_ref = next(it)
    out_ref = next(it)

    x = x_ref[0]
    mod = mod_ref[0]
    if has_oproj:
        x = x + mod[5:6] * _dot(a_ref[0], wo_ref[...])
    h = _modulated_norm(x, g_ref[...], mod[k0 + 1:k0 + 2], mod[k0:k0 + 1]).astype(BF16)
    hgu = _dot(h, w13_ref[...])
    act = (_silu(hgu[:, :d_ff]) * hgu[:, d_ff:]).astype(BF16)
    y = x + (0.5 * mod[k0 + 2:k0 + 3]) * _dot(act, w2_ref[...])
    if final_norm:
        y = _rms(y) * fg_ref[...]
    out_ref[0] = y


def _ffn(x, mod, mod_row, g, w13, w2, layer, *, k0, attn=None, w_o=None, final_g=None):
    b, t, d = x.shape
    d_ff = w2.shape[1]
    tm = min(ROW_TILE, t)
    row = lambda bi, ti: (bi, ti, 0)
    this_layer = lambda *_: (layer, 0, 0)
    in_specs = [
        pl.BlockSpec((1, tm, d), row),
        pl.BlockSpec((1, N_MOD, d), lambda bi, ti: (mod_row(bi), 0, 0)),
        _const_spec((1, d)),
        pl.BlockSpec((None,) + w13.shape[1:], this_layer, pipeline_mode=pl.Buffered(1)),
        pl.BlockSpec((None,) + w2.shape[1:], this_layer, pipeline_mode=pl.Buffered(1)),
    ]
    args = [x, mod, g.reshape(1, d), w13, w2]
    if attn is not None:
        in_specs += [pl.BlockSpec((1, tm, attn.shape[2]), row), _const_spec(w_o.shape)]
        args += [attn, w_o]
    if final_g is not None:
        in_specs.append(_const_spec((1, d)))
        args.append(final_g.reshape(1, d))
    return pl.pallas_call(
        functools.partial(_ffn_kernel, k0=k0, d_ff=d_ff, has_oproj=attn is not None,
                          final_norm=final_g is not None),
        grid=(b, t // tm),
        in_specs=in_specs,
        out_specs=pl.BlockSpec((1, tm, d), row),
        out_shape=jax.ShapeDtypeStruct((b, t, d), F32),
        compiler_params=_params("parallel", "parallel"),
        name="half_ffn",
    )(*args)


def _rope_pairs(x, cos, sin_signed, half):
    if 2 * half == LANES:
        rot = pltpu.roll(x, half, 1)
    else:
        lane = lax.broadcasted_iota(jnp.int32, x.shape, 1)
        first = (lane % (2 * half)) < half
        rot = jnp.where(first, pltpu.roll(x, LANES - half, 1), pltpu.roll(x, half, 1))
    return x * cos + rot * sin_signed


def _mla_proj_kernel(*refs, use_rope, sm_scale):
    it = iter(refs)
    x_ref, mod_ref, g_ref, w_in_ref, qg_ref, kvg_ref, w_uq_ref, w_uk_ref, w_uvt_ref = (next(it) for _ in range(9))
    if use_rope:
        cos_ref, sin_ref = next(it), next(it)
    q_ref, kn_ref, kr_ref, vt_ref = (next(it) for _ in range(4))

    mod = mod_ref[0]
    h = _modulated_norm(x_ref[0], g_ref[...], mod[4:5], mod[3:4]).astype(BF16)
    proj = _dot(h, w_in_ref[...])
    c_q = (_rms(proj[:, :A_Q_RANK]) * qg_ref[...]).astype(BF16)
    c_kv = (_rms(proj[:, A_Q_RANK:A_Q_RANK + A_KV_RANK]) * kvg_ref[...]).astype(BF16)
    k_r = proj[:, A_Q_RANK + A_KV_RANK:]
    q_all = _dot(c_q, w_uq_ref[...])
    n_nope = A_HEADS * A_NOPE
    if use_rope:
        cos, sin = cos_ref[...], sin_ref[...]
        k_r = _rope_pairs(k_r, cos, sin, A_ROPE // 2)
    lane = lax.broadcasted_iota(jnp.int32, (1, LANES), 1)
    for j in range(A_HEADS // 2):
        qr = q_all[:, n_nope + j * LANES:n_nope + (j + 1) * LANES]
        if use_rope:
            qr = _rope_pairs(qr, cos, sin, A_ROPE // 2)
        qr = qr * sm_scale
        for e in range(2):
            hd = 2 * j + e
            keep = (lane < A_ROPE) if e == 0 else (lane >= A_ROPE)
            q_ref[0, :, 2 * hd * LANES:(2 * hd + 1) * LANES] = (
                q_all[:, hd * A_NOPE:(hd + 1) * A_NOPE] * sm_scale).astype(BF16)
            q_ref[0, :, (2 * hd + 1) * LANES:(2 * hd + 2) * LANES] = jnp.where(keep, qr, 0.0).astype(BF16)
    kn_ref[0] = _dot(c_kv, w_uk_ref[...]).astype(BF16)
    kr_ref[0] = k_r.astype(BF16)
    vt_ref[0] = _dot_nt(w_uvt_ref[...], c_kv).astype(BF16)


def _mla_proj(x, mod, mod_row, g, w, rope):
    b, t, d = x.shape
    tm = min(ROW_TILE, t)
    row = lambda bi, ti: (bi, ti, 0)
    in_specs = [
        pl.BlockSpec((1, tm, d), row),
        pl.BlockSpec((1, N_MOD, d), lambda bi, ti: (mod_row(bi), 0, 0)),
        _const_spec((1, d)),
        _const_spec(w["w_in"].shape), _const_spec((1, A_Q_RANK)), _const_spec((1, A_KV_RANK)),
        _const_spec(w["w_uq"].shape), _const_spec(w["w_uk"].shape), _const_spec(w["w_uvt"].shape),
    ]
    args = [x, mod, g.reshape(1, d), w["w_in"], w["q_norm"].reshape(1, -1), w["kv_norm"].reshape(1, -1),
            w["w_uq"], w["w_uk"], w["w_uvt"]]
    if rope is not None:
        in_specs += [pl.BlockSpec((tm, LANES), lambda bi, ti: (ti, 0))] * 2
        args += list(rope)
    hv = A_HEADS * A_V
    return pl.pallas_call(
        functools.partial(_mla_proj_kernel, use_rope=rope is not None,
                          sm_scale=float((A_NOPE + A_ROPE) ** -0.5 * LOG2E)),
        grid=(b, t // tm),
        in_specs=in_specs,
        out_specs=[
            pl.BlockSpec((1, tm, 2 * LANES * A_HEADS), row),
            pl.BlockSpec((1, tm, A_HEADS * A_NOPE), row),
            pl.BlockSpec((1, tm, LANES), row),
            pl.BlockSpec((1, hv, tm), lambda bi, ti: (bi, 0, ti)),
        ],
        out_shape=[
            jax.ShapeDtypeStruct((b, t, 2 * LANES * A_HEADS), BF16),
            jax.ShapeDtypeStruct((b, t, A_HEADS * A_NOPE), BF16),
            jax.ShapeDtypeStruct((b, t, LANES), BF16),
            jax.ShapeDtypeStruct((b, hv, t), BF16),
        ],
        compiler_params=_params("parallel", "parallel"),
        name="mla_proj",
    )(*args)


def _gqa_proj_kernel(*refs, use_rope, sm_scale):
    it = iter(refs)
    x_ref, mod_ref, g_ref, w_ref, w_vt_ref, qg_ref, kg_ref = (next(it) for _ in range(7))
    if use_rope:
        cos_ref, sin_ref = next(it), next(it)
    q_ref, k_ref, vt_ref = (next(it) for _ in range(3))

    mod = mod_ref[0]
    h = _modulated_norm(x_ref[0], g_ref[...], mod[4:5], mod[3:4]).astype(BF16)
    proj = _dot(h, w_ref[...])
    qw = B_HEADS * B_HEAD_DIM
    kw = B_KV_HEADS * B_HEAD_DIM
    if use_rope:
        cos, sin = cos_ref[...], sin_ref[...]

    def head(col, gain, scale):
        y = _rms(proj[:, col:col + B_HEAD_DIM]) * gain
        if use_rope:
            y = _rope_pairs(y, cos, sin, B_HEAD_DIM // 2)
        return (y * scale).astype(BF16) if scale != 1.0 else y.astype(BF16)

    for hd in range(B_HEADS):
        q_ref[0, :, hd * B_HEAD_DIM:(hd + 1) * B_HEAD_DIM] = head(hd * B_HEAD_DIM, qg_ref[...], sm_scale)
    for hd in range(B_KV_HEADS):
        k_ref[0, :, hd * B_HEAD_DIM:(hd + 1) * B_HEAD_DIM] = head(qw + hd * B_HEAD_DIM, kg_ref[...], 1.0)
    vt_ref[0] = _dot_nt(w_vt_ref[...], h).astype(BF16)


def _gqa_proj(x, mod, mod_row, g, w, rope):
    b, t, d = x.shape
    tm = min(GQA_ROW_TILE, t)
    row = lambda bi, ti: (bi, ti, 0)
    qw, kw = B_HEADS * B_HEAD_DIM, B_KV_HEADS * B_HEAD_DIM
    in_specs = [
        pl.BlockSpec((1, tm, d), row),
        pl.BlockSpec((1, N_MOD, d), lambda bi, ti: (mod_row(bi), 0, 0)),
        _const_spec((1, d)),
        _const_spec(w["w_qk"].shape), _const_spec(w["w_vt"].shape),
        _const_spec((1, B_HEAD_DIM)), _const_spec((1, B_HEAD_DIM)),
    ]
    args = [x, mod, g.reshape(1, d), w["w_qk"], w["w_vt"], w["q_norm"].reshape(1, -1), w["k_norm"].reshape(1, -1)]
    if rope is not None:
        in_specs += [pl.BlockSpec((tm, LANES), lambda bi, ti: (ti, 0))] * 2
        args += list(rope)
    return pl.pallas_call(
        functools.partial(_gqa_proj_kernel, use_rope=rope is not None, sm_scale=float(B_HEAD_DIM ** -0.5 * LOG2E)),
        grid=(b, t // tm),
        in_specs=in_specs,
        out_specs=[
            pl.BlockSpec((1, tm, qw), row),
            pl.BlockSpec((1, tm, kw), row),
            pl.BlockSpec((1, kw, tm), lambda bi, ti: (bi, 0, ti)),
        ],
        out_shape=[
            jax.ShapeDtypeStruct((b, t, qw), BF16),
            jax.ShapeDtypeStruct((b, t, kw), BF16),
            jax.ShapeDtypeStruct((b, kw, t), BF16),
        ],
        compiler_params=_params("parallel", "parallel"),
        name="gqa_proj",
    )(*args)


def _nb_proj_kernel(x_ref, mod_ref, g_ref, w_ref, w_vt_ref, q_ref, k_ref, vt_ref, *, sm_scale):
    mod = mod_ref[0]
    h = _modulated_norm(x_ref[0], g_ref[...], mod[4:5], mod[3:4]).astype(BF16)
    proj = _dot(h, w_ref[...])
    hd = C_HEADS * C_HEAD_DIM
    q_ref[0] = (proj[:, :hd] * sm_scale).astype(BF16)
    k_ref[0] = proj[:, hd:2 * hd].astype(BF16)
    vt_ref[0] = _dot_nt(w_vt_ref[...], h).astype(BF16)


def _nb_proj(x, mod, mod_row, g, w):
    b, t, d = x.shape
    tm = min(ROW_TILE, t)
    row = lambda bi, ti: (bi, ti, 0)
    hd = C_HEADS * C_HEAD_DIM
    return pl.pallas_call(
        functools.partial(_nb_proj_kernel, sm_scale=float(C_HEAD_DIM ** -0.5 * LOG2E)),
        grid=(b, t // tm),
        in_specs=[
            pl.BlockSpec((1, tm, d), row),
            pl.BlockSpec((1, N_MOD, d), lambda bi, ti: (mod_row(bi), 0, 0)),
            _const_spec((1, d)),
            _const_spec(w["w_qk"].shape), _const_spec(w["w_vt"].shape),
        ],
        out_specs=[
            pl.BlockSpec((1, tm, hd), row),
            pl.BlockSpec((1, tm, hd), row),
            pl.BlockSpec((1, hd, tm), lambda bi, ti: (bi, 0, ti)),
        ],
        out_shape=[
            jax.ShapeDtypeStruct((b, t, hd), BF16),
            jax.ShapeDtypeStruct((b, t, hd), BF16),
            jax.ShapeDtypeStruct((b, hd, t), BF16),
        ],
        compiler_params=_params("parallel", "parallel"),
        name="nb_proj",
    )(x, mod, g.reshape(1, d), w["w_qk"], w["w_vt"])


def _attn_kernel(*refs, key_rows, per_head, n_kv, hps, dq, q_tile, n_tiles, key_chunk):
    n_groups, k_pieces = len(key_rows), len(per_head)
    it = iter(refs)
    q_ref = next(it)
    k_refs = [[next(it) for _ in range(k_pieces)] for _ in range(n_groups)]
    v_refs = [next(it) for _ in range(n_groups)]
    o_ref = next(it)
    k_scr, v_scr, s_a, s_b, m_a, m_b = (next(it) for _ in range(6))

    r0 = 0
    for gi, rows in enumerate(key_rows):
        for j in range(n_kv):
            for pi in range(k_pieces):
                lanes = slice(j * LANES, (j + 1) * LANES) if per_head[pi] else slice(0, LANES)
                k_scr[j, r0:r0 + rows, pi * LANES:(pi + 1) * LANES] = k_refs[gi][pi][0, :, lanes]
            v_scr[j, :, r0:r0 + rows] = v_refs[gi][0, j * LANES:(j + 1) * LANES, :]
        r0 += rows

    def rows_of(tile):
        return pl.ds(pl.multiple_of(tile * q_tile, q_tile), q_tile)

    def kv_of(head):
        return head if n_kv > 1 else 0

    def scores(tile, head, s_buf, m_buf):
        q = q_ref[0, rows_of(tile), head * dq:(head + 1) * dq]
        s_t = _dot_nt(k_scr[kv_of(head)], q)
        s_buf[...] = s_t
        m_buf[...] = jnp.max(s_t, axis=0, keepdims=True)

    def softmax_pv(tile, head, s_buf, m_buf):
        p = jnp.exp2(s_buf[...] - m_buf[...])
        l = jnp.sum(p, axis=0, keepdims=True)
        o_t = _dot(v_scr[kv_of(head)], p.astype(BF16)) / l
        o_ref[0, rows_of(tile), head * LANES:(head + 1) * LANES] = o_t.T.astype(BF16)

    scores(0, 0, s_a, m_a)
    if hps * n_tiles == 1:
        softmax_pv(0, 0, s_a, m_a)
        return

    s_rows = s_a.shape[0]
    bufs = ((s_a, m_a), (s_b, m_b))

    def phase(tile_n, head_n, s_next, m_next, tile_c, head_c, s_cur, m_cur_buf):
        if not key_chunk:
            scores(tile_n, head_n, s_next, m_next)
            softmax_pv(tile_c, head_c, s_cur, m_cur_buf)
            return
        q = q_ref[0, rows_of(tile_n), head_n * dq:(head_n + 1) * dq]
        m_cur = m_cur_buf[...]
        m8 = l8 = acc = None
        for c0 in range(0, s_rows, key_chunk):
            rows = slice(c0, min(c0 + key_chunk, s_rows))
            s_c = _dot_nt(k_scr[kv_of(head_n), rows, :], q)
            s_next[rows, :] = s_c
            mc = jnp.max(s_c.reshape(-1, 8, q_tile), axis=0)
            m8 = mc if m8 is None else jnp.maximum(m8, mc)
            p = jnp.exp2(s_cur[rows, :] - m_cur)
            lc = jnp.sum(p.reshape(-1, 8, q_tile), axis=0)
            l8 = lc if l8 is None else l8 + lc
            a = _dot(v_scr[kv_of(head_c), :, rows], p.astype(BF16))
            acc = a if acc is None else acc + a
        m_next[...] = jnp.max(m8, axis=0, keepdims=True)
        l = jnp.sum(l8, axis=0, keepdims=True)
        o_ref[0, rows_of(tile_c), head_c * LANES:(head_c + 1) * LANES] = (acc / l).T.astype(BF16)

    def one_tile(tile, last):
        for head in range(hps):
            cur = (tile, head, *bufs[head % 2])
            if head + 1 < hps:
                phase(tile, head + 1, *bufs[(head + 1) % 2], *cur)
            elif not last:
                phase(tile + 1, 0, *bufs[(head + 1) % 2], *cur)
            else:
                softmax_pv(*cur)

    assert hps % 2 == 0

    def body(tile, carry):
        one_tile(tile, False)
        return carry

    lax.fori_loop(0, n_tiles - 1, body, 0)
    one_tile(n_tiles - 1, True)


def _attention(q, k_groups, v_groups, *, heads, hps, per_head, share_kv, key_chunk):
    b, tq, qw = q.shape
    dq = qw // heads
    q_tile = min(Q_TILE, tq)
    n_tiles = tq // q_tile
    n_kv = 1 if share_kv else hps
    assert heads % hps == 0 and (hps * n_tiles == 1 or hps % 2 == 0)
    key_rows = tuple(g[0].shape[1] for g in k_groups)
    k_pieces = len(per_head)
    s_total = sum(key_rows)
    in_specs = [pl.BlockSpec((1, tq, hps * dq), lambda bi, hg: (bi, 0, hg))]
    args = [q]
    for grp in k_groups:
        for pi, arr in enumerate(grp):
            width = n_kv * LANES if per_head[pi] else LANES
            moves = per_head[pi] or share_kv
            in_specs.append(pl.BlockSpec((1, arr.shape[1], width),
                                         lambda bi, hg, moves=moves: (bi, 0, hg if moves else 0)))
            args.append(arr)
    for arr in v_groups:
        in_specs.append(pl.BlockSpec((1, n_kv * LANES, arr.shape[2]), lambda bi, hg: (bi, hg, 0)))
        args.append(arr)
    return pl.pallas_call(
        functools.partial(_attn_kernel, key_rows=key_rows, per_head=tuple(per_head), n_kv=n_kv, hps=hps,
                          dq=dq, q_tile=q_tile, n_tiles=n_tiles, key_chunk=key_chunk),
        grid=(b, heads // hps),
        in_specs=in_specs,
        out_specs=pl.BlockSpec((1, tq, hps * LANES), lambda bi, hg: (bi, 0, hg)),
        out_shape=jax.ShapeDtypeStruct((b, tq, heads * LANES), BF16),
        scratch_shapes=[
            pltpu.VMEM((n_kv, s_total, k_pieces * LANES), BF16), pltpu.VMEM((n_kv, LANES, s_total), BF16),
            pltpu.VMEM((s_total, q_tile), F32), pltpu.VMEM((s_total, q_tile), F32),
            pltpu.VMEM((1, q_tile), F32), pltpu.VMEM((1, q_tile), F32),
        ],
        compiler_params=_params("parallel", "parallel"),
        name="attention",
    )(*args)


def _nb_attn_kernel(*refs, has_window, rows, tile, n_groups, pps):
    it = iter(refs)
    q_ref, kc_ref, vc_ref = next(it), next(it), next(it)
    if has_window:
        kl_ref, vl_ref, bias_ref = next(it), next(it), next(it)
    o_ref = next(it)
    s_a, s_b, m_a, m_b = (next(it) for _ in range(4))
    bufs = ((s_a, m_a), (s_b, m_b))
    c = kc_ref.shape[1]
    win = NB_KEY_ROWS * GRID_W
    lane = lax.broadcasted_iota(jnp.int32, (1, LANES), 1)

    def tile_rows(g):
        return pl.ds(pl.multiple_of(g * tile, tile), tile)

    def lanes_of(j):
        return slice(j * LANES, (j + 1) * LANES)

    def window_start(g):
        first_row = jnp.clip(NB_ROWS * g - C_WIN_ROWS // 2, 0, rows - NB_KEY_ROWS)
        return pl.multiple_of(first_row * GRID_W, 2 * LANES)

    def two_heads(g, j):
        q = q_ref[0, tile_rows(g), lanes_of(j)]
        zero = jnp.zeros_like(q)
        return jnp.concatenate([jnp.where(lane < C_HEAD_DIM, q, zero), jnp.where(lane >= C_HEAD_DIM, q, zero)],
                               axis=0)

    def finish(o2, l, g, j):
        o_t = jnp.concatenate([o2[e * C_HEAD_DIM:(e + 1) * C_HEAD_DIM, e * tile:(e + 1) * tile]
                               / l[:, e * tile:(e + 1) * tile] for e in range(2)], axis=0)
        o_ref[0, tile_rows(g), lanes_of(j)] = o_t.T.astype(BF16)

    def chunks(g_next, j_next, g_cur, j_cur):
        yield kc_ref[0, :, lanes_of(j_next)], None, slice(0, c), vc_ref[0, lanes_of(j_cur), :]
        if has_window:
            variant = jnp.where(g_next == 0, 0, jnp.where(g_next == n_groups - 1, 2, 1))
            w_next, w_cur = window_start(g_next), window_start(g_cur)
            for c0 in range(0, win, NB_CHUNK):
                bias = [bias_ref[variant, 2 * j_next + e, c0:c0 + NB_CHUNK, :] for e in range(2)]
                yield (kl_ref[0, pl.ds(w_next + c0, NB_CHUNK), lanes_of(j_next)], bias,
                       slice(c + c0, c + c0 + NB_CHUNK), vl_ref[0, lanes_of(j_cur), pl.ds(w_cur + c0, NB_CHUNK)])

    def phase(g_next, j_next, s_next, m_next, g_cur, j_cur, s_cur, m_cur_buf):
        q2 = two_heads(g_next, j_next) if g_next is not None else None
        m_cur = m_cur_buf[...] if g_cur is not None else None
        m8 = l8 = o2 = None
        gn = g_next if g_next is not None else g_cur
        gc = g_cur if g_cur is not None else g_next
        for k, bias, rws, v_t in chunks(gn, j_next if g_next is not None else j_cur,
                                        gc, j_cur if g_cur is not None else j_next):
            if g_next is not None:
                s_k = _dot_nt(k, q2)
                if bias is not None:
                    s_k = jnp.concatenate([s_k[:, e * tile:(e + 1) * tile] + bias[e] for e in range(2)], axis=1)
                s_next[rws, :] = s_k
                mk = jnp.max(s_k.reshape(-1, 8, 2 * tile), axis=0)
                m8 = mk if m8 is None else jnp.maximum(m8, mk)
            if g_cur is not None:
                p = jnp.exp2(s_cur[rws, :] - m_cur)
                lk = jnp.sum(p.reshape(-1, 8, 2 * tile), axis=0)
                l8 = lk if l8 is None else l8 + lk
                ok = _dot(v_t, p.astype(BF16))
                o2 = ok if o2 is None else o2 + ok
        if g_next is not None:
            m_next[...] = jnp.max(m8, axis=0, keepdims=True)
        if g_cur is not None:
            finish(o2, jnp.sum(l8, axis=0, keepdims=True), g_cur, j_cur)

    def one_group(g, last):
        for j in range(pps):
            cur = (g, j, *bufs[j % 2])
            nxt_bufs = bufs[(j + 1) % 2]
            if j + 1 < pps:
                phase(g, j + 1, *nxt_bufs, *cur)
            elif not last:
                phase(g + 1, 0, *nxt_bufs, *cur)
            else:
                phase(None, None, None, None, *cur)

    assert pps % 2 == 0
    phase(0, 0, *bufs[0], None, None, None, None)

    def body(g, carry):
        one_group(g, False)
        return carry

    lax.fori_loop(0, n_groups - 1, body, 0)
    one_group(n_groups - 1, True)


def _nb_attention(q, k_ctx, vt_ctx, k_lat=None, vt_lat=None, bias=None):
    b, tq, hw = q.shape
    pairs = hw // LANES
    has_window = k_lat is not None
    c = k_ctx.shape[1]
    if has_window:
        t = k_lat.shape[1]
        rows = t // GRID_W
        tile = NB_ROWS * GRID_W
        keys = c + NB_KEY_ROWS * GRID_W
        pps = NB_PAIRS_PER_STEP
    else:
        rows, tile, keys, pps = 0, tq, c, pairs
    n_groups = tq // tile
    w = pps * LANES
    in_specs = [
        pl.BlockSpec((1, tq, w), lambda p, bi: (bi, 0, p)),
        pl.BlockSpec((1, c, w), lambda p, bi: (bi, 0, p)),
        pl.BlockSpec((1, w, c), lambda p, bi: (bi, p, 0)),
    ]
    args = [q, k_ctx, vt_ctx]
    if has_window:
        in_specs += [
            pl.BlockSpec((1, t, w), lambda p, bi: (bi, 0, p)),
            pl.BlockSpec((1, w, t), lambda p, bi: (bi, p, 0)),
            pl.BlockSpec((3, 2 * pps, NB_KEY_ROWS * GRID_W, tile), lambda p, bi: (0, p, 0, 0),
                         pipeline_mode=pl.Buffered(1)),
        ]
        args += [k_lat, vt_lat, bias]
    return pl.pallas_call(
        functools.partial(_nb_attn_kernel, has_window=has_window, rows=rows, tile=tile, n_groups=n_groups, pps=pps),
        grid=(pairs // pps, b),
        in_specs=in_specs,
        out_specs=pl.BlockSpec((1, tq, w), lambda p, bi: (bi, 0, p)),
        out_shape=jax.ShapeDtypeStruct((b, tq, hw), BF16),
        scratch_shapes=[pltpu.VMEM((keys, 2 * tile), F32), pltpu.VMEM((keys, 2 * tile), F32),
                        pltpu.VMEM((1, 2 * tile), F32), pltpu.VMEM((1, 2 * tile), F32)],
        compiler_params=_params("parallel", "parallel"),
        name="nb_attention",
    )(*args)


def _nb_bias_table(rpb, rows):
    heads = rpb.shape[0]
    tile_q = NB_ROWS * GRID_W
    return pl.pallas_call(
        functools.partial(_nb_bias_kernel, rows=rows),
        grid=(heads,),
        in_specs=[pl.BlockSpec(memory_space=pltpu.SMEM)],
        out_specs=pl.BlockSpec((3, 1, NB_KEY_ROWS * GRID_W, tile_q), lambda h: (0, h, 0, 0)),
        out_shape=jax.ShapeDtypeStruct((3, heads, NB_KEY_ROWS * GRID_W, tile_q), F32),
        compiler_params=_params("parallel"),
        name="nb_bias",
    )(rpb.reshape(-1))


def _nb_bias_kernel(rpb_ref, o_ref, *, rows):
    n_a, n_b = 2 * C_WIN_ROWS - 1, 2 * C_WIN_COLS - 1
    tile_q = NB_ROWS * GRID_W
    shape = (GRID_W, tile_q)
    kc = lax.broadcasted_iota(jnp.int32, shape, 0)
    lane = lax.broadcasted_iota(jnp.int32, shape, 1)
    qc = lane % GRID_W
    qi = lane // GRID_W
    c0 = jnp.clip(qc - C_WIN_COLS // 2, 0, GRID_W - C_WIN_COLS)
    col_ok = (kc >= c0) & (kc < c0 + C_WIN_COLS)
    dcol = kc - qc + (C_WIN_COLS - 1)
    base = pl.program_id(0) * (n_a * n_b)
    masked = jnp.full(shape, MASK_VALUE, F32)
    planes = []
    for a in range(n_a):
        acc = masked
        for bb in range(n_b):
            acc = jnp.where(dcol == bb, rpb_ref[base + a * n_b + bb], acc)
        planes.append(jnp.where(col_ok, acc * LOG2E, MASK_VALUE))
    groups = rows // NB_ROWS
    for v, g in enumerate((0, 1, groups - 1)):
        first_key_row = min(max(NB_ROWS * g - C_WIN_ROWS // 2, 0), rows - NB_KEY_ROWS)
        for j in range(NB_KEY_ROWS):
            kr = first_key_row + j
            blk = masked
            for i in range(NB_ROWS):
                qr = NB_ROWS * g + i
                r0 = min(max(qr - C_WIN_ROWS // 2, 0), rows - C_WIN_ROWS)
                if r0 <= kr < r0 + C_WIN_ROWS:
                    blk = jnp.where(qi == i, planes[kr - qr + C_WIN_ROWS - 1], blk)
            o_ref[v, 0, j * GRID_W:(j + 1) * GRID_W, :] = blk


def _rope_tables(rows, rot_dim):
    n = rot_dim // 4
    inv_freq = ROPE_THETA ** (-jnp.arange(n, dtype=F32) / n)
    t = jnp.arange(rows * GRID_W, dtype=jnp.int32)
    r = (t // GRID_W).astype(F32)
    col = (t % GRID_W).astype(F32)
    ang = jnp.concatenate([r[:, None] * inv_freq[None, :], col[:, None] * inv_freq[None, :]], axis=-1)
    cos, sin = jnp.cos(ang), jnp.sin(ang)
    reps = LANES // rot_dim
    return (jnp.tile(jnp.concatenate([cos, cos], axis=-1), (1, reps)),
            jnp.tile(jnp.concatenate([-sin, sin], axis=-1), (1, reps)))


def _mla_weights(w_in, q_norm, kv_norm, w_uq, w_ukv, w_o):
    rank = A_Q_RANK + A_KV_RANK
    k_r = w_in[:, rank:]
    uq = w_uq.reshape(A_Q_RANK, A_HEADS, A_NOPE + A_ROPE)
    ukv = w_ukv.reshape(A_KV_RANK, A_HEADS, A_NOPE + A_V)
    return {
        "w_in": jnp.concatenate([w_in[:, :rank], k_r, k_r], axis=1).astype(BF16),
        "q_norm": q_norm, "kv_norm": kv_norm,
        "w_uq": jnp.concatenate([uq[:, :, :A_NOPE].reshape(A_Q_RANK, -1),
                                 uq[:, :, A_NOPE:].reshape(A_Q_RANK, -1)], axis=1).astype(BF16),
        "w_uk": ukv[:, :, :A_NOPE].reshape(A_KV_RANK, -1).astype(BF16),
        "w_uvt": ukv[:, :, A_NOPE:].reshape(A_KV_RANK, -1).T.astype(BF16),
        "w_o": w_o.astype(BF16),
    }


def kernel(x, c, ctx, c_ctx, norm_g, w_mod, b_mod, ffn1_w13, ffn1_w2, ffn2_w13, ffn2_w2, a_w_in, a_q_norm, a_kv_norm, a_w_uq, a_w_ukv, a_w_o, b_w_qkv, b_q_norm, b_k_norm, b_w_o, c_w_qkv, c_rpb, c_w_o, final_norm_g):
    b, t, d = x.shape
    depth = w_mod.shape[0]
    rows = t // GRID_W
    assert b < MOD_ROWS and t % (NB_ROWS * GRID_W) == 0 and rows >= NB_KEY_ROWS + NB_ROWS

    c_rows = jnp.zeros((MOD_ROWS, d), F32).at[:b].set(c).at[b].set(c_ctx)
    mod_all = _modulation(c_rows, w_mod, b_mod).reshape(depth, MOD_ROWS, N_MOD, d)
    lat_row = lambda bi: bi
    ctx_row = lambda bi: b

    rope_a = _rope_tables(rows, A_ROPE)
    rope_b = _rope_tables(rows, B_HEAD_DIM)

    w13_1, w2_1 = ffn1_w13.astype(BF16), ffn1_w2.astype(BF16)
    w13_2, w2_2 = ffn2_w13.astype(BF16), ffn2_w2.astype(BF16)

    xc = ctx
    for i in range(depth):
        ctx_out = i < depth - 1
        last = i == depth - 1
        mod = mod_all[i]
        kind, j = i % N_MIXERS, i // N_MIXERS
        x = _ffn(x, mod, lat_row, norm_g[i, 0], w13_1, w2_1, i, k0=0)
        xc = _ffn(xc, mod, ctx_row, norm_g[i, 0], w13_1, w2_1, i, k0=0)

        if kind == 0:
            w = _mla_weights(a_w_in[j], a_q_norm[j], a_kv_norm[j], a_w_uq[j], a_w_ukv[j], a_w_o[j])
            q, kn, kr, vt = _mla_proj(x, mod, lat_row, norm_g[i, 1], w, rope_a)
            qc, knc, krc, vtc = _mla_proj(xc, mod, ctx_row, norm_g[i, 1], w, None)
            cfg = dict(heads=A_HEADS, hps=A_HEADS_PER_STEP, per_head=(True, False), share_kv=False,
                       key_chunk=MLA_KEY_CHUNK)
            o = _attention(q, [[knc, krc], [kn, kr]], [vtc, vt], **cfg)
            if ctx_out:
                oc = _attention(qc, [[knc, krc]], [vtc], **dict(cfg, hps=A_HEADS))
            w_o = w["w_o"]
        elif kind == 1:
            n_qk = (B_HEADS + B_KV_HEADS) * B_HEAD_DIM
            w = {"w_qk": b_w_qkv[j][:, :n_qk].astype(BF16), "w_vt": b_w_qkv[j][:, n_qk:].T.astype(BF16),
                 "q_norm": b_q_norm[j], "k_norm": b_k_norm[j]}
            q, k, vt = _gqa_proj(x, mod, lat_row, norm_g[i, 1], w, rope_b)
            qc, kc, vtc = _gqa_proj(xc, mod, ctx_row, norm_g[i, 1], w, None)
            cfg = dict(heads=B_HEADS, hps=B_HEADS // B_KV_HEADS, per_head=(False,), share_kv=True,
                       key_chunk=GQA_KEY_CHUNK)
            o = _attention(q, [[kc], [k]], [vtc, vt], **cfg)
            if ctx_out:
                oc = _attention(qc, [[kc]], [vtc], **cfg)
            w_o = b_w_o[j].astype(BF16)
        else:
            n_qk = 2 * C_HEADS * C_HEAD_DIM
            w = {"w_qk": c_w_qkv[j][:, :n_qk].astype(BF16), "w_vt": c_w_qkv[j][:, n_qk:].T.astype(BF16)}
            q, k, vt = _nb_proj(x, mod, lat_row, norm_g[i, 1], w)
            qc, kc, vtc = _nb_proj(xc, mod, ctx_row, norm_g[i, 1], w)
            o = _nb_attention(q, kc, vtc, k, vt, _nb_bias_table(c_rpb[j], rows))
            if ctx_out:
                oc = _nb_attention(qc, kc, vtc)
            w_o = c_w_o[j].astype(BF16)

        x = _ffn(x, mod, lat_row, norm_g[i, 2], w13_2, w2_2, i, k0=6, attn=o, w_o=w_o,
                 final_g=final_norm_g if last else None)
        if ctx_out:
            xc = _ffn(xc, mod, ctx_row, norm_g[i, 2], w13_2, w2_2, i, k0=6, attn=oc, w_o=w_o)
    return x
```

```python
import functools

import jax
import jax.numpy as jnp
from jax import lax
from jax.experimental import pallas as pl
from jax.experimental.pallas import tpu as pltpu

F32 = jnp.float32
BF16 = jnp.bfloat16

GRID_W = 64
N_MIXERS = 3
N_MOD = 9
EPS = 1e-6
ROPE_THETA = 10000.0

A_HEADS, A_NOPE, A_ROPE, A_V = 8, 128, 64, 128
A_Q_RANK, A_KV_RANK = 384, 256
B_HEADS, B_KV_HEADS, B_HEAD_DIM = 8, 2, 128
C_HEADS, C_HEAD_DIM, C_WIN_ROWS, C_WIN_COLS = 16, 64, 8, 16

LANES = 128
MOD_ROWS = 16
MASK_VALUE = -1e30
LOG2E = 1.4426950408889634
VMEM_LIMIT = 56 * 1024 * 1024

ROW_TILE = 512
GQA_ROW_TILE = 256
A_HEADS_PER_STEP = 2
Q_TILE = 512
GQA_KEY_CHUNK = 512
MLA_KEY_CHUNK = 0
NB_CHUNK = 256
NB_ROWS = 4
NB_PAIRS_PER_STEP = 2
NB_KEY_ROWS = 12


def _params(*sem):
    return pltpu.CompilerParams(dimension_semantics=sem, vmem_limit_bytes=VMEM_LIMIT)


def _const_spec(shape):
    nd = len(shape)
    return pl.BlockSpec(shape, lambda *_: (0,) * nd, pipeline_mode=pl.Buffered(1))


def _dot(a, b):
    return jnp.dot(a, b, preferred_element_type=F32)


def _dot_nt(a, b):
    return lax.dot_general(a, b, (((1,), (1,)), ((), ())), preferred_element_type=F32)


def _rms(x):
    return x * lax.rsqrt(jnp.mean(x * x, axis=-1, keepdims=True) + EPS)


def _modulated_norm(x, g, scale, shift):
    return _rms(x) * (g * (1.0 + scale)) + shift


def _silu(x):
    return x / (1.0 + jnp.exp(-x))


def _mod_kernel(c_ref, w_ref, b_ref, o_ref):
    sc = _silu(c_ref[...]).astype(BF16)
    o_ref[0] = _dot(sc, w_ref[0].astype(BF16)) + b_ref[0]


def _modulation(c_rows, w_mod, b_mod):
    depth, d, n = w_mod.shape
    tn = d
    return pl.pallas_call(
        _mod_kernel,
        grid=(depth, n // tn),
        in_specs=[
            pl.BlockSpec((MOD_ROWS, d), lambda i, j: (0, 0)),
            pl.BlockSpec((1, d, tn), lambda i, j: (i, 0, j)),
            pl.BlockSpec((1, 1, tn), lambda i, j: (i, 0, j)),
        ],
        out_specs=pl.BlockSpec((1, MOD_ROWS, tn), lambda i, j: (i, 0, j)),
        out_shape=jax.ShapeDtypeStruct((depth, MOD_ROWS, n), F32),
        compiler_params=_params("parallel", "parallel"),
        name="modulation",
    )(c_rows, w_mod, b_mod.reshape(depth, 1, n))


def _ffn_kernel(*refs, k0, d_ff, has_oproj, final_norm):
    it = iter(refs)
    x_ref, mod_ref, g_ref, w13_ref, w2_ref = (next(it) for _ in range(5))
    if has_oproj:
        a_ref, wo_ref = next(it), next(it)
    if final_norm:
        fg_ref = next(it)
    out_ref = next(it)

    x = x_ref[0]
    mod = mod_ref[0]
    if has_oproj:
        x = x + mod[5:6] * _dot(a_ref[0], wo_ref[...])
    h = _modulated_norm(x, g_ref[...], mod[k0 + 1:k0 + 2], mod[k0:k0 + 1]).astype(BF16)
    hgu = _dot(h, w13_ref[...])
    act = (_silu(hgu[:, :d_ff]) * hgu[:, d_ff:]).astype(BF16)
    y = x + (0.5 * mod[k0 + 2:k0 + 3]) * _dot(act, w2_ref[...])
    if final_norm:
        y = _rms(y) * fg_ref[...]
    out_ref[0] = y


def _ffn(x, mod, mod_row, g, w13, w2, layer, *, k0, attn=None, w_o=None, final_g=None):
    b, t, d = x.shape
    d_ff = w2.shape[1]
    tm = min(ROW_TILE, t)
    row = lambda bi, ti: (bi, ti, 0)
    this_layer = lambda *_: (layer, 0, 0)
    in_specs = [
        pl.BlockSpec((1, tm, d), row),
        pl.BlockSpec((1, N_MOD, d), lambda bi, ti: (mod_row(bi), 0, 0)),
        _const_spec((1, d)),
        pl.BlockSpec((None,) + w13.shape[1:], this_layer, pipeline_mode=pl.Buffered(1)),
        pl.BlockSpec((None,) + w2.shape[1:], this_layer, pipeline_mode=pl.Buffered(1)),
    ]
    args = [x, mod, g.reshape(1, d), w13, w2]
    if attn is not None:
        in_specs += [pl.BlockSpec((1, tm, attn.shape[2]), row), _const_spec(w_o.shape)]
        args += [attn, w_o]
    if final_g is not None:
        in_specs.append(_const_spec((1, d)))
        args.append(final_g.reshape(1, d))
    return pl.pallas_call(
        functools.partial(_ffn_kernel, k0=k0, d_ff=d_ff, has_oproj=attn is not None,
                          final_norm=final_g is not None),
        grid=(b, t // tm),
        in_specs=in_specs,
        out_specs=pl.BlockSpec((1, tm, d), row),
        out_shape=jax.ShapeDtypeStruct((b, t, d), F32),
        compiler_params=_params("parallel", "parallel"),
        name="half_ffn",
    )(*args)


def _rope_pairs(x, cos, sin_signed, half):
    if 2 * half == LANES:
        rot = pltpu.roll(x, half, 1)
    else:
        lane = lax.broadcasted_iota(jnp.int32, x.shape, 1)
        first = (lane % (2 * half)) < half
        rot = jnp.where(first, pltpu.roll(x, LANES - half, 1), pltpu.roll(x, half, 1))
    return x * cos + rot * sin_signed


def _mla_proj_kernel(*refs, use_rope, sm_scale):
    it = iter(refs)
    x_ref, mod_ref, g_ref, w_in_ref, qg_ref, kvg_ref, w_uq_ref, w_uk_ref, w_uvt_ref = (next(it) for _ in range(9))
    if use_rope:
        cos_ref, sin_ref = next(it), next(it)
    q_ref, kn_ref, kr_ref, vt_ref = (next(it) for _ in range(4))

    mod = mod_ref[0]
    h = _modulated_norm(x_ref[0], g_ref[...], mod[4:5], mod[3:4]).astype(BF16)
    proj = _dot(h, w_in_ref[...])
    c_q = (_rms(proj[:, :A_Q_RANK]) * qg_ref[...]).astype(BF16)
    c_kv = (_rms(proj[:, A_Q_RANK:A_Q_RANK + A_KV_RANK]) * kvg_ref[...]).astype(BF16)
    k_r = proj[:, A_Q_RANK + A_KV_RANK:]
    q_all = _dot(c_q, w_uq_ref[...])
    n_nope = A_HEADS * A_NOPE
    if use_rope:
        cos, sin = cos_ref[...], sin_ref[...]
        k_r = _rope_pairs(k_r, cos, sin, A_ROPE // 2)
    lane = lax.broadcasted_iota(jnp.int32, (1, LANES), 1)
    for j in range(A_HEADS // 2):
        qr = q_all[:, n_nope + j * LANES:n_nope + (j + 1) * LANES]
        if use_rope:
            qr = _rope_pairs(qr, cos, sin, A_ROPE // 2)
        qr = qr * sm_scale
        for e in range(2):
            hd = 2 * j + e
            keep = (lane < A_ROPE) if e == 0 else (lane >= A_ROPE)
            q_ref[0, :, 2 * hd * LANES:(2 * hd + 1) * LANES] = (
                q_all[:, hd * A_NOPE:(hd + 1) * A_NOPE] * sm_scale).astype(BF16)
            q_ref[0, :, (2 * hd + 1) * LANES:(2 * hd + 2) * LANES] = jnp.where(keep, qr, 0.0).astype(BF16)
    kn_ref[0] = _dot(c_kv, w_uk_ref[...]).astype(BF16)
    kr_ref[0] = k_r.astype(BF16)
    vt_ref[0] = _dot_nt(w_uvt_ref[...], c_kv).astype(BF16)


def _mla_proj(x, mod, mod_row, g, w, rope):
    b, t, d = x.shape
    tm = min(ROW_TILE, t)
    row = lambda bi, ti: (bi, ti, 0)
    in_specs = [
        pl.BlockSpec((1, tm, d), row),
        pl.BlockSpec((1, N_MOD, d), lambda bi, ti: (mod_row(bi), 0, 0)),
        _const_spec((1, d)),
        _const_spec(w["w_in"].shape), _const_spec((1, A_Q_RANK)), _const_spec((1, A_KV_RANK)),
        _const_spec(w["w_uq"].shape), _const_spec(w["w_uk"].shape), _const_spec(w["w_uvt"].shape),
    ]
    args = [x, mod, g.reshape(1, d), w["w_in"], w["q_norm"].reshape(1, -1), w["kv_norm"].reshape(1, -1),
            w["w_uq"], w["w_uk"], w["w_uvt"]]
    if rope is not None:
        in_specs += [pl.BlockSpec((tm, LANES), lambda bi, ti: (ti, 0))] * 2
        args += list(rope)
    hv = A_HEADS * A_V
    return pl.pallas_call(
        functools.partial(_mla_proj_kernel, use_rope=rope is not None,
                          sm_scale=float((A_NOPE + A_ROPE) ** -0.5 * LOG2E)),
        grid=(b, t // tm),
        in_specs=in_specs,
        out_specs=[
            pl.BlockSpec((1, tm, 2 * LANES * A_HEADS), row),
            pl.BlockSpec((1, tm, A_HEADS * A_NOPE), row),
            pl.BlockSpec((1, tm, LANES), row),
            pl.BlockSpec((1, hv, tm), lambda bi, ti: (bi, 0, ti)),
        ],
        out_shape=[
            jax.ShapeDtypeStruct((b, t, 2 * LANES * A_HEADS), BF16),
            jax.ShapeDtypeStruct((b, t, A_HEADS * A_NOPE), BF16),
            jax.ShapeDtypeStruct((b, t, LANES), BF16),
            jax.ShapeDtypeStruct((b, hv, t), BF16),
        ],
        compiler_params=_params("parallel", "parallel"),
        name="mla_proj",
    )(*args)


def _gqa_proj_kernel(*refs, use_rope, sm_scale):
    it = iter(refs)
    x_ref, mod_ref, g_ref, w_ref, qg_ref, kg_ref = (next(it) for _ in range(6))
    if use_rope:
        cos_ref, sin_ref = next(it), next(it)
    q_ref, k_ref, vt_ref = (next(it) for _ in range(3))

    mod = mod_ref[0]
    h = _modulated_norm(x_ref[0], g_ref[...], mod[4:5], mod[3:4]).astype(BF16)
    proj = _dot(h, w_ref[...])
    qw = B_HEADS * B_HEAD_DIM
    kw = B_KV_HEADS * B_HEAD_DIM
    if use_rope:
        cos, sin = cos_ref[...], sin_ref[...]

    def head(col, gain, scale):
        y = _rms(proj[:, col:col + B_HEAD_DIM]) * gain
        if use_rope:
            y = _rope_pairs(y, cos, sin, B_HEAD_DIM // 2)
        return (y * scale).astype(BF16) if scale != 1.0 else y.astype(BF16)

    for hd in range(B_HEADS):
        q_ref[0, :, hd * B_HEAD_DIM:(hd + 1) * B_HEAD_DIM] = head(hd * B_HEAD_DIM, qg_ref[...], sm_scale)
    for hd in range(B_KV_HEADS):
        k_ref[0, :, hd * B_HEAD_DIM:(hd + 1) * B_HEAD_DIM] = head(qw + hd * B_HEAD_DIM, kg_ref[...], 1.0)
    vt_ref[0] = proj[:, qw + kw:].T.astype(BF16)


def _gqa_proj(x, mod, mod_row, g, w, rope):
    b, t, d = x.shape
    tm = min(GQA_ROW_TILE, t)
    row = lambda bi, ti: (bi, ti, 0)
    qw, kw = B_HEADS * B_HEAD_DIM, B_KV_HEADS * B_HEAD_DIM
    in_specs = [
        pl.BlockSpec((1, tm, d), row),
        pl.BlockSpec((1, N_MOD, d), lambda bi, ti: (mod_row(bi), 0, 0)),
        _const_spec((1, d)),
        _const_spec(w["w_qkv"].shape), _const_spec((1, B_HEAD_DIM)), _const_spec((1, B_HEAD_DIM)),
    ]
    args = [x, mod, g.reshape(1, d), w["w_qkv"], w["q_norm"].reshape(1, -1), w["k_norm"].reshape(1, -1)]
    if rope is not None:
        in_specs += [pl.BlockSpec((tm, LANES), lambda bi, ti: (ti, 0))] * 2
        args += list(rope)
    return pl.pallas_call(
        functools.partial(_gqa_proj_kernel, use_rope=rope is not None, sm_scale=float(B_HEAD_DIM ** -0.5 * LOG2E)),
        grid=(b, t // tm),
        in_specs=in_specs,
        out_specs=[
            pl.BlockSpec((1, tm, qw), row),
            pl.BlockSpec((1, tm, kw), row),
            pl.BlockSpec((1, kw, tm), lambda bi, ti: (bi, 0, ti)),
        ],
        out_shape=[
            jax.ShapeDtypeStruct((b, t, qw), BF16),
            jax.ShapeDtypeStruct((b, t, kw), BF16),
            jax.ShapeDtypeStruct((b, kw, t), BF16),
        ],
        compiler_params=_params("parallel", "parallel"),
        name="gqa_proj",
    )(*args)


def _nb_proj_kernel(x_ref, mod_ref, g_ref, w_ref, q_ref, k_ref, vt_ref, *, sm_scale):
    mod = mod_ref[0]
    h = _modulated_norm(x_ref[0], g_ref[...], mod[4:5], mod[3:4]).astype(BF16)
    proj = _dot(h, w_ref[...])
    hd = C_HEADS * C_HEAD_DIM
    q_ref[0] = (proj[:, :hd] * sm_scale).astype(BF16)
    k_ref[0] = proj[:, hd:2 * hd].astype(BF16)
    vt_ref[0] = proj[:, 2 * hd:].T.astype(BF16)


def _nb_proj(x, mod, mod_row, g, w):
    b, t, d = x.shape
    tm = min(ROW_TILE, t)
    row = lambda bi, ti: (bi, ti, 0)
    hd = C_HEADS * C_HEAD_DIM
    return pl.pallas_call(
        functools.partial(_nb_proj_kernel, sm_scale=float(C_HEAD_DIM ** -0.5 * LOG2E)),
        grid=(b, t // tm),
        in_specs=[
            pl.BlockSpec((1, tm, d), row),
            pl.BlockSpec((1, N_MOD, d), lambda bi, ti: (mod_row(bi), 0, 0)),
            _const_spec((1, d)),
            _const_spec(w["w_qkv"].shape),
        ],
        out_specs=[
            pl.BlockSpec((1, tm, hd), row),
            pl.BlockSpec((1, tm, hd), row),
            pl.BlockSpec((1, hd, tm), lambda bi, ti: (bi, 0, ti)),
        ],
        out_shape=[
            jax.ShapeDtypeStruct((b, t, hd), BF16),
            jax.ShapeDtypeStruct((b, t, hd), BF16),
            jax.ShapeDtypeStruct((b, hd, t), BF16),
        ],
        compiler_params=_params("parallel", "parallel"),
        name="nb_proj",
    )(x, mod, g.reshape(1, d), w["w_qkv"])


def _attn_kernel(*refs, key_rows, per_head, n_kv, hps, dq, q_tile, n_tiles, key_chunk):
    n_groups, k_pieces = len(key_rows), len(per_head)
    it = iter(refs)
    q_ref = next(it)
    k_refs = [[next(it) for _ in range(k_pieces)] for _ in range(n_groups)]
    v_refs = [next(it) for _ in range(n_groups)]
    o_ref = next(it)
    k_scr, v_scr, s_a, s_b, m_a, m_b = (next(it) for _ in range(6))

    r0 = 0
    for gi, rows in enumerate(key_rows):
        for j in range(n_kv):
            for pi in range(k_pieces):
                lanes = slice(j * LANES, (j + 1) * LANES) if per_head[pi] else slice(0, LANES)
                k_scr[j, r0:r0 + rows, pi * LANES:(pi + 1) * LANES] = k_refs[gi][pi][0, :, lanes]
            v_scr[j, :, r0:r0 + rows] = v_refs[gi][0, j * LANES:(j + 1) * LANES, :]
        r0 += rows

    def rows_of(tile):
        return pl.ds(pl.multiple_of(tile * q_tile, q_tile), q_tile)

    def kv_of(head):
        return head if n_kv > 1 else 0

    def scores(tile, head, s_buf, m_buf):
        q = q_ref[0, rows_of(tile), head * dq:(head + 1) * dq]
        s_t = _dot_nt(k_scr[kv_of(head)], q)
        s_buf[...] = s_t
        m_buf[...] = jnp.max(s_t, axis=0, keepdims=True)

    def softmax_pv(tile, head, s_buf, m_buf):
        p = jnp.exp2(s_buf[...] - m_buf[...])
        l = jnp.sum(p, axis=0, keepdims=True)
        o_t = _dot(v_scr[kv_of(head)], p.astype(BF16)) / l
        o_ref[0, rows_of(tile), head * LANES:(head + 1) * LANES] = o_t.T.astype(BF16)

    scores(0, 0, s_a, m_a)
    if hps * n_tiles == 1:
        softmax_pv(0, 0, s_a, m_a)
        return

    s_rows = s_a.shape[0]
    bufs = ((s_a, m_a), (s_b, m_b))

    def phase(tile_n, head_n, s_next, m_next, tile_c, head_c, s_cur, m_cur_buf):
        if not key_chunk:
            scores(tile_n, head_n, s_next, m_next)
            softmax_pv(tile_c, head_c, s_cur, m_cur_buf)
            return
        q = q_ref[0, rows_of(tile_n), head_n * dq:(head_n + 1) * dq]
        m_cur = m_cur_buf[...]
        m8 = l8 = acc = None
        for c0 in range(0, s_rows, key_chunk):
            rows = slice(c0, min(c0 + key_chunk, s_rows))
            s_c = _dot_nt(k_scr[kv_of(head_n), rows, :], q)
            s_next[rows, :] = s_c
            mc = jnp.max(s_c.reshape(-1, 8, q_tile), axis=0)
            m8 = mc if m8 is None else jnp.maximum(m8, mc)
            p = jnp.exp2(s_cur[rows, :] - m_cur)
            lc = jnp.sum(p.reshape(-1, 8, q_tile), axis=0)
            l8 = lc if l8 is None else l8 + lc
            a = _dot(v_scr[kv_of(head_c), :, rows], p.astype(BF16))
            acc = a if acc is None else acc + a
        m_next[...] = jnp.max(m8, axis=0, keepdims=True)
        l = jnp.sum(l8, axis=0, keepdims=True)
        o_ref[0, rows_of(tile_c), head_c * LANES:(head_c + 1) * LANES] = (acc / l).T.astype(BF16)

    def one_tile(tile, last):
        for head in range(hps):
            cur = (tile, head, *bufs[head % 2])
            if head + 1 < hps:
                phase(tile, head + 1, *bufs[(head + 1) % 2], *cur)
            elif not last:
                phase(tile + 1, 0, *bufs[(head + 1) % 2], *cur)
            else:
                softmax_pv(*cur)

    assert hps % 2 == 0

    def body(tile, carry):
        one_tile(tile, False)
        return carry

    lax.fori_loop(0, n_tiles - 1, body, 0)
    one_tile(n_tiles - 1, True)


def _attention(q, k_groups, v_groups, *, heads, hps, per_head, share_kv, key_chunk):
    b, tq, qw = q.shape
    dq = qw // heads
    q_tile = min(Q_TILE, tq)
    n_tiles = tq // q_tile
    n_kv = 1 if share_kv else hps
    assert heads % hps == 0 and (hps * n_tiles == 1 or hps % 2 == 0)
    key_rows = tuple(g[0].shape[1] for g in k_groups)
    k_pieces = len(per_head)
    s_total = sum(key_rows)
    in_specs = [pl.BlockSpec((1, tq, hps * dq), lambda bi, hg: (bi, 0, hg))]
    args = [q]
    for grp in k_groups:
        for pi, arr in enumerate(grp):
            width = n_kv * LANES if per_head[pi] else LANES
            moves = per_head[pi] or share_kv
            in_specs.append(pl.BlockSpec((1, arr.shape[1], width),
                                         lambda bi, hg, moves=moves: (bi, 0, hg if moves else 0)))
            args.append(arr)
    for arr in v_groups:
        in_specs.append(pl.BlockSpec((1, n_kv * LANES, arr.shape[2]), lambda bi, hg: (bi, hg, 0)))
        args.append(arr)
    return pl.pallas_call(
        functools.partial(_attn_kernel, key_rows=key_rows, per_head=tuple(per_head), n_kv=n_kv, hps=hps,
                          dq=dq, q_tile=q_tile, n_tiles=n_tiles, key_chunk=key_chunk),
        grid=(b, heads // hps),
        in_specs=in_specs,
        out_specs=pl.BlockSpec((1, tq, hps * LANES), lambda bi, hg: (bi, 0, hg)),
        out_shape=jax.ShapeDtypeStruct((b, tq, heads * LANES), BF16),
        scratch_shapes=[
            pltpu.VMEM((n_kv, s_total, k_pieces * LANES), BF16), pltpu.VMEM((n_kv, LANES, s_total), BF16),
            pltpu.VMEM((s_total, q_tile), F32), pltpu.VMEM((s_total, q_tile), F32),
            pltpu.VMEM((1, q_tile), F32), pltpu.VMEM((1, q_tile), F32),
        ],
        compiler_params=_params("parallel", "parallel"),
        name="attention",
    )(*args)


def _nb_attn_kernel(*refs, has_window, rows, tile, n_groups, pps):
    it = iter(refs)
    q_ref, kc_ref, vc_ref = next(it), next(it), next(it)
    if has_window:
        kl_ref, vl_ref, bias_ref = next(it), next(it), next(it)
    o_ref = next(it)
    s_a, s_b, m_a, m_b = (next(it) for _ in range(4))
    bufs = ((s_a, m_a), (s_b, m_b))
    c = kc_ref.shape[1]
    win = NB_KEY_ROWS * GRID_W
    lane = lax.broadcasted_iota(jnp.int32, (1, LANES), 1)

    def tile_rows(g):
        return pl.ds(pl.multiple_of(g * tile, tile), tile)

    def lanes_of(j):
        return slice(j * LANES, (j + 1) * LANES)

    def window_start(g):
        first_row = jnp.clip(NB_ROWS * g - C_WIN_ROWS // 2, 0, rows - NB_KEY_ROWS)
        return pl.multiple_of(first_row * GRID_W, 2 * LANES)

    def two_heads(g, j):
        q = q_ref[0, tile_rows(g), lanes_of(j)]
        zero = jnp.zeros_like(q)
        return jnp.concatenate([jnp.where(lane < C_HEAD_DIM, q, zero), jnp.where(lane >= C_HEAD_DIM, q, zero)],
                               axis=0)

    def finish(o2, l, g, j):
        o_t = jnp.concatenate([o2[e * C_HEAD_DIM:(e + 1) * C_HEAD_DIM, e * tile:(e + 1) * tile]
                               / l[:, e * tile:(e + 1) * tile] for e in range(2)], axis=0)
        o_ref[0, tile_rows(g), lanes_of(j)] = o_t.T.astype(BF16)

    def chunks(g_next, j_next, g_cur, j_cur):
        yield kc_ref[0, :, lanes_of(j_next)], None, slice(0, c), vc_ref[0, lanes_of(j_cur), :]
        if has_window:
            variant = jnp.where(g_next == 0, 0, jnp.where(g_next == n_groups - 1, 2, 1))
            w_next, w_cur = window_start(g_next), window_start(g_cur)
            for c0 in range(0, win, NB_CHUNK):
                bias = [bias_ref[variant, 2 * j_next + e, c0:c0 + NB_CHUNK, :] for e in range(2)]
                yield (kl_ref[0, pl.ds(w_next + c0, NB_CHUNK), lanes_of(j_next)], bias,
                       slice(c + c0, c + c0 + NB_CHUNK), vl_ref[0, lanes_of(j_cur), pl.ds(w_cur + c0, NB_CHUNK)])

    def phase(g_next, j_next, s_next, m_next, g_cur, j_cur, s_cur, m_cur_buf):
        q2 = two_heads(g_next, j_next) if g_next is not None else None
        m_cur = m_cur_buf[...] if g_cur is not None else None
        m8 = l8 = o2 = None
        gn = g_next if g_next is not None else g_cur
        gc = g_cur if g_cur is not None else g_next
        for k, bias, rws, v_t in chunks(gn, j_next if g_next is not None else j_cur,
                                        gc, j_cur if g_cur is not None else j_next):
            if g_next is not None:
                s_k = _dot_nt(k, q2)
                if bias is not None:
                    s_k = jnp.concatenate([s_k[:, e * tile:(e + 1) * tile] + bias[e] for e in range(2)], axis=1)
                s_next[rws, :] = s_k
                mk = jnp.max(s_k.reshape(-1, 8, 2 * tile), axis=0)
                m8 = mk if m8 is None else jnp.maximum(m8, mk)
            if g_cur is not None:
                p = jnp.exp2(s_cur[rws, :] - m_cur)
                lk = jnp.sum(p.reshape(-1, 8, 2 * tile), axis=0)
                l8 = lk if l8 is None else l8 + lk
                ok = _dot(v_t, p.astype(BF16))
                o2 = ok if o2 is None else o2 + ok
        if g_next is not None:
            m_next[...] = jnp.max(m8, axis=0, keepdims=True)
        if g_cur is not None:
            finish(o2, jnp.sum(l8, axis=0, keepdims=True), g_cur, j_cur)

    def one_group(g, last):
        for j in range(pps):
            cur = (g, j, *bufs[j % 2])
            nxt_bufs = bufs[(j + 1) % 2]
            if j + 1 < pps:
                phase(g, j + 1, *nxt_bufs, *cur)
            elif not last:
                phase(g + 1, 0, *nxt_bufs, *cur)
            else:
                phase(None, None, None, None, *cur)

    assert pps % 2 == 0
    phase(0, 0, *bufs[0], None, None, None, None)

    def body(g, carry):
        one_group(g, False)
        return carry

    lax.fori_loop(0, n_groups - 1, body, 0)
    one_group(n_groups - 1, True)


def _nb_attention(q, k_ctx, vt_ctx, k_lat=None, vt_lat=None, bias=None):
    b, tq, hw = q.shape
    pairs = hw // LANES
    has_window = k_lat is not None
    c = k_ctx.shape[1]
    if has_window:
        t = k_lat.shape[1]
        rows = t // GRID_W
        tile = NB_ROWS * GRID_W
        keys = c + NB_KEY_ROWS * GRID_W
        pps = NB_PAIRS_PER_STEP
    else:
        rows, tile, keys, pps = 0, tq, c, pairs
    n_groups = tq // tile
    w = pps * LANES
    in_specs = [
        pl.BlockSpec((1, tq, w), lambda p, bi: (bi, 0, p)),
        pl.BlockSpec((1, c, w), lambda p, bi: (bi, 0, p)),
        pl.BlockSpec((1, w, c), lambda p, bi: (bi, p, 0)),
    ]
    args = [q, k_ctx, vt_ctx]
    if has_window:
        in_specs += [
            pl.BlockSpec((1, t, w), lambda p, bi: (bi, 0, p)),
            pl.BlockSpec((1, w, t), lambda p, bi: (bi, p, 0)),
            pl.BlockSpec((3, 2 * pps, NB_KEY_ROWS * GRID_W, tile), lambda p, bi: (0, p, 0, 0),
                         pipeline_mode=pl.Buffered(1)),
        ]
        args += [k_lat, vt_lat, bias]
    return pl.pallas_call(
        functools.partial(_nb_attn_kernel, has_window=has_window, rows=rows, tile=tile, n_groups=n_groups, pps=pps),
        grid=(pairs // pps, b),
        in_specs=in_specs,
        out_specs=pl.BlockSpec((1, tq, w), lambda p, bi: (bi, 0, p)),
        out_shape=jax.ShapeDtypeStruct((b, tq, hw), BF16),
        scratch_shapes=[pltpu.VMEM((keys, 2 * tile), F32), pltpu.VMEM((keys, 2 * tile), F32),
                        pltpu.VMEM((1, 2 * tile), F32), pltpu.VMEM((1, 2 * tile), F32)],
        compiler_params=_params("parallel", "parallel"),
        name="nb_attention",
    )(*args)


def _nb_bias_table(rpb, rows):
    heads = rpb.shape[0]
    tile_q = NB_ROWS * GRID_W
    return pl.pallas_call(
        functools.partial(_nb_bias_kernel, rows=rows),
        grid=(heads,),
        in_specs=[pl.BlockSpec(memory_space=pltpu.SMEM)],
        out_specs=pl.BlockSpec((3, 1, NB_KEY_ROWS * GRID_W, tile_q), lambda h: (0, h, 0, 0)),
        out_shape=jax.ShapeDtypeStruct((3, heads, NB_KEY_ROWS * GRID_W, tile_q), F32),
        compiler_params=_params("parallel"),
        name="nb_bias",
    )(rpb.reshape(-1))


def _nb_bias_kernel(rpb_ref, o_ref, *, rows):
    n_a, n_b = 2 * C_WIN_ROWS - 1, 2 * C_WIN_COLS - 1
    tile_q = NB_ROWS * GRID_W
    shape = (GRID_W, tile_q)
    kc = lax.broadcasted_iota(jnp.int32, shape, 0)
    lane = lax.broadcasted_iota(jnp.int32, shape, 1)
    qc = lane % GRID_W
    qi = lane // GRID_W
    c0 = jnp.clip(qc - C_WIN_COLS // 2, 0, GRID_W - C_WIN_COLS)
    col_ok = (kc >= c0) & (kc < c0 + C_WIN_COLS)
    dcol = kc - qc + (C_WIN_COLS - 1)
    base = pl.program_id(0) * (n_a * n_b)
    masked = jnp.full(shape, MASK_VALUE, F32)
    planes = []
    for a in range(n_a):
        acc = masked
        for bb in range(n_b):
            acc = jnp.where(dcol == bb, rpb_ref[base + a * n_b + bb], acc)
        planes.append(jnp.where(col_ok, acc * LOG2E, MASK_VALUE))
    groups = rows // NB_ROWS
    for v, g in enumerate((0, 1, groups - 1)):
        first_key_row = min(max(NB_ROWS * g - C_WIN_ROWS // 2, 0), rows - NB_KEY_ROWS)
        for j in range(NB_KEY_ROWS):
            kr = first_key_row + j
            blk = masked
            for i in range(NB_ROWS):
                qr = NB_ROWS * g + i
                r0 = min(max(qr - C_WIN_ROWS // 2, 0), rows - C_WIN_ROWS)
                if r0 <= kr < r0 + C_WIN_ROWS:
                    blk = jnp.where(qi == i, planes[kr - qr + C_WIN_ROWS - 1], blk)
            o_ref[v, 0, j * GRID_W:(j + 1) * GRID_W, :] = blk


def _rope_tables(rows, rot_dim):
    n = rot_dim // 4
    inv_freq = ROPE_THETA ** (-jnp.arange(n, dtype=F32) / n)
    t = jnp.arange(rows * GRID_W, dtype=jnp.int32)
    r = (t // GRID_W).astype(F32)
    col = (t % GRID_W).astype(F32)
    ang = jnp.concatenate([r[:, None] * inv_freq[None, :], col[:, None] * inv_freq[None, :]], axis=-1)
    cos, sin = jnp.cos(ang), jnp.sin(ang)
    reps = LANES // rot_dim
    return (jnp.tile(jnp.concatenate([cos, cos], axis=-1), (1, reps)),
            jnp.tile(jnp.concatenate([-sin, sin], axis=-1), (1, reps)))


def _mla_weights(w_in, q_norm, kv_norm, w_uq, w_ukv, w_o):
    rank = A_Q_RANK + A_KV_RANK
    k_r = w_in[:, rank:]
    uq = w_uq.reshape(A_Q_RANK, A_HEADS, A_NOPE + A_ROPE)
    ukv = w_ukv.reshape(A_KV_RANK, A_HEADS, A_NOPE + A_V)
    return {
        "w_in": jnp.concatenate([w_in[:, :rank], k_r, k_r], axis=1).astype(BF16),
        "q_norm": q_norm, "kv_norm": kv_norm,
        "w_uq": jnp.concatenate([uq[:, :, :A_NOPE].reshape(A_Q_RANK, -1),
                                 uq[:, :, A_NOPE:].reshape(A_Q_RANK, -1)], axis=1).astype(BF16),
        "w_uk": ukv[:, :, :A_NOPE].reshape(A_KV_RANK, -1).astype(BF16),
        "w_uvt": ukv[:, :, A_NOPE:].reshape(A_KV_RANK, -1).T.astype(BF16),
        "w_o": w_o.astype(BF16),
    }


def kernel(x, c, ctx, c_ctx, norm_g, w_mod, b_mod, ffn1_w13, ffn1_w2, ffn2_w13, ffn2_w2, a_w_in, a_q_norm, a_kv_norm, a_w_uq, a_w_ukv, a_w_o, b_w_qkv, b_q_norm, b_k_norm, b_w_o, c_w_qkv, c_rpb, c_w_o, final_norm_g):
    b, t, d = x.shape
    depth = w_mod.shape[0]
    rows = t // GRID_W
    assert b < MOD_ROWS and t % (NB_ROWS * GRID_W) == 0 and rows >= NB_KEY_ROWS + NB_ROWS

    c_rows = jnp.zeros((MOD_ROWS, d), F32).at[:b].set(c).at[b].set(c_ctx)
    mod_all = _modulation(c_rows, w_mod, b_mod).reshape(depth, MOD_ROWS, N_MOD, d)
    lat_row = lambda bi: bi
    ctx_row = lambda bi: b

    rope_a = _rope_tables(rows, A_ROPE)
    rope_b = _rope_tables(rows, B_HEAD_DIM)

    w13_1, w2_1 = ffn1_w13.astype(BF16), ffn1_w2.astype(BF16)
    w13_2, w2_2 = ffn2_w13.astype(BF16), ffn2_w2.astype(BF16)

    xc = ctx
    for i in range(depth):
        ctx_out = i < depth - 1
        last = i == depth - 1
        mod = mod_all[i]
        kind, j = i % N_MIXERS, i // N_MIXERS
        x = _ffn(x, mod, lat_row, norm_g[i, 0], w13_1, w2_1, i, k0=0)
        xc = _ffn(xc, mod, ctx_row, norm_g[i, 0], w13_1, w2_1, i, k0=0)

        if kind == 0:
            w = _mla_weights(a_w_in[j], a_q_norm[j], a_kv_norm[j], a_w_uq[j], a_w_ukv[j], a_w_o[j])
            q, kn, kr, vt = _mla_proj(x, mod, lat_row, norm_g[i, 1], w, rope_a)
            qc, knc, krc, vtc = _mla_proj(xc, mod, ctx_row, norm_g[i, 1], w, None)
            cfg = dict(heads=A_HEADS, hps=A_HEADS_PER_STEP, per_head=(True, False), share_kv=False,
                       key_chunk=MLA_KEY_CHUNK)
            o = _attention(q, [[knc, krc], [kn, kr]], [vtc, vt], **cfg)
            if ctx_out:
                oc = _attention(qc, [[knc, krc]], [vtc], **dict(cfg, hps=A_HEADS))
            w_o = w["w_o"]
        elif kind == 1:
            w = {"w_qkv": b_w_qkv[j].astype(BF16), "q_norm": b_q_norm[j], "k_norm": b_k_norm[j]}
            q, k, vt = _gqa_proj(x, mod, lat_row, norm_g[i, 1], w, rope_b)
            qc, kc, vtc = _gqa_proj(xc, mod, ctx_row, norm_g[i, 1], w, None)
            cfg = dict(heads=B_HEADS, hps=B_HEADS // B_KV_HEADS, per_head=(False,), share_kv=True,
                       key_chunk=GQA_KEY_CHUNK)
            o = _attention(q, [[kc], [k]], [vtc, vt], **cfg)
            if ctx_out:
                oc = _attention(qc, [[kc]], [vtc], **cfg)
            w_o = b_w_o[j].astype(BF16)
        else:
            w = {"w_qkv": c_w_qkv[j].astype(BF16)}
            q, k, vt = _nb_proj(x, mod, lat_row, norm_g[i, 1], w)
            qc, kc, vtc = _nb_proj(xc, mod, ctx_row, norm_g[i, 1], w)
            o = _nb_attention(q, kc, vtc, k, vt, _nb_bias_table(c_rpb[j], rows))
            if ctx_out:
                oc = _nb_attention(qc, kc, vtc)
            w_o = c_w_o[j].astype(BF16)

        x = _ffn(x, mod, lat_row, norm_g[i, 2], w13_2, w2_2, i, k0=6, attn=o, w_o=w_o,
                 final_g=final_norm_g if last else None)
        if ctx_out:
            xc = _ffn(xc, mod, ctx_row, norm_g[i, 2], w13_2, w2_2, i, k0=6, attn=oc, w_o=w_o)
    return x
```

```python
import functools

import jax
import jax.numpy as jnp
from jax import lax
from jax.experimental import pallas as pl
from jax.experimental.pallas import tpu as pltpu

F32 = jnp.float32
BF16 = jnp.bfloat16

GRID_W = 64
N_MIXERS = 3
N_MOD = 9
EPS = 1e-6
ROPE_THETA = 10000.0

A_HEADS, A_NOPE, A_ROPE, A_V = 8, 128, 64, 128
A_Q_RANK, A_KV_RANK = 384, 256
B_HEADS, B_KV_HEADS, B_HEAD_DIM = 8, 2, 128
C_HEADS, C_HEAD_DIM, C_WIN_ROWS, C_WIN_COLS = 16, 64, 8, 16

LANES = 128
MOD_ROWS = 16
MASK_VALUE = -1e30
LOG2E = 1.4426950408889634
VMEM_LIMIT = 56 * 1024 * 1024

ROW_TILE = 512
GQA_ROW_TILE = 256
A_HEADS_PER_STEP = 2
Q_TILE = 512
GQA_KEY_CHUNK = 512
MLA_KEY_CHUNK = 0
NB_CHUNK = 256
NB_ROWS = 4
NB_PAIRS_PER_STEP = 2
NB_KEY_ROWS = 12


def _params(*sem):
    return pltpu.CompilerParams(dimension_semantics=sem, vmem_limit_bytes=VMEM_LIMIT)


def _const_spec(shape):
    nd = len(shape)
    return pl.BlockSpec(shape, lambda *_: (0,) * nd, pipeline_mode=pl.Buffered(1))


def _dot(a, b):
    return jnp.dot(a, b, preferred_element_type=F32)


def _dot_nt(a, b):
    return lax.dot_general(a, b, (((1,), (1,)), ((), ())), preferred_element_type=F32)


def _rms(x):
    return x * lax.rsqrt(jnp.mean(x * x, axis=-1, keepdims=True) + EPS)


def _modulated_norm(x, g, scale, shift):
    return _rms(x) * (g * (1.0 + scale)) + shift


def _silu(x):
    return x / (1.0 + jnp.exp(-x))


def _mod_kernel(c_ref, w_ref, b_ref, o_ref):
    sc = _silu(c_ref[...]).astype(BF16)
    o_ref[0] = _dot(sc, w_ref[0].astype(BF16)) + b_ref[0]


def _modulation(c_rows, w_mod, b_mod):
    depth, d, n = w_mod.shape
    tn = d
    return pl.pallas_call(
        _mod_kernel,
        grid=(depth, n // tn),
        in_specs=[
            pl.BlockSpec((MOD_ROWS, d), lambda i, j: (0, 0)),
            pl.BlockSpec((1, d, tn), lambda i, j: (i, 0, j)),
            pl.BlockSpec((1, 1, tn), lambda i, j: (i, 0, j)),
        ],
        out_specs=pl.BlockSpec((1, MOD_ROWS, tn), lambda i, j: (i, 0, j)),
        out_shape=jax.ShapeDtypeStruct((depth, MOD_ROWS, n), F32),
        compiler_params=_params("parallel", "parallel"),
        name="modulation",
    )(c_rows, w_mod, b_mod.reshape(depth, 1, n))


def _ffn_kernel(*refs, k0, d_ff, has_oproj, final_norm):
    it = iter(refs)
    x_ref, mod_ref, g_ref, w13_ref, w2_ref = (next(it) for _ in range(5))
    if has_oproj:
        a_ref, wo_ref = next(it), next(it)
    if final_norm:
        fg_ref = next(it)
    out_ref = next(it)

    x = x_ref[0]
    mod = mod_ref[0]
    if has_oproj:
        x = x + mod[5:6] * _dot(a_ref[0], wo_ref[...])
    h = _modulated_norm(x, g_ref[...], mod[k0 + 1:k0 + 2], mod[k0:k0 + 1]).astype(BF16)
    hgu = _dot(h, w13_ref[...])
    act = (_silu(hgu[:, :d_ff]) * hgu[:, d_ff:]).astype(BF16)
    y = x + (0.5 * mod[k0 + 2:k0 + 3]) * _dot(act, w2_ref[...])
    if final_norm:
        y = _rms(y) * fg_ref[...]
    out_ref[0] = y


def _ffn(x, mod, mod_row, g, w13, w2, layer, *, k0, attn=None, w_o=None, final_g=None):
    b, t, d = x.shape
    d_ff = w2.shape[1]
    tm = min(ROW_TILE, t)
    row = lambda bi, ti: (bi, ti, 0)
    this_layer = lambda *_: (layer, 0, 0)
    in_specs = [
        pl.BlockSpec((1, tm, d), row),
        pl.BlockSpec((1, N_MOD, d), lambda bi, ti: (mod_row(bi), 0, 0)),
        _const_spec((1, d)),
        pl.BlockSpec((None,) + w13.shape[1:], this_layer, pipeline_mode=pl.Buffered(1)),
        pl.BlockSpec((None,) + w2.shape[1:], this_layer, pipeline_mode=pl.Buffered(1)),
    ]
    args = [x, mod, g.reshape(1, d), w13, w2]
    if attn is not None:
        in_specs += [pl.BlockSpec((1, tm, attn.shape[2]), row), _const_spec(w_o.shape)]
        args += [attn, w_o]
    if final_g is not None:
        in_specs.append(_const_spec((1, d)))
        args.append(final_g.reshape(1, d))
    return pl.pallas_call(
        functools.partial(_ffn_kernel, k0=k0, d_ff=d_ff, has_oproj=attn is not None,
                          final_norm=final_g is not None),
        grid=(b, t // tm),
        in_specs=in_specs,
        out_specs=pl.BlockSpec((1, tm, d), row),
        out_shape=jax.ShapeDtypeStruct((b, t, d), F32),
        compiler_params=_params("parallel", "parallel"),
        name="half_ffn",
    )(*args)


def _rope_pairs(x, cos, sin_signed, half):
    if 2 * half == LANES:
        rot = pltpu.roll(x, half, 1)
    else:
        lane = lax.broadcasted_iota(jnp.int32, x.shape, 1)
        first = (lane % (2 * half)) < half
        rot = jnp.where(first, pltpu.roll(x, LANES - half, 1), pltpu.roll(x, half, 1))
    return x * cos + rot * sin_signed


def _mla_proj_kernel(*refs, use_rope, sm_scale):
    it = iter(refs)
    x_ref, mod_ref, g_ref, w_in_ref, qg_ref, kvg_ref, w_uq_ref, w_uk_ref, w_uvt_ref = (next(it) for _ in range(9))
    if use_rope:
        cos_ref, sin_ref = next(it), next(it)
    q_ref, kn_ref, kr_ref, vt_ref = (next(it) for _ in range(4))

    mod = mod_ref[0]
    h = _modulated_norm(x_ref[0], g_ref[...], mod[4:5], mod[3:4]).astype(BF16)
    proj = _dot(h, w_in_ref[...])
    c_q = (_rms(proj[:, :A_Q_RANK]) * qg_ref[...]).astype(BF16)
    c_kv = (_rms(proj[:, A_Q_RANK:A_Q_RANK + A_KV_RANK]) * kvg_ref[...]).astype(BF16)
    k_r = proj[:, A_Q_RANK + A_KV_RANK:]
    q_all = _dot(c_q, w_uq_ref[...])
    n_nope = A_HEADS * A_NOPE
    if use_rope:
        cos, sin = cos_ref[...], sin_ref[...]
        k_r = _rope_pairs(k_r, cos, sin, A_ROPE // 2)
    lane = lax.broadcasted_iota(jnp.int32, (1, LANES), 1)
    for j in range(A_HEADS // 2):
        qr = q_all[:, n_nope + j * LANES:n_nope + (j + 1) * LANES]
        if use_rope:
            qr = _rope_pairs(qr, cos, sin, A_ROPE // 2)
        qr = qr * sm_scale
        for e in range(2):
            hd = 2 * j + e
            keep = (lane < A_ROPE) if e == 0 else (lane >= A_ROPE)
            q_ref[0, :, 2 * hd * LANES:(2 * hd + 1) * LANES] = (
                q_all[:, hd * A_NOPE:(hd + 1) * A_NOPE] * sm_scale).astype(BF16)
            q_ref[0, :, (2 * hd + 1) * LANES:(2 * hd + 2) * LANES] = jnp.where(keep, qr, 0.0).astype(BF16)
    kn_ref[0] = _dot(c_kv, w_uk_ref[...]).astype(BF16)
    kr_ref[0] = k_r.astype(BF16)
    vt_ref[0] = _dot_nt(w_uvt_ref[...], c_kv).astype(BF16)


def _mla_proj(x, mod, mod_row, g, w, rope):
    b, t, d = x.shape
    tm = min(ROW_TILE, t)
    row = lambda bi, ti: (bi, ti, 0)
    in_specs = [
        pl.BlockSpec((1, tm, d), row),
        pl.BlockSpec((1, N_MOD, d), lambda bi, ti: (mod_row(bi), 0, 0)),
        _const_spec((1, d)),
        _const_spec(w["w_in"].shape), _const_spec((1, A_Q_RANK)), _const_spec((1, A_KV_RANK)),
        _const_spec(w["w_uq"].shape), _const_spec(w["w_uk"].shape), _const_spec(w["w_uvt"].shape),
    ]
    args = [x, mod, g.reshape(1, d), w["w_in"], w["q_norm"].reshape(1, -1), w["kv_norm"].reshape(1, -1),
            w["w_uq"], w["w_uk"], w["w_uvt"]]
    if rope is not None:
        in_specs += [pl.BlockSpec((tm, LANES), lambda bi, ti: (ti, 0))] * 2
        args += list(rope)
    hv = A_HEADS * A_V
    return pl.pallas_call(
        functools.partial(_mla_proj_kernel, use_rope=rope is not None,
                          sm_scale=float((A_NOPE + A_ROPE) ** -0.5 * LOG2E)),
        grid=(b, t // tm),
        in_specs=in_specs,
        out_specs=[
            pl.BlockSpec((1, tm, 2 * LANES * A_HEADS), row),
            pl.BlockSpec((1, tm, A_HEADS * A_NOPE), row),
            pl.BlockSpec((1, tm, LANES), row),
            pl.BlockSpec((1, hv, tm), lambda bi, ti: (bi, 0, ti)),
        ],
        out_shape=[
            jax.ShapeDtypeStruct((b, t, 2 * LANES * A_HEADS), BF16),
            jax.ShapeDtypeStruct((b, t, A_HEADS * A_NOPE), BF16),
            jax.ShapeDtypeStruct((b, t, LANES), BF16),
            jax.ShapeDtypeStruct((b, hv, t), BF16),
        ],
        compiler_params=_params("parallel", "parallel"),
        name="mla_proj",
    )(*args)


def _gqa_proj_kernel(*refs, use_rope, sm_scale):
    it = iter(refs)
    x_ref, mod_ref, g_ref, w_ref, qg_ref, kg_ref = (next(it) for _ in range(6))
    if use_rope:
        cos_ref, sin_ref = next(it), next(it)
    q_ref, k_ref, vt_ref = (next(it) for _ in range(3))

    mod = mod_ref[0]
    h = _modulated_norm(x_ref[0], g_ref[...], mod[4:5], mod[3:4]).astype(BF16)
    proj = _dot(h, w_ref[...])
    qw = B_HEADS * B_HEAD_DIM
    kw = B_KV_HEADS * B_HEAD_DIM
    if use_rope:
        cos, sin = cos_ref[...], sin_ref[...]

    def head(col, gain, scale):
        y = _rms(proj[:, col:col + B_HEAD_DIM]) * gain
        if use_rope:
            y = _rope_pairs(y, cos, sin, B_HEAD_DIM // 2)
        return (y * scale).astype(BF16) if scale != 1.0 else y.astype(BF16)

    for hd in range(B_HEADS):
        q_ref[0, :, hd * B_HEAD_DIM:(hd + 1) * B_HEAD_DIM] = head(hd * B_HEAD_DIM, qg_ref[...], sm_scale)
    for hd in range(B_KV_HEADS):
        k_ref[0, :, hd * B_HEAD_DIM:(hd + 1) * B_HEAD_DIM] = head(qw + hd * B_HEAD_DIM, kg_ref[...], 1.0)
    vt_ref[0] = proj[:, qw + kw:].T.astype(BF16)


def _gqa_proj(x, mod, mod_row, g, w, rope):
    b, t, d = x.shape
    tm = min(GQA_ROW_TILE, t)
    row = lambda bi, ti: (bi, ti, 0)
    qw, kw = B_HEADS * B_HEAD_DIM, B_KV_HEADS * B_HEAD_DIM
    in_specs = [
        pl.BlockSpec((1, tm, d), row),
        pl.BlockSpec((1, N_MOD, d), lambda bi, ti: (mod_row(bi), 0, 0)),
        _const_spec((1, d)),
        _const_spec(w["w_qkv"].shape), _const_spec((1, B_HEAD_DIM)), _const_spec((1, B_HEAD_DIM)),
    ]
    args = [x, mod, g.reshape(1, d), w["w_qkv"], w["q_norm"].reshape(1, -1), w["k_norm"].reshape(1, -1)]
    if rope is not None:
        in_specs += [pl.BlockSpec((tm, LANES), lambda bi, ti: (ti, 0))] * 2
        args += list(rope)
    return pl.pallas_call(
        functools.partial(_gqa_proj_kernel, use_rope=rope is not None, sm_scale=float(B_HEAD_DIM ** -0.5 * LOG2E)),
        grid=(b, t // tm),
        in_specs=in_specs,
        out_specs=[
            pl.BlockSpec((1, tm, qw), row),
            pl.BlockSpec((1, tm, kw), row),
            pl.BlockSpec((1, kw, tm), lambda bi, ti: (bi, 0, ti)),
        ],
        out_shape=[
            jax.ShapeDtypeStruct((b, t, qw), BF16),
            jax.ShapeDtypeStruct((b, t, kw), BF16),
            jax.ShapeDtypeStruct((b, kw, t), BF16),
        ],
        compiler_params=_params("parallel", "parallel"),
        name="gqa_proj",
    )(*args)


def _nb_proj_kernel(x_ref, mod_ref, g_ref, w_ref, q_ref, k_ref, vt_ref, *, sm_scale):
    mod = mod_ref[0]
    h = _modulated_norm(x_ref[0], g_ref[...], mod[4:5], mod[3:4]).astype(BF16)
    proj = _dot(h, w_ref[...])
    hd = C_HEADS * C_HEAD_DIM
    q_ref[0] = (proj[:, :hd] * sm_scale).astype(BF16)
    k_ref[0] = proj[:, hd:2 * hd].astype(BF16)
    vt_ref[0] = proj[:, 2 * hd:].T.astype(BF16)


def _nb_proj(x, mod, mod_row, g, w):
    b, t, d = x.shape
    tm = min(ROW_TILE, t)
    row = lambda bi, ti: (bi, ti, 0)
    hd = C_HEADS * C_HEAD_DIM
    return pl.pallas_call(
        functools.partial(_nb_proj_kernel, sm_scale=float(C_HEAD_DIM ** -0.5 * LOG2E)),
        grid=(b, t // tm),
        in_specs=[
            pl.BlockSpec((1, tm, d), row),
            pl.BlockSpec((1, N_MOD, d), lambda bi, ti: (mod_row(bi), 0, 0)),
            _const_spec((1, d)),
            _const_spec(w["w_qkv"].shape),
        ],
        out_specs=[
            pl.BlockSpec((1, tm, hd), row),
            pl.BlockSpec((1, tm, hd), row),
            pl.BlockSpec((1, hd, tm), lambda bi, ti: (bi, 0, ti)),
        ],
        out_shape=[
            jax.ShapeDtypeStruct((b, t, hd), BF16),
            jax.ShapeDtypeStruct((b, t, hd), BF16),
            jax.ShapeDtypeStruct((b, hd, t), BF16),
        ],
        compiler_params=_params("parallel", "parallel"),
        name="nb_proj",
    )(x, mod, g.reshape(1, d), w["w_qkv"])


def _attn_kernel(*refs, key_rows, per_head, n_kv, hps, dq, q_tile, n_tiles, key_chunk):
    n_groups, k_pieces = len(key_rows), len(per_head)
    it = iter(refs)
    q_ref = next(it)
    k_refs = [[next(it) for _ in range(k_pieces)] for _ in range(n_groups)]
    v_refs = [next(it) for _ in range(n_groups)]
    o_ref = next(it)
    k_scr, v_scr, s_a, s_b, m_a, m_b = (next(it) for _ in range(6))

    r0 = 0
    for gi, rows in enumerate(key_rows):
        for j in range(n_kv):
            for pi in range(k_pieces):
                lanes = slice(j * LANES, (j + 1) * LANES) if per_head[pi] else slice(0, LANES)
                k_scr[j, r0:r0 + rows, pi * LANES:(pi + 1) * LANES] = k_refs[gi][pi][0, :, lanes]
            v_scr[j, :, r0:r0 + rows] = v_refs[gi][0, j * LANES:(j + 1) * LANES, :]
        r0 += rows

    def rows_of(tile):
        return pl.ds(pl.multiple_of(tile * q_tile, q_tile), q_tile)

    def kv_of(head):
        return head if n_kv > 1 else 0

    def scores(tile, head, s_buf, m_buf):
        q = q_ref[0, rows_of(tile), head * dq:(head + 1) * dq]
        s_t = _dot_nt(k_scr[kv_of(head)], q)
        s_buf[...] = s_t
        m_buf[...] = jnp.max(s_t, axis=0, keepdims=True)

    def softmax_pv(tile, head, s_buf, m_buf):
        p = jnp.exp2(s_buf[...] - m_buf[...])
        l = jnp.sum(p, axis=0, keepdims=True)
        o_t = _dot(v_scr[kv_of(head)], p.astype(BF16)) / l
        o_ref[0, rows_of(tile), head * LANES:(head + 1) * LANES] = o_t.T.astype(BF16)

    scores(0, 0, s_a, m_a)
    if hps * n_tiles == 1:
        softmax_pv(0, 0, s_a, m_a)
        return

    s_rows = s_a.shape[0]
    bufs = ((s_a, m_a), (s_b, m_b))

    def phase(tile_n, head_n, s_next, m_next, tile_c, head_c, s_cur, m_cur_buf):
        if not key_chunk:
            scores(tile_n, head_n, s_next, m_next)
            softmax_pv(tile_c, head_c, s_cur, m_cur_buf)
            return
        q = q_ref[0, rows_of(tile_n), head_n * dq:(head_n + 1) * dq]
        m_cur = m_cur_buf[...]
        m8 = l8 = acc = None
        for c0 in range(0, s_rows, key_chunk):
            rows = slice(c0, min(c0 + key_chunk, s_rows))
            s_c = _dot_nt(k_scr[kv_of(head_n), rows, :], q)
            s_next[rows, :] = s_c
            mc = jnp.max(s_c.reshape(-1, 8, q_tile), axis=0)
            m8 = mc if m8 is None else jnp.maximum(m8, mc)
            p = jnp.exp2(s_cur[rows, :] - m_cur)
            lc = jnp.sum(p.reshape(-1, 8, q_tile), axis=0)
            l8 = lc if l8 is None else l8 + lc
            a = _dot(v_scr[kv_of(head_c), :, rows], p.astype(BF16))
            acc = a if acc is None else acc + a
        m_next[...] = jnp.max(m8, axis=0, keepdims=True)
        l = jnp.sum(l8, axis=0, keepdims=True)
        o_ref[0, rows_of(tile_c), head_c * LANES:(head_c + 1) * LANES] = (acc / l).T.astype(BF16)

    def one_tile(tile, last):
        for head in range(hps):
            cur = (tile, head, *bufs[head % 2])
            if head + 1 < hps:
                phase(tile, head + 1, *bufs[(head + 1) % 2], *cur)
            elif not last:
                phase(tile + 1, 0, *bufs[(head + 1) % 2], *cur)
            else:
                softmax_pv(*cur)

    assert hps % 2 == 0

    def body(tile, carry):
        one_tile(tile, False)
        return carry

    lax.fori_loop(0, n_tiles - 1, body, 0)
    one_tile(n_tiles - 1, True)


def _attention(q, k_groups, v_groups, *, heads, hps, per_head, share_kv, key_chunk):
    b, tq, qw = q.shape
    dq = qw // heads
    q_tile = min(Q_TILE, tq)
    n_tiles = tq // q_tile
    n_kv = 1 if share_kv else hps
    assert heads % hps == 0 and (hps * n_tiles == 1 or hps % 2 == 0)
    key_rows = tuple(g[0].shape[1] for g in k_groups)
    k_pieces = len(per_head)
    s_total = sum(key_rows)
    in_specs = [pl.BlockSpec((1, tq, hps * dq), lambda bi, hg: (bi, 0, hg))]
    args = [q]
    for grp in k_groups:
        for pi, arr in enumerate(grp):
            width = n_kv * LANES if per_head[pi] else LANES
            moves = per_head[pi] or share_kv
            in_specs.append(pl.BlockSpec((1, arr.shape[1], width),
                                         lambda bi, hg, moves=moves: (bi, 0, hg if moves else 0)))
            args.append(arr)
    for arr in v_groups:
        in_specs.append(pl.BlockSpec((1, n_kv * LANES, arr.shape[2]), lambda bi, hg: (bi, hg, 0)))
        args.append(arr)
    return pl.pallas_call(
        functools.partial(_attn_kernel, key_rows=key_rows, per_head=tuple(per_head), n_kv=n_kv, hps=hps,
                          dq=dq, q_tile=q_tile, n_tiles=n_tiles, key_chunk=key_chunk),
        grid=(b, heads // hps),
        in_specs=in_specs,
        out_specs=pl.BlockSpec((1, tq, hps * LANES), lambda bi, hg: (bi, 0, hg)),
        out_shape=jax.ShapeDtypeStruct((b, tq, heads * LANES), BF16),
        scratch_shapes=[
            pltpu.VMEM((n_kv, s_total, k_pieces * LANES), BF16), pltpu.VMEM((n_kv, LANES, s_total), BF16),
            pltpu.VMEM((s_total, q_tile), F32), pltpu.VMEM((s_total, q_tile), F32),
            pltpu.VMEM((1, q_tile), F32), pltpu.VMEM((1, q_tile), F32),
        ],
        compiler_params=_params("parallel", "parallel"),
        name="attention",
    )(*args)


def _nb_attn_kernel(*refs, has_window, rows, tile, n_groups, pps):
    it = iter(refs)
    q_ref, kc_ref, vc_ref = next(it), next(it), next(it)
    if has_window:
        kl_ref, vl_ref, bias_ref = next(it), next(it), next(it)
    o_ref = next(it)
    s_a, s_b, m_a, m_b = (next(it) for _ in range(4))
    bufs = ((s_a, m_a), (s_b, m_b))
    c = kc_ref.shape[1]
    win = NB_KEY_ROWS * GRID_W
    lane = lax.broadcasted_iota(jnp.int32, (1, LANES), 1)

    def tile_rows(g):
        return pl.ds(pl.multiple_of(g * tile, tile), tile)

    def lanes_of(j):
        return slice(j * LANES, (j + 1) * LANES)

    def window_start(g):
        first_row = jnp.clip(NB_ROWS * g - C_WIN_ROWS // 2, 0, rows - NB_KEY_ROWS)
        return pl.multiple_of(first_row * GRID_W, 2 * LANES)

    def two_heads(g, j):
        q = q_ref[0, tile_rows(g), lanes_of(j)]
        zero = jnp.zeros_like(q)
        return jnp.concatenate([jnp.where(lane < C_HEAD_DIM, q, zero), jnp.where(lane >= C_HEAD_DIM, q, zero)],
                               axis=0)

    def finish(o2, l, g, j):
        o_t = jnp.concatenate([o2[e * C_HEAD_DIM:(e + 1) * C_HEAD_DIM, e * tile:(e + 1) * tile]
                               / l[:, e * tile:(e + 1) * tile] for e in range(2)], axis=0)
        o_ref[0, tile_rows(g), lanes_of(j)] = o_t.T.astype(BF16)

    def chunks(g_next, j_next, g_cur, j_cur):
        yield kc_ref[0, :, lanes_of(j_next)], None, slice(0, c), vc_ref[0, lanes_of(j_cur), :]
        if has_window:
            variant = (0 if g_next == 0 else 2 if g_next == n_groups - 1 else 1) if isinstance(g_next, int) else 1
            w_next, w_cur = window_start(g_next), window_start(g_cur)
            for c0 in range(0, win, NB_CHUNK):
                bias = [bias_ref[variant, 2 * j_next + e, c0:c0 + NB_CHUNK, :] for e in range(2)]
                yield (kl_ref[0, pl.ds(w_next + c0, NB_CHUNK), lanes_of(j_next)], bias,
                       slice(c + c0, c + c0 + NB_CHUNK), vl_ref[0, lanes_of(j_cur), pl.ds(w_cur + c0, NB_CHUNK)])

    def phase(g_next, j_next, s_next, m_next, g_cur, j_cur, s_cur, m_cur_buf):
        q2 = two_heads(g_next, j_next) if g_next is not None else None
        m_cur = m_cur_buf[...] if g_cur is not None else None
        m8 = l8 = o2 = None
        gn = g_next if g_next is not None else g_cur
        gc = g_cur if g_cur is not None else g_next
        for k, bias, rws, v_t in chunks(gn, j_next if g_next is not None else j_cur,
                                        gc, j_cur if g_cur is not None else j_next):
            if g_next is not None:
                s_k = _dot_nt(k, q2)
                if bias is not None:
                    s_k = jnp.concatenate([s_k[:, e * tile:(e + 1) * tile] + bias[e] for e in range(2)], axis=1)
                s_next[rws, :] = s_k
                mk = jnp.max(s_k.reshape(-1, 8, 2 * tile), axis=0)
                m8 = mk if m8 is None else jnp.maximum(m8, mk)
            if g_cur is not None:
                p = jnp.exp2(s_cur[rws, :] - m_cur)
                lk = jnp.sum(p.reshape(-1, 8, 2 * tile), axis=0)
                l8 = lk if l8 is None else l8 + lk
                ok = _dot(v_t, p.astype(BF16))
                o2 = ok if o2 is None else o2 + ok
        if g_next is not None:
            m_next[...] = jnp.max(m8, axis=0, keepdims=True)
        if g_cur is not None:
            finish(o2, jnp.sum(l8, axis=0, keepdims=True), g_cur, j_cur)

    def one_group(g, last):
        for j in range(pps):
            cur = (g, j, *bufs[j % 2])
            nxt_bufs = bufs[(j + 1) % 2]
            if j + 1 < pps:
                phase(g, j + 1, *nxt_bufs, *cur)
            elif not last:
                phase(g + 1, 0, *nxt_bufs, *cur)
            else:
                phase(None, None, None, None, *cur)

    assert pps % 2 == 0
    phase(0, 0, *bufs[0], None, None, None, None)

    def body(g, carry):
        one_group(g, False)
        return carry

    peeled_tail = min(2, n_groups - 1)
    one_group(0, n_groups == 1)
    lax.fori_loop(1, n_groups - peeled_tail, body, 0)
    for g in range(max(1, n_groups - peeled_tail), n_groups):
        one_group(g, g == n_groups - 1)


def _nb_attention(q, k_ctx, vt_ctx, k_lat=None, vt_lat=None, bias=None):
    b, tq, hw = q.shape
    pairs = hw // LANES
    has_window = k_lat is not None
    c = k_ctx.shape[1]
    if has_window:
        t = k_lat.shape[1]
        rows = t // GRID_W
        tile = NB_ROWS * GRID_W
        keys = c + NB_KEY_ROWS * GRID_W
        pps = NB_PAIRS_PER_STEP
    else:
        rows, tile, keys, pps = 0, tq, c, pairs
    n_groups = tq // tile
    w = pps * LANES
    in_specs = [
        pl.BlockSpec((1, tq, w), lambda p, bi: (bi, 0, p)),
        pl.BlockSpec((1, c, w), lambda p, bi: (bi, 0, p)),
        pl.BlockSpec((1, w, c), lambda p, bi: (bi, p, 0)),
    ]
    args = [q, k_ctx, vt_ctx]
    if has_window:
        in_specs += [
            pl.BlockSpec((1, t, w), lambda p, bi: (bi, 0, p)),
            pl.BlockSpec((1, w, t), lambda p, bi: (bi, p, 0)),
            pl.BlockSpec((3, 2 * pps, NB_KEY_ROWS * GRID_W, tile), lambda p, bi: (0, p, 0, 0),
                         pipeline_mode=pl.Buffered(1)),
        ]
        args += [k_lat, vt_lat, bias]
    return pl.pallas_call(
        functools.partial(_nb_attn_kernel, has_window=has_window, rows=rows, tile=tile, n_groups=n_groups, pps=pps),
        grid=(pairs // pps, b),
        in_specs=in_specs,
        out_specs=pl.BlockSpec((1, tq, w), lambda p, bi: (bi, 0, p)),
        out_shape=jax.ShapeDtypeStruct((b, tq, hw), BF16),
        scratch_shapes=[pltpu.VMEM((keys, 2 * tile), F32), pltpu.VMEM((keys, 2 * tile), F32),
                        pltpu.VMEM((1, 2 * tile), F32), pltpu.VMEM((1, 2 * tile), F32)],
        compiler_params=_params("parallel", "parallel"),
        name="nb_attention",
    )(*args)


def _nb_bias_table(rpb, rows):
    heads = rpb.shape[0]
    tile_q = NB_ROWS * GRID_W
    return pl.pallas_call(
        functools.partial(_nb_bias_kernel, rows=rows),
        grid=(heads,),
        in_specs=[pl.BlockSpec(memory_space=pltpu.SMEM)],
        out_specs=pl.BlockSpec((3, 1, NB_KEY_ROWS * GRID_W, tile_q), lambda h: (0, h, 0, 0)),
        out_shape=jax.ShapeDtypeStruct((3, heads, NB_KEY_ROWS * GRID_W, tile_q), F32),
        compiler_params=_params("parallel"),
        name="nb_bias",
    )(rpb.reshape(-1))


def _nb_bias_kernel(rpb_ref, o_ref, *, rows):
    n_a, n_b = 2 * C_WIN_ROWS - 1, 2 * C_WIN_COLS - 1
    tile_q = NB_ROWS * GRID_W
    shape = (GRID_W, tile_q)
    kc = lax.broadcasted_iota(jnp.int32, shape, 0)
    lane = lax.broadcasted_iota(jnp.int32, shape, 1)
    qc = lane % GRID_W
    qi = lane // GRID_W
    c0 = jnp.clip(qc - C_WIN_COLS // 2, 0, GRID_W - C_WIN_COLS)
    col_ok = (kc >= c0) & (kc < c0 + C_WIN_COLS)
    dcol = kc - qc + (C_WIN_COLS - 1)
    base = pl.program_id(0) * (n_a * n_b)
    masked = jnp.full(shape, MASK_VALUE, F32)
    planes = []
    for a in range(n_a):
        acc = masked
        for bb in range(n_b):
            acc = jnp.where(dcol == bb, rpb_ref[base + a * n_b + bb], acc)
        planes.append(jnp.where(col_ok, acc * LOG2E, MASK_VALUE))
    groups = rows // NB_ROWS
    for v, g in enumerate((0, 1, groups - 1)):
        first_key_row = min(max(NB_ROWS * g - C_WIN_ROWS // 2, 0), rows - NB_KEY_ROWS)
        for j in range(NB_KEY_ROWS):
            kr = first_key_row + j
            blk = masked
            for i in range(NB_ROWS):
                qr = NB_ROWS * g + i
                r0 = min(max(qr - C_WIN_ROWS // 2, 0), rows - C_WIN_ROWS)
                if r0 <= kr < r0 + C_WIN_ROWS:
                    blk = jnp.where(qi == i, planes[kr - qr + C_WIN_ROWS - 1], blk)
            o_ref[v, 0, j * GRID_W:(j + 1) * GRID_W, :] = blk


def _rope_tables(rows, rot_dim):
    n = rot_dim // 4
    inv_freq = ROPE_THETA ** (-jnp.arange(n, dtype=F32) / n)
    t = jnp.arange(rows * GRID_W, dtype=jnp.int32)
    r = (t // GRID_W).astype(F32)
    col = (t % GRID_W).astype(F32)
    ang = jnp.concatenate([r[:, None] * inv_freq[None, :], col[:, None] * inv_freq[None, :]], axis=-1)
    cos, sin = jnp.cos(ang), jnp.sin(ang)
    reps = LANES // rot_dim
    return (jnp.tile(jnp.concatenate([cos, cos], axis=-1), (1, reps)),
            jnp.tile(jnp.concatenate([-sin, sin], axis=-1), (1, reps)))


def _mla_weights(w_in, q_norm, kv_norm, w_uq, w_ukv, w_o):
    rank = A_Q_RANK + A_KV_RANK
    k_r = w_in[:, rank:]
    uq = w_uq.reshape(A_Q_RANK, A_HEADS, A_NOPE + A_ROPE)
    ukv = w_ukv.reshape(A_KV_RANK, A_HEADS, A_NOPE + A_V)
    return {
        "w_in": jnp.concatenate([w_in[:, :rank], k_r, k_r], axis=1).astype(BF16),
        "q_norm": q_norm, "kv_norm": kv_norm,
        "w_uq": jnp.concatenate([uq[:, :, :A_NOPE].reshape(A_Q_RANK, -1),
                                 uq[:, :, A_NOPE:].reshape(A_Q_RANK, -1)], axis=1).astype(BF16),
        "w_uk": ukv[:, :, :A_NOPE].reshape(A_KV_RANK, -1).astype(BF16),
        "w_uvt": ukv[:, :, A_NOPE:].reshape(A_KV_RANK, -1).T.astype(BF16),
        "w_o": w_o.astype(BF16),
    }


def kernel(x, c, ctx, c_ctx, norm_g, w_mod, b_mod, ffn1_w13, ffn1_w2, ffn2_w13, ffn2_w2, a_w_in, a_q_norm, a_kv_norm, a_w_uq, a_w_ukv, a_w_o, b_w_qkv, b_q_norm, b_k_norm, b_w_o, c_w_qkv, c_rpb, c_w_o, final_norm_g):
    b, t, d = x.shape
    depth = w_mod.shape[0]
    rows = t // GRID_W
    assert b < MOD_ROWS and t % (NB_ROWS * GRID_W) == 0 and rows >= NB_KEY_ROWS + NB_ROWS

    c_rows = jnp.zeros((MOD_ROWS, d), F32).at[:b].set(c).at[b].set(c_ctx)
    mod_all = _modulation(c_rows, w_mod, b_mod).reshape(depth, MOD_ROWS, N_MOD, d)
    lat_row = lambda bi: bi
    ctx_row = lambda bi: b

    rope_a = _rope_tables(rows, A_ROPE)
    rope_b = _rope_tables(rows, B_HEAD_DIM)

    w13_1, w2_1 = ffn1_w13.astype(BF16), ffn1_w2.astype(BF16)
    w13_2, w2_2 = ffn2_w13.astype(BF16), ffn2_w2.astype(BF16)

    xc = ctx
    for i in range(depth):
        ctx_out = i < depth - 1
        last = i == depth - 1
        mod = mod_all[i]
        kind, j = i % N_MIXERS, i // N_MIXERS
        x = _ffn(x, mod, lat_row, norm_g[i, 0], w13_1, w2_1, i, k0=0)
        xc = _ffn(xc, mod, ctx_row, norm_g[i, 0], w13_1, w2_1, i, k0=0)

        if kind == 0:
            w = _mla_weights(a_w_in[j], a_q_norm[j], a_kv_norm[j], a_w_uq[j], a_w_ukv[j], a_w_o[j])
            q, kn, kr, vt = _mla_proj(x, mod, lat_row, norm_g[i, 1], w, rope_a)
            qc, knc, krc, vtc = _mla_proj(xc, mod, ctx_row, norm_g[i, 1], w, None)
            cfg = dict(heads=A_HEADS, hps=A_HEADS_PER_STEP, per_head=(True, False), share_kv=False,
                       key_chunk=MLA_KEY_CHUNK)
            o = _attention(q, [[knc, krc], [kn, kr]], [vtc, vt], **cfg)
            if ctx_out:
                oc = _attention(qc, [[knc, krc]], [vtc], **dict(cfg, hps=A_HEADS))
            w_o = w["w_o"]
        elif kind == 1:
            w = {"w_qkv": b_w_qkv[j].astype(BF16), "q_norm": b_q_norm[j], "k_norm": b_k_norm[j]}
            q, k, vt = _gqa_proj(x, mod, lat_row, norm_g[i, 1], w, rope_b)
            qc, kc, vtc = _gqa_proj(xc, mod, ctx_row, norm_g[i, 1], w, None)
            cfg = dict(heads=B_HEADS, hps=B_HEADS // B_KV_HEADS, per_head=(False,), share_kv=True,
                       key_chunk=GQA_KEY_CHUNK)
            o = _attention(q, [[kc], [k]], [vtc, vt], **cfg)
            if ctx_out:
                oc = _attention(qc, [[kc]], [vtc], **cfg)
            w_o = b_w_o[j].astype(BF16)
        else:
            w = {"w_qkv": c_w_qkv[j].astype(BF16)}
            q, k, vt = _nb_proj(x, mod, lat_row, norm_g[i, 1], w)
            qc, kc, vtc = _nb_proj(xc, mod, ctx_row, norm_g[i, 1], w)
            o = _nb_attention(q, kc, vtc, k, vt, _nb_bias_table(c_rpb[j], rows))
            if ctx_out:
                oc = _nb_attention(qc, kc, vtc)
            w_o = c_w_o[j].astype(BF16)

        x = _ffn(x, mod, lat_row, norm_g[i, 2], w13_2, w2_2, i, k0=6, attn=o, w_o=w_o,
                 final_g=final_norm_g if last else None)
        if ctx_out:
            xc = _ffn(xc, mod, ctx_row, norm_g[i, 2], w13_2, w2_2, i, k0=6, attn=oc, w_o=w_o)
    return x
```
